```python
import math
import jax, jax.numpy as jnp
from jax import lax
import numpy as np

D_MODEL = 1024
BATCH = 8
SEQ = 2048
DEPTH = 2

HEAD_DIM = 64
SB_HEADS = 8
SB_WIDTH = SB_HEADS * HEAD_DIM
CONV_CH = 512
CONV_WIDTH = 3
ATT_HEADS = 16
KV_HEADS = 4
GQA_REP = ATT_HEADS // KV_HEADS
ATT_WIDTH = ATT_HEADS * HEAD_DIM
KV_WIDTH = KV_HEADS * HEAD_DIM
IDX_HEADS = 8
IDX_DIM = 64
TOPK_MAX = 256
REL_BUCKETS = 32
REL_MAX_DIST = 128
FFN_HIDDEN = ((-(-8 * D_MODEL // 3) + 255) // 256) * 256
QBLOCK = 128
EPS = 1e-6
N_EVEN = (DEPTH + 1) // 2
N_ODD = DEPTH // 2
EVEN_IN = 3 * SB_WIDTH + 3 * CONV_CH
ODD_IN = ATT_WIDTH + 2 * KV_WIDTH + IDX_HEADS * IDX_DIM + IDX_DIM + IDX_HEADS

kernel_name = "hybrid_stickbreak_shortconv_dsa_block"


def rms_norm(x, g):
    xf = x.astype(jnp.float32)
    y = xf * lax.rsqrt(jnp.mean(xf * xf, axis=-1, keepdims=True) + EPS)
    return (y * g.astype(jnp.float32)).astype(x.dtype)


def offsets(widths):
    out, acc = [], 0
    for w in widths[:-1]:
        acc += w
        out.append(acc)
    return out


def to_blocks(a):
    b, s = a.shape[:2]
    a = a.reshape((b, s // QBLOCK, QBLOCK) + a.shape[2:])
    return jnp.moveaxis(a, 1, 0)


def from_blocks(a):
    a = jnp.moveaxis(a, 0, 1)
    return a.reshape((a.shape[0], a.shape[1] * a.shape[2]) + a.shape[3:])


def stick_breaking_attention(q, k, v):
    s = q.shape[1]
    scale = HEAD_DIM ** -0.5
    kpos = jnp.arange(s)

    def block(args):
        qb, qpos = args
        z = jnp.einsum('bqhd,bkhd->bhqk', qb, k).astype(jnp.float32) * scale
        mask = kpos[None, :] < qpos[:, None]
        log_beta = jax.nn.log_sigmoid(z)
        log_1m_beta = jnp.where(mask, log_beta - z, 0.0)
        after = lax.cumsum(log_1m_beta, axis=3, reverse=True) - log_1m_beta
        w = jnp.where(mask, jnp.exp(log_beta + after), 0.0)
        return jnp.einsum('bhqk,bkhd->bqhd', w.astype(v.dtype), v)

    qpos = jnp.arange(s).reshape(-1, QBLOCK)
    return from_blocks(lax.map(block, (to_blocks(q), qpos)))


def short_conv_mixer(b_gate, c_gate, u, conv_w):
    g = c_gate * u
    y = lax.conv_general_dilated(
        g, conv_w[:, None, :], window_strides=(1,),
        padding=[(CONV_WIDTH - 1, 0)],
        dimension_numbers=('NWC', 'WIO', 'NWC'),
        feature_group_count=CONV_CH)
    return b_gate * y


def rel_bucket(dist):
    exact = REL_BUCKETS // 2
    d_f = jnp.maximum(dist, 1).astype(jnp.float32)
    large = exact + (jnp.log(d_f / exact) / math.log(REL_MAX_DIST / exact)
                     * (REL_BUCKETS - exact)).astype(jnp.int32)
    large = jnp.minimum(large, REL_BUCKETS - 1)
    return jnp.where(dist < exact, dist, large)


def dsa_attention(q, k, v, q_idx, k_idx, w_idx, rel_bias):
    s = q.shape[1]
    topk = min(TOPK_MAX, s // 4)
    kpos = jnp.arange(s)
    gather = jax.vmap(lambda a, i: a[i])

    def block(args):
        qb, qib, wb, qpos = args
        dots = jnp.einsum('bqhd,bkd->bqhk', qib, k_idx).astype(jnp.float32) * IDX_DIM ** -0.5
        score = jnp.einsum('bqh,bqhk->bqk', wb.astype(jnp.float32) * IDX_HEADS ** -0.5,
                           jax.nn.relu(dots))
        causal = kpos[None, :] <= qpos[:, None]
        score = jnp.where(causal[None], score, -jnp.inf)
        _, idx = lax.top_k(score, topk)
        valid = idx <= qpos[None, :, None]
        k_sel = gather(k, idx)
        v_sel = gather(v, idx)
        bucket = rel_bucket(jnp.maximum(qpos[None, :, None] - idx, 0))
        bias = rel_bias[bucket].reshape(idx.shape + (KV_HEADS, GQA_REP))
        bias = jnp.moveaxis(bias, 2, -1)
        logits = (jnp.einsum('bqgrd,bqkgd->bqgrk', qb, k_sel).astype(jnp.float32)
                  * HEAD_DIM ** -0.5 + bias.astype(jnp.float32))
        logits = jnp.where(valid[:, :, None, None, :], logits, -jnp.inf)
        p = jax.nn.softmax(logits, axis=-1)
        return jnp.einsum('bqgrk,bqkgd->bqgrd', p.astype(v.dtype), v_sel)

    qpos = jnp.arange(s).reshape(-1, QBLOCK)
    out = lax.map(block, (to_blocks(q), to_blocks(q_idx), to_blocks(w_idx), qpos))
    return from_blocks(out)


def even_mixer(h, w_in, conv_w, w_out):
    b, s, _ = h.shape
    proj = h @ w_in
    q, k, v, bg, cg, u = jnp.split(
        proj, offsets([SB_WIDTH, SB_WIDTH, SB_WIDTH, CONV_CH, CONV_CH, CONV_CH]), axis=-1)
    heads = lambda a: a.reshape(b, s, SB_HEADS, HEAD_DIM)
    a_out = stick_breaking_attention(heads(q), heads(k), heads(v)).reshape(b, s, SB_WIDTH)
    b_out = short_conv_mixer(bg, cg, u, conv_w)
    return jnp.concatenate([a_out, b_out], axis=-1) @ w_out


def odd_mixer(h, w_in, q_gain, k_gain, w_out, rel_bias):
    b, s, _ = h.shape
    proj = h @ w_in
    q, k, v, qi, ki, wi = jnp.split(
        proj, offsets([ATT_WIDTH, KV_WIDTH, KV_WIDTH, IDX_HEADS * IDX_DIM, IDX_DIM, IDX_HEADS]),
        axis=-1)
    q = rms_norm(q.reshape(b, s, ATT_HEADS, HEAD_DIM), q_gain)
    q = q.reshape(b, s, KV_HEADS, GQA_REP, HEAD_DIM)
    k = rms_norm(k.reshape(b, s, KV_HEADS, HEAD_DIM), k_gain)
    v = v.reshape(b, s, KV_HEADS, HEAD_DIM)
    qi = qi.reshape(b, s, IDX_HEADS, IDX_DIM)
    out = dsa_attention(q, k, v, qi, ki, wi, rel_bias).reshape(b, s, ATT_WIDTH)
    return out @ w_out


def swiglu(h, w_gate, w_up, w_down):
    return (jax.nn.silu(h @ w_gate) * (h @ w_up)) @ w_down


def setup_inputs(seed: int = 0) -> dict:
    key = jax.random.key(seed)
    ks = jax.random.split(key, 14)
    f32 = jnp.float32
    nrm = lambda k, shape, fan_in: jax.random.normal(k, shape, f32) * fan_in ** -0.5
    return {
        'x': jax.random.normal(ks[0], (BATCH, SEQ, D_MODEL), f32),
        'norm_mix': 1.0 + 0.01 * jax.random.normal(ks[1], (DEPTH, D_MODEL), f32),
        'norm_ffn': 1.0 + 0.01 * jax.random.normal(ks[2], (DEPTH, D_MODEL), f32),
        'ev_w_in': nrm(ks[3], (N_EVEN, D_MODEL, EVEN_IN), D_MODEL),
        'ev_conv_w': nrm(ks[4], (N_EVEN, CONV_WIDTH, CONV_CH), CONV_WIDTH),
        'ev_w_out': nrm(ks[5], (N_EVEN, SB_WIDTH + CONV_CH, D_MODEL), SB_WIDTH + CONV_CH),
        'od_w_in': nrm(ks[6], (N_ODD, D_MODEL, ODD_IN), D_MODEL),
        'od_q_gain': 1.0 + 0.01 * jax.random.normal(ks[7], (N_ODD, HEAD_DIM), f32),
        'od_k_gain': 1.0 + 0.01 * jax.random.normal(ks[8], (N_ODD, HEAD_DIM), f32),
        'od_w_out': nrm(ks[9], (N_ODD, ATT_WIDTH, D_MODEL), ATT_WIDTH),
        'rel_bias': 0.5 * jax.random.normal(ks[10], (REL_BUCKETS, ATT_HEADS), f32),
        'ffn_w_gate': nrm(ks[11], (DEPTH, D_MODEL, FFN_HIDDEN), D_MODEL),
        'ffn_w_up': nrm(ks[12], (DEPTH, D_MODEL, FFN_HIDDEN), D_MODEL),
        'ffn_w_down': nrm(ks[13], (DEPTH, FFN_HIDDEN, D_MODEL), FFN_HIDDEN),
    }


def reference(x, norm_mix, norm_ffn, ev_w_in, ev_conv_w, ev_w_out, od_w_in, od_q_gain,
              od_k_gain, od_w_out, rel_bias, ffn_w_gate, ffn_w_up, ffn_w_down):
    for layer in range(DEPTH):
        i = layer // 2
        h = rms_norm(x, norm_mix[layer])
        if layer % 2 == 0:
            x = x + even_mixer(h, ev_w_in[i], ev_conv_w[i], ev_w_out[i])
        else:
            x = x + odd_mixer(h, od_w_in[i], od_q_gain[i], od_k_gain[i], od_w_out[i], rel_bias)
        h = rms_norm(x, norm_ffn[layer])
        x = x + swiglu(h, ffn_w_gate[layer], ffn_w_up[layer], ffn_w_down[layer])
    return x
```

```python
import functools
import math

import jax
import jax.numpy as jnp
from jax import lax
from jax.experimental import pallas as pl
from jax.experimental.pallas import tpu as pltpu

F32 = jnp.float32
BF16 = jnp.bfloat16
I32 = jnp.int32

D_MODEL = 1024
SEQ = 2048
HEAD_DIM = 64
SB_WIDTH = 512
CONV_CH = 512
ATT_HEADS = 16
KV_HEADS = 4
GQA_REP = 4
IDX_HEADS = 8
TOPK = 256
REL_BUCKETS = 32
REL_MAX_DIST = 128
FFN_HIDDEN = 2816
EPS = 1e-6
ODD_IN_PAD = 2176

LANES = 128
VMEM_LIMIT = 56 * 1024 * 1024
NEG = -1e30
INT_MIN = -(2 ** 31)

NT_DIMS = (((1,), (1,)), ((), ()))


def _rms(x, g):
    ms = jnp.mean(x * x, axis=-1, keepdims=True)
    return x * lax.rsqrt(ms + EPS) * g


def _norm_matmul_kernel(x_ref, g_ref, w_ref, o_ref, *, tn):
    h = _rms(x_ref[...], g_ref[...]).astype(BF16)
    n = w_ref.shape[1]
    for start in range(0, n, tn):
        size = min(tn, n - start)
        o_ref[:, start:start + size] = jnp.dot(
            h, w_ref[:, start:start + size], preferred_element_type=F32)


def _norm_matmul(x, g, w, *, tm=512, tn=512):
    m, d = x.shape
    n = w.shape[1]
    return pl.pallas_call(
        functools.partial(_norm_matmul_kernel, tn=tn),
        out_shape=jax.ShapeDtypeStruct((m, n), F32),
        grid=(m // tm,),
        in_specs=[
            pl.BlockSpec((tm, d), lambda i: (i, 0)),
            pl.BlockSpec((1, d), lambda i: (0, 0)),
            pl.BlockSpec((d, n), lambda i: (0, 0)),
        ],
        out_specs=pl.BlockSpec((tm, n), lambda i: (i, 0)),
        compiler_params=pltpu.CompilerParams(
            dimension_semantics=("parallel",), vmem_limit_bytes=VMEM_LIMIT),
        name="norm_matmul",
    )(x, g, w)


def _sb_kernel(q_ref, k_ref, v_ref, o_ref, *, tq):
    qi = pl.program_id(2)
    lane = lax.broadcasted_iota(I32, (tq, LANES), 1)
    row = lax.broadcasted_iota(I32, (tq, tq), 0)
    col = lax.broadcasted_iota(I32, (tq, tq), 1)
    later = (row > col).astype(BF16)
    causal = col < row
    q = q_ref[...] * 0.125

    outs = []
    for h in range(2):
        in_head = (lane >= HEAD_DIM * h) & (lane < HEAD_DIM * (h + 1))
        qh = jnp.where(in_head, q, 0.0).astype(BF16)

        def block(kb, carry, acc, masked, qh=qh):
            start = pl.multiple_of(kb * tq, tq)
            kblk = k_ref[pl.ds(start, tq), :].astype(BF16)
            vblk = v_ref[pl.ds(start, tq), :].astype(BF16)
            z = lax.dot_general(qh, kblk, NT_DIMS, preferred_element_type=F32)
            sp = jnp.log1p(jnp.exp(-jnp.abs(z)))
            log_beta = jnp.minimum(z, 0.0) - sp
            log_1m = -jnp.maximum(z, 0.0) - sp
            if masked:
                log_1m = jnp.where(causal, log_1m, 0.0)
            hi = log_1m.astype(BF16)
            lo = (log_1m - hi.astype(F32)).astype(BF16)
            after = (jnp.dot(hi, later, preferred_element_type=F32)
                     + jnp.dot(lo, later, preferred_element_type=F32))
            w = jnp.exp(log_beta + after + carry)
            if masked:
                w = jnp.where(causal, w, 0.0)
            acc = acc + jnp.dot(w.astype(BF16), vblk, preferred_element_type=F32)
            carry = carry + jnp.sum(log_1m, axis=1, keepdims=True)
            return carry, acc

        carry = jnp.zeros((tq, 1), F32)
        acc = jnp.zeros((tq, LANES), F32)
        carry, acc = block(qi, carry, acc, True)

        def body(i, st, block=block):
            return block(qi - 1 - i, st[0], st[1], False)

        carry, acc = lax.fori_loop(0, qi, body, (carry, acc))
        outs.append(acc)
    o_ref[...] = jnp.where(lane < HEAD_DIM, outs[0], outs[1])


def _stickbreak(proj, *, tq=256):
    b, s, _ = proj.shape
    n_pairs = SB_WIDTH // LANES
    return pl.pallas_call(
        functools.partial(_sb_kernel, tq=tq),
        out_shape=jax.ShapeDtypeStruct((b, s, SB_WIDTH), F32),
        grid=(b, n_pairs, s // tq),
        in_specs=[
            pl.BlockSpec((None, tq, LANES), lambda bi, hp, qi: (bi, qi, hp)),
            pl.BlockSpec((None, s, LANES), lambda bi, hp, qi: (bi, 0, n_pairs + hp)),
            pl.BlockSpec((None, s, LANES), lambda bi, hp, qi: (bi, 0, 2 * n_pairs + hp)),
        ],
        out_specs=pl.BlockSpec((None, tq, LANES), lambda bi, hp, qi: (bi, qi, hp)),
        compiler_params=pltpu.CompilerParams(
            dimension_semantics=("parallel", "parallel", "arbitrary"),
            vmem_limit_bytes=VMEM_LIMIT),
        name="stickbreak",
    )(proj, proj, proj)


def _mix0_kernel(a_ref, bg_ref, cg_ref, u_ref, cgh_ref, uh_ref, cw_ref, w_ref, x_ref, o_ref,
                 *, tm):
    i = pl.program_id(1)
    g = cg_ref[...] * u_ref[...]
    gh = cgh_ref[...] * uh_ref[...]
    gh = jnp.where(i == 0, 0.0, gh)
    row = lax.broadcasted_iota(I32, g.shape, 0)
    g1 = jnp.where(row == 0, gh[7:8, :], pltpu.roll(g, 1, axis=0))
    g2 = pltpu.roll(g, 2, axis=0)
    g2 = jnp.where(row == 0, gh[6:7, :], jnp.where(row == 1, gh[7:8, :], g2))
    cw = cw_ref[...]
    y = bg_ref[...] * (cw[0:1, :] * g2 + cw[1:2, :] * g1 + cw[2:3, :] * g)
    acc = jnp.dot(a_ref[...].astype(BF16), w_ref[0:SB_WIDTH, :], preferred_element_type=F32)
    acc = acc + jnp.dot(y.astype(BF16), w_ref[SB_WIDTH:, :], preferred_element_type=F32)
    o_ref[...] = x_ref[...] + acc


def _mix0(a_out, proj, conv_w, w_out, x, *, tm=512):
    b, s, d = x.shape
    halo = lambda col: (lambda bi, i: (bi, jnp.maximum(i * (tm // 8) - 1, 0), col))
    return pl.pallas_call(
        functools.partial(_mix0_kernel, tm=tm),
        out_shape=jax.ShapeDtypeStruct((b, s, d), F32),
        grid=(b, s // tm),
        in_specs=[
            pl.BlockSpec((None, tm, SB_WIDTH), lambda bi, i: (bi, i, 0)),
            pl.BlockSpec((None, tm, CONV_CH), lambda bi, i: (bi, i, 3)),
            pl.BlockSpec((None, tm, CONV_CH), lambda bi, i: (bi, i, 4)),
            pl.BlockSpec((None, tm, CONV_CH), lambda bi, i: (bi, i, 5)),
            pl.BlockSpec((None, 8, CONV_CH), halo(4)),
            pl.BlockSpec((None, 8, CONV_CH), halo(5)),
            pl.BlockSpec((3, CONV_CH), lambda bi, i: (0, 0)),
            pl.BlockSpec((SB_WIDTH + CONV_CH, d), lambda bi, i: (0, 0)),
            pl.BlockSpec((None, tm, d), lambda bi, i: (bi, i, 0)),
        ],
        out_specs=pl.BlockSpec((None, tm, d), lambda bi, i: (bi, i, 0)),
        compiler_params=pltpu.CompilerParams(
            dimension_semantics=("parallel", "arbitrary"), vmem_limit_bytes=VMEM_LIMIT),
        name="mix0_out",
    )(a_out, proj, proj, proj, proj, proj, conv_w, w_out, x)


def _proj_res_kernel(a_ref, w_ref, x_ref, o_ref):
    o_ref[...] = x_ref[...] + jnp.dot(a_ref[...].astype(BF16), w_ref[...],
                                       preferred_element_type=F32)


def _proj_res(a, w, x, *, tm=512):
    m, k = a.shape
    n = w.shape[1]
    return pl.pallas_call(
        _proj_res_kernel,
        out_shape=jax.ShapeDtypeStruct((m, n), F32),
        grid=(m // tm,),
        in_specs=[
            pl.BlockSpec((tm, k), lambda i: (i, 0)),
            pl.BlockSpec((k, n), lambda i: (0, 0)),
            pl.BlockSpec((tm, n), lambda i: (i, 0)),
        ],
        out_specs=pl.BlockSpec((tm, n), lambda i: (i, 0)),
        compiler_params=pltpu.CompilerParams(
            dimension_semantics=("parallel",), vmem_limit_bytes=VMEM_LIMIT),
        name="proj_residual",
    )(a, w, x)


def _swiglu_kernel(x_ref, g_ref, wg_ref, wu_ref, wd_ref, o_ref, act_ref, *, th):
    x = x_ref[...]
    h = _rms(x, g_ref[...]).astype(BF16)
    for start in range(0, FFN_HIDDEN, th):
        hg = jnp.dot(h, wg_ref[:, start:start + th], preferred_element_type=F32)
        hu = jnp.dot(h, wu_ref[:, start:start + th], preferred_element_type=F32)
        sig = 1.0 / (1.0 + jnp.exp(-hg))
        act_ref[:, start:start + th] = (hg * sig * hu).astype(BF16)
    o_ref[...] = x + jnp.dot(act_ref[...], wd_ref[...], preferred_element_type=F32)


def _swiglu(x, g, wg, wu, wd, *, tm=512, th=256):
    m, d = x.shape
    resident = pl.Buffered(1)
    return pl.pallas_call(
        functools.partial(_swiglu_kernel, th=th),
        out_shape=jax.ShapeDtypeStruct((m, d), F32),
        grid=(m // tm,),
        in_specs=[
            pl.BlockSpec((tm, d), lambda i: (i, 0)),
            pl.BlockSpec((1, d), lambda i: (0, 0)),
            pl.BlockSpec((d, FFN_HIDDEN), lambda i: (0, 0), pipeline_mode=resident),
            pl.BlockSpec((d, FFN_HIDDEN), lambda i: (0, 0), pipeline_mode=resident),
            pl.BlockSpec((FFN_HIDDEN, d), lambda i: (0, 0), pipeline_mode=resident),
        ],
        out_specs=pl.BlockSpec((tm, d), lambda i: (i, 0)),
        scratch_shapes=[pltpu.VMEM((tm, FFN_HIDDEN), BF16)],
        compiler_params=pltpu.CompilerParams(
            dimension_semantics=("parallel",), vmem_limit_bytes=VMEM_LIMIT),
        name="norm_swiglu",
    )(x, g, wg, wu, wd)


def _bias_tile_kernel(rb_ref, o_ref):
    h = pl.program_id(0)
    t = lax.broadcasted_iota(I32, (LANES, 2 * LANES), 0)
    c = lax.broadcasted_iota(I32, (LANES, 2 * LANES), 1)
    dist = jnp.maximum(LANES + t - c, 0)
    exact = REL_BUCKETS // 2
    d_f = jnp.maximum(dist, 1).astype(F32)
    large = exact + (jnp.log(d_f / exact) / math.log(REL_MAX_DIST / exact)
                     * (REL_BUCKETS - exact)).astype(I32)
    large = jnp.minimum(large, REL_BUCKETS - 1)
    bucket = jnp.where(dist < exact, dist, large)
    out = jnp.zeros((LANES, 2 * LANES), F32)
    for b in range(REL_BUCKETS):
        out = jnp.where(bucket == b, rb_ref[b, h], out)
    o_ref[...] = out


def _bias_tiles(rel_bias):
    return pl.pallas_call(
        _bias_tile_kernel,
        out_shape=jax.ShapeDtypeStruct((ATT_HEADS, LANES, 2 * LANES), F32),
        grid=(ATT_HEADS,),
        in_specs=[pl.BlockSpec(memory_space=pltpu.SMEM)],
        out_specs=pl.BlockSpec((None, LANES, 2 * LANES), lambda h: (h, 0, 0)),
        name="rel_bias_tiles",
    )(rel_bias)


def _dsa_kernel(rb_ref, q_ref, k_ref, v_ref, qi_ref, wq_ref, kik_ref, qg_ref, kg_ref, bt_ref,
                o_ref, kn_ref, vb_ref, kib_ref, key_ref, madd_ref, *, tq):
    i = pl.program_id(1)
    s_len = k_ref.shape[0]
    grp = 4
    n_grp = (i + grp) // grp
    n_tiles = n_grp * grp

    @pl.when(i == 0)
    def _prepare_keys():
        kg = kg_ref[...]
        for g in range(KV_HEADS):
            kk = k_ref[:, g * HEAD_DIM:(g + 1) * HEAD_DIM]
            kn_ref[g] = _rms(kk, kg).astype(BF16)
            vb_ref[g] = v_ref[:, g * HEAD_DIM:(g + 1) * HEAD_DIM].astype(BF16)
        kib_ref[...] = kik_ref[:, 0:HEAD_DIM].astype(BF16)

    row = lax.broadcasted_iota(I32, (tq, LANES), 0)
    col = lax.broadcasted_iota(I32, (tq, LANES), 1)
    tpos = i * tq + row

    wrow = wq_ref[...]
    qi_heads = [(qi_ref[:, h * HEAD_DIM:(h + 1) * HEAD_DIM] * 0.125).astype(BF16)
                for h in range(IDX_HEADS)]
    w_heads = [wrow[:, HEAD_DIM + h:HEAD_DIM + h + 1] * (IDX_HEADS ** -0.5)
               for h in range(IDX_HEADS)]

    def score_tile(j, _):
        kt = kib_ref[pl.ds(pl.multiple_of(j * LANES, LANES), LANES), :]
        sc = jnp.zeros((tq, LANES), F32)
        for h in range(IDX_HEADS):
            d = lax.dot_general(qi_heads[h], kt, NT_DIMS, preferred_element_type=F32)
            sc = sc + w_heads[h] * jnp.maximum(d, 0.0)
        sc = jnp.where(sc == 0.0, 0.0, sc)
        sc = jnp.where(j * LANES + col <= tpos, sc, -jnp.inf)
        bits = pltpu.bitcast(sc, I32)
        key_ref[j] = bits ^ ((bits >> 31) & 0x7FFFFFFF)
        return 0

    lax.fori_loop(0, n_tiles, score_tile, 0)

    def count_rows(pred):
        def body(gi, c):
            for u in range(grp):
                j = gi * grp + u
                c = c + pred(j, key_ref[j]).astype(I32)
            return c
        c = lax.fori_loop(0, n_grp, body, jnp.zeros((tq, LANES), I32))
        return jnp.sum(c, axis=1, keepdims=True)

    def thr_bit(b, thr):
        cand = thr + lax.shift_left(jnp.int32(1), 31 - b)
        cand_b = jnp.broadcast_to(cand, (tq, LANES))
        cnt = count_rows(lambda j, key: key >= cand_b)
        return jnp.where(cnt >= TOPK, cand, thr)

    thr = lax.fori_loop(0, 32, thr_bit, jnp.full((tq, 1), INT_MIN, I32))
    thr_b = jnp.broadcast_to(thr, (tq, LANES))
    n_gt = count_rows(lambda j, key: key > thr_b)
    need = TOPK - n_gt

    def tie_bit(b, cut):
        cand = cut + lax.shift_left(jnp.int32(1), 10 - b)
        cand_b = jnp.broadcast_to(cand, (tq, LANES))
        cnt = count_rows(lambda j, key: (key == thr_b) & (j * LANES + col < cand_b))
        return jnp.where(cnt < need, cand, cut)

    cut = lax.fori_loop(0, 11, tie_bit, jnp.zeros((tq, 1), I32))
    cut_b = jnp.broadcast_to(cut, (tq, LANES))

    def mask_tile(j, _):
        key = key_ref[j]
        spos = j * LANES + col
        sel = (key > thr_b) | ((key == thr_b) & (spos <= cut_b))
        madd_ref[j] = jnp.where(sel & (spos <= tpos), 0.0, NEG)
        return 0

    lax.fori_loop(0, i + 1, mask_tile, 0)

    qg = qg_ref[...]
    for g in range(KV_HEADS):
        heads = [g * GQA_REP + r for r in range(GQA_REP)]
        qs = jnp.concatenate(
            [(_rms(q_ref[:, h * HEAD_DIM:(h + 1) * HEAD_DIM], qg) * 0.125).astype(BF16)
             for h in heads], axis=0)

        def attend(j, st, bias, g=g, qs=qs):
            m, l, acc = st
            start = pl.multiple_of(j * LANES, LANES)
            kt = kn_ref[g, pl.ds(start, LANES), :]
            vt = vb_ref[g, pl.ds(start, LANES), :]
            s = lax.dot_general(qs, kt, NT_DIMS, preferred_element_type=F32)
            ma = madd_ref[j]
            s = jnp.concatenate(
                [s[r * tq:(r + 1) * tq] + bias[r] + ma for r in range(GQA_REP)], axis=0)
            m_new = jnp.maximum(m, jnp.max(s, axis=1, keepdims=True))
            alpha = jnp.exp(m - m_new)
            p = jnp.exp(s - m_new)
            l = alpha * l + jnp.sum(p, axis=1, keepdims=True)
            acc = alpha * acc + jnp.dot(p.astype(BF16), vt, preferred_element_type=F32)
            return m_new, l, acc

        st = (jnp.full((GQA_REP * tq, 1), NEG, F32),
              jnp.zeros((GQA_REP * tq, 1), F32),
              jnp.zeros((GQA_REP * tq, HEAD_DIM), F32))
        far_bias = [rb_ref[REL_BUCKETS - 1, h] for h in heads]
        st = lax.fori_loop(0, jnp.maximum(i - 1, 0),
                           lambda j, st: attend(j, st, far_bias), st)
        no_prev = jnp.where(i >= 1, 0.0, NEG)
        prev_bias = [bt_ref[h, :, 0:LANES] + no_prev for h in heads]
        st = attend(jnp.maximum(i - 1, 0), st, prev_bias)
        diag_bias = [bt_ref[h, :, LANES:2 * LANES] for h in heads]
        m, l, acc = attend(i, st, diag_bias)
        out = acc / l
        for r, h in enumerate(heads):
            o_ref[:, h * HEAD_DIM:(h + 1) * HEAD_DIM] = out[r * tq:(r + 1) * tq]


def _dsa(proj, q_gain, k_gain, rel_bias, bias_tiles, *, tq=128):
    b, s, _ = proj.shape
    n_t = s // LANES
    return pl.pallas_call(
        functools.partial(_dsa_kernel, tq=tq),
        out_shape=jax.ShapeDtypeStruct((b, s, ATT_HEADS * HEAD_DIM), F32),
        grid=(b, s // tq),
        in_specs=[
            pl.BlockSpec(memory_space=pltpu.SMEM),
            pl.BlockSpec((None, tq, 1024), lambda bi, i: (bi, i, 0)),
            pl.BlockSpec((None, s, 256), lambda bi, i: (bi, 0, 4)),
            pl.BlockSpec((None, s, 256), lambda bi, i: (bi, 0, 5)),
            pl.BlockSpec((None, tq, 512), lambda bi, i: (bi, i, 3)),
            pl.BlockSpec((None, tq, LANES), lambda bi, i: (bi, i, 16)),
            pl.BlockSpec((None, s, LANES), lambda bi, i: (bi, 0, 16)),
            pl.BlockSpec((1, HEAD_DIM), lambda bi, i: (0, 0)),
            pl.BlockSpec((1, HEAD_DIM), lambda bi, i: (0, 0)),
            pl.BlockSpec((ATT_HEADS, LANES, 2 * LANES), lambda bi, i: (0, 0, 0)),
        ],
        out_specs=pl.BlockSpec((None, tq, ATT_HEADS * HEAD_DIM), lambda bi, i: (bi, i, 0)),
        scratch_shapes=[
            pltpu.VMEM((KV_HEADS, s, HEAD_DIM), BF16),
            pltpu.VMEM((KV_HEADS, s, HEAD_DIM), BF16),
            pltpu.VMEM((s, HEAD_DIM), BF16),
            pltpu.VMEM((n_t, tq, LANES), I32),
            pltpu.VMEM((n_t, tq, LANES), F32),
        ],
        compiler_params=pltpu.CompilerParams(
            dimension_semantics=("parallel", "arbitrary"), vmem_limit_bytes=VMEM_LIMIT),
        name="dsa_attention",
    )(rel_bias, proj, proj, proj, proj, proj, proj, q_gain, k_gain, bias_tiles)


def kernel(x, norm_mix, norm_ffn, ev_w_in, ev_conv_w, ev_w_out, od_w_in, od_q_gain,
           od_k_gain, od_w_out, rel_bias, ffn_w_gate, ffn_w_up, ffn_w_down):
    b, s, d = x.shape
    n_tok = b * s
    bf = lambda w: w.astype(BF16)

    proj0 = _norm_matmul(x.reshape(n_tok, d), norm_mix[0:1], bf(ev_w_in[0]))
    proj0 = proj0.reshape(b, s, -1)
    a_out = _stickbreak(proj0)
    x1 = _mix0(a_out, proj0, ev_conv_w[0], bf(ev_w_out[0]), x)
    x2 = _swiglu(x1.reshape(n_tok, d), norm_ffn[0:1], bf(ffn_w_gate[0]), bf(ffn_w_up[0]),
                 bf(ffn_w_down[0]))

    w_in1 = jnp.pad(od_w_in[0], ((0, 0), (0, ODD_IN_PAD - od_w_in.shape[2])))
    proj1 = _norm_matmul(x2, norm_mix[1:2], bf(w_in1)).reshape(b, s, ODD_IN_PAD)
    att = _dsa(proj1, od_q_gain[0:1], od_k_gain[0:1], rel_bias, _bias_tiles(rel_bias))
    x3 = _proj_res(att.reshape(n_tok, -1), bf(od_w_out[0]), x2)
    x4 = _swiglu(x3, norm_ffn[1:2], bf(ffn_w_gate[1]), bf(ffn_w_up[1]), bf(ffn_w_down[1]))
    return x4.reshape(b, s, d)
```

```python
import functools
import math

import jax
import jax.numpy as jnp
from jax import lax
from jax.experimental import pallas as pl
from jax.experimental.pallas import tpu as pltpu

F32 = jnp.float32
BF16 = jnp.bfloat16
I32 = jnp.int32

D_MODEL = 1024
SEQ = 2048
HEAD_DIM = 64
SB_WIDTH = 512
CONV_CH = 512
ATT_HEADS = 16
KV_HEADS = 4
GQA_REP = 4
IDX_HEADS = 8
TOPK = 256
REL_BUCKETS = 32
REL_MAX_DIST = 128
FFN_HIDDEN = 2816
EPS = 1e-6
ODD_IN_PAD = 2176

LANES = 128
VMEM_LIMIT = 56 * 1024 * 1024
NEG = -1e30
INT_MIN = -(2 ** 31)
KEY_NEG_INF = -(2 ** 31) + 0x7FFFFF

NT_DIMS = (((1,), (1,)), ((), ()))


def _rms(x, g):
    ms = jnp.mean(x * x, axis=-1, keepdims=True)
    return x * lax.rsqrt(ms + EPS) * g


def _norm_matmul_kernel(x_ref, g_ref, w_ref, o_ref, *, tn):
    h = _rms(x_ref[...], g_ref[...]).astype(BF16)
    n = w_ref.shape[1]
    for start in range(0, n, tn):
        size = min(tn, n - start)
        o_ref[:, start:start + size] = jnp.dot(
            h, w_ref[:, start:start + size], preferred_element_type=F32)


def _norm_matmul(x, g, w, *, tm=512, tn=512):
    m, d = x.shape
    n = w.shape[1]
    return pl.pallas_call(
        functools.partial(_norm_matmul_kernel, tn=tn),
        out_shape=jax.ShapeDtypeStruct((m, n), F32),
        grid=(m // tm,),
        in_specs=[
            pl.BlockSpec((tm, d), lambda i: (i, 0)),
            pl.BlockSpec((1, d), lambda i: (0, 0)),
            pl.BlockSpec((d, n), lambda i: (0, 0)),
        ],
        out_specs=pl.BlockSpec((tm, n), lambda i: (i, 0)),
        compiler_params=pltpu.CompilerParams(
            dimension_semantics=("parallel",), vmem_limit_bytes=VMEM_LIMIT),
        name="norm_matmul",
    )(x, g, w)


def _sb_kernel(q_ref, k_ref, v_ref, o_ref, vt_ref, *, tq, pairs):
    qi = pl.program_id(2)
    n_sub = tq // LANES
    s_len = k_ref.shape[0]

    @pl.when(qi == 0)
    def _transpose_values():
        for p in range(pairs):
            for j in range(s_len // LANES):
                vt_ref[p, j] = v_ref[j * LANES:(j + 1) * LANES,
                                     p * LANES:(p + 1) * LANES].T.astype(BF16)

    lane = lax.broadcasted_iota(I32, (tq, LANES), 1)
    krow = lax.broadcasted_iota(I32, (tq, tq), 0)
    qcol = lax.broadcasted_iota(I32, (tq, tq), 1)
    from_here = (qcol >= krow).astype(BF16)
    causal = krow < qcol
    q_heads = []
    for p in range(pairs):
        q = q_ref[:, p * LANES:(p + 1) * LANES] * 0.125
        for h in range(2):
            in_head = (lane >= HEAD_DIM * h) & (lane < HEAD_DIM * (h + 1))
            q_heads.append(jnp.where(in_head, q, 0.0).astype(BF16))

    def block(kb, st, masked):
        start = pl.multiple_of(kb * tq, tq)
        heads = range(2 * pairs)
        kblks = [k_ref[pl.ds(start, tq), p * LANES:(p + 1) * LANES].astype(BF16)
                 for p in range(pairs)]
        v_ts = [jnp.concatenate([vt_ref[p, kb * n_sub + u] for u in range(n_sub)], axis=1)
                for p in range(pairs)]
        zs = [lax.dot_general(kblks[idx // 2], q_heads[idx], NT_DIMS, preferred_element_type=F32)
              for idx in heads]
        softplus, his, los = [], [], []
        for z in zs:
            sp = jnp.maximum(z, 0.0) + jnp.log(1.0 + jnp.exp(-jnp.abs(z)))
            if masked:
                sp = jnp.where(causal, sp, 0.0)
            hi = sp.astype(BF16)
            softplus.append(sp)
            his.append(hi)
            los.append((sp - hi.astype(F32)).astype(BF16))
        since = [jnp.dot(from_here, his[idx], preferred_element_type=F32)
                 + jnp.dot(from_here, los[idx], preferred_element_type=F32) for idx in heads]
        ws = []
        for idx in heads:
            w = jnp.exp(zs[idx] - since[idx] - st[2 * idx])
            if masked:
                w = jnp.where(causal, w, 0.0)
            ws.append(w.astype(BF16))
        new = []
        for idx in heads:
            h = idx % 2
            acc = st[2 * idx + 1] + jnp.dot(v_ts[idx // 2][h * HEAD_DIM:(h + 1) * HEAD_DIM, :],
                                            ws[idx], preferred_element_type=F32)
            carry = st[2 * idx] + jnp.sum(softplus[idx], axis=0, keepdims=True)
            new += [carry, acc]
        return tuple(new)

    st = (jnp.zeros((1, tq), F32), jnp.zeros((HEAD_DIM, tq), F32)) * (2 * pairs)
    st = block(qi, st, True)
    st = lax.fori_loop(0, qi, lambda i, st: block(qi - 1 - i, st, False), st)
    for p in range(pairs):
        out_t = jnp.concatenate([st[4 * p + 1], st[4 * p + 3]], axis=0)
        for u in range(n_sub):
            o_ref[u * LANES:(u + 1) * LANES, p * LANES:(p + 1) * LANES] = (
                out_t[:, u * LANES:(u + 1) * LANES].T)


def _stickbreak(proj, *, tq=256, pairs=2):
    b, s, _ = proj.shape
    width = pairs * LANES
    n_steps = SB_WIDTH // width
    return pl.pallas_call(
        functools.partial(_sb_kernel, tq=tq, pairs=pairs),
        out_shape=jax.ShapeDtypeStruct((b, s, SB_WIDTH), F32),
        grid=(b, n_steps, s // tq),
        in_specs=[
            pl.BlockSpec((None, tq, width), lambda bi, hp, qi: (bi, qi, hp)),
            pl.BlockSpec((None, s, width), lambda bi, hp, qi: (bi, 0, n_steps + hp)),
            pl.BlockSpec((None, s, width), lambda bi, hp, qi: (bi, 0, 2 * n_steps + hp)),
        ],
        out_specs=pl.BlockSpec((None, tq, width), lambda bi, hp, qi: (bi, qi, hp)),
        scratch_shapes=[pltpu.VMEM((pairs, s // LANES, LANES, LANES), BF16)],
        compiler_params=pltpu.CompilerParams(
            dimension_semantics=("parallel", "parallel", "arbitrary"),
            vmem_limit_bytes=VMEM_LIMIT),
        name="stickbreak",
    )(proj, proj, proj)


def _mix0_kernel(a_ref, bg_ref, cg_ref, u_ref, cgh_ref, uh_ref, cw_ref, w_ref, x_ref, o_ref,
                 *, tm):
    i = pl.program_id(1)
    g = cg_ref[...] * u_ref[...]
    gh = cgh_ref[...] * uh_ref[...]
    gh = jnp.where(i == 0, 0.0, gh)
    row = lax.broadcasted_iota(I32, g.shape, 0)
    g1 = jnp.where(row == 0, gh[7:8, :], pltpu.roll(g, 1, axis=0))
    g2 = pltpu.roll(g, 2, axis=0)
    g2 = jnp.where(row == 0, gh[6:7, :], jnp.where(row == 1, gh[7:8, :], g2))
    cw = cw_ref[...]
    y = bg_ref[...] * (cw[0:1, :] * g2 + cw[1:2, :] * g1 + cw[2:3, :] * g)
    acc = jnp.dot(a_ref[...].astype(BF16), w_ref[0:SB_WIDTH, :], preferred_element_type=F32)
    acc = acc + jnp.dot(y.astype(BF16), w_ref[SB_WIDTH:, :], preferred_element_type=F32)
    o_ref[...] = x_ref[...] + acc


def _mix0(a_out, proj, conv_w, w_out, x, *, tm=512):
    b, s, d = x.shape
    halo = lambda col: (lambda bi, i: (bi, jnp.maximum(i * (tm // 8) - 1, 0), col))
    return pl.pallas_call(
        functools.partial(_mix0_kernel, tm=tm),
        out_shape=jax.ShapeDtypeStruct((b, s, d), F32),
        grid=(b, s // tm),
        in_specs=[
            pl.BlockSpec((None, tm, SB_WIDTH), lambda bi, i: (bi, i, 0)),
            pl.BlockSpec((None, tm, CONV_CH), lambda bi, i: (bi, i, 3)),
            pl.BlockSpec((None, tm, CONV_CH), lambda bi, i: (bi, i, 4)),
            pl.BlockSpec((None, tm, CONV_CH), lambda bi, i: (bi, i, 5)),
            pl.BlockSpec((None, 8, CONV_CH), halo(4)),
            pl.BlockSpec((None, 8, CONV_CH), halo(5)),
            pl.BlockSpec((3, CONV_CH), lambda bi, i: (0, 0)),
            pl.BlockSpec((SB_WIDTH + CONV_CH, d), lambda bi, i: (0, 0)),
            pl.BlockSpec((None, tm, d), lambda bi, i: (bi, i, 0)),
        ],
        out_specs=pl.BlockSpec((None, tm, d), lambda bi, i: (bi, i, 0)),
        compiler_params=pltpu.CompilerParams(
            dimension_semantics=("parallel", "arbitrary"), vmem_limit_bytes=VMEM_LIMIT),
        name="mix0_out",
    )(a_out, proj, proj, proj, proj, proj, conv_w, w_out, x)


def _proj_res_kernel(a_ref, w_ref, x_ref, o_ref):
    o_ref[...] = x_ref[...] + jnp.dot(a_ref[...].astype(BF16), w_ref[...],
                                       preferred_element_type=F32)


def _proj_res(a, w, x, *, tm=512):
    m, k = a.shape
    n = w.shape[1]
    return pl.pallas_call(
        _proj_res_kernel,
        out_shape=jax.ShapeDtypeStruct((m, n), F32),
        grid=(m // tm,),
        in_specs=[
            pl.BlockSpec((tm, k), lambda i: (i, 0)),
            pl.BlockSpec((k, n), lambda i: (0, 0)),
            pl.BlockSpec((tm, n), lambda i: (i, 0)),
        ],
        out_specs=pl.BlockSpec((tm, n), lambda i: (i, 0)),
        compiler_params=pltpu.CompilerParams(
            dimension_semantics=("parallel",), vmem_limit_bytes=VMEM_LIMIT),
        name="proj_residual",
    )(a, w, x)


def _swiglu_kernel(x_ref, g_ref, wg_ref, wu_ref, wd_ref, o_ref, act_ref, *, th):
    x = x_ref[...]
    h = _rms(x, g_ref[...]).astype(BF16)
    for start in range(0, FFN_HIDDEN, th):
        hg = jnp.dot(h, wg_ref[:, start:start + th], preferred_element_type=F32)
        hu = jnp.dot(h, wu_ref[:, start:start + th], preferred_element_type=F32)
        sig = 1.0 / (1.0 + jnp.exp(-hg))
        act_ref[:, start:start + th] = (hg * sig * hu).astype(BF16)
    o_ref[...] = x + jnp.dot(act_ref[...], wd_ref[...], preferred_element_type=F32)


def _swiglu(x, g, wg, wu, wd, *, tm=512, th=256):
    m, d = x.shape
    resident = pl.Buffered(1)
    return pl.pallas_call(
        functools.partial(_swiglu_kernel, th=th),
        out_shape=jax.ShapeDtypeStruct((m, d), F32),
        grid=(m // tm,),
        in_specs=[
            pl.BlockSpec((tm, d), lambda i: (i, 0)),
            pl.BlockSpec((1, d), lambda i: (0, 0)),
            pl.BlockSpec((d, FFN_HIDDEN), lambda i: (0, 0), pipeline_mode=resident),
            pl.BlockSpec((d, FFN_HIDDEN), lambda i: (0, 0), pipeline_mode=resident),
            pl.BlockSpec((FFN_HIDDEN, d), lambda i: (0, 0), pipeline_mode=resident),
        ],
        out_specs=pl.BlockSpec((tm, d), lambda i: (i, 0)),
        scratch_shapes=[pltpu.VMEM((tm, FFN_HIDDEN), BF16)],
        compiler_params=pltpu.CompilerParams(
            dimension_semantics=("parallel",), vmem_limit_bytes=VMEM_LIMIT),
        name="norm_swiglu",
    )(x, g, wg, wu, wd)


def _bias_tile_kernel(rb_ref, o_ref):
    h = pl.program_id(0)
    c = lax.broadcasted_iota(I32, (2 * LANES, LANES), 0)
    t = lax.broadcasted_iota(I32, (2 * LANES, LANES), 1)
    dist = jnp.maximum(LANES + t - c, 0)
    exact = REL_BUCKETS // 2
    d_f = jnp.maximum(dist, 1).astype(F32)
    large = exact + (jnp.log(d_f / exact) / math.log(REL_MAX_DIST / exact)
                     * (REL_BUCKETS - exact)).astype(I32)
    large = jnp.minimum(large, REL_BUCKETS - 1)
    bucket = jnp.where(dist < exact, dist, large)
    out = jnp.zeros((2 * LANES, LANES), F32)
    for b in range(REL_BUCKETS):
        out = jnp.where(bucket == b, rb_ref[b, h], out)
    o_ref[...] = out - rb_ref[REL_BUCKETS - 1, h]


def _bias_tiles(rel_bias):
    return pl.pallas_call(
        _bias_tile_kernel,
        out_shape=jax.ShapeDtypeStruct((ATT_HEADS, 2 * LANES, LANES), F32),
        grid=(ATT_HEADS,),
        in_specs=[pl.BlockSpec(memory_space=pltpu.SMEM)],
        out_specs=pl.BlockSpec((None, 2 * LANES, LANES), lambda h: (h, 0, 0)),
        name="rel_bias_tiles",
    )(rel_bias)


def _dsa_kernel(q_ref, k_ref, v_ref, qi_ref, wq_ref, kik_ref, qg_ref, kg_ref, bt_ref,
                o_ref, kn_ref, vt_ref, kib_ref, key_ref, mfar_ref, *, tq, chunk):
    i = pl.program_id(1)
    s_len = k_ref.shape[0]
    n_chunk = (i + chunk) // chunk
    n_far = (i - 1 + chunk - 1) // chunk
    w0 = jnp.maximum(i - 1, 0)

    @pl.when(i == 0)
    def _prepare_keys():
        kg = kg_ref[...]
        for g in range(KV_HEADS):
            kn_ref[g] = _rms(k_ref[:, g * HEAD_DIM:(g + 1) * HEAD_DIM], kg).astype(BF16)
        for j in range(s_len // LANES):
            for pair in range(KV_HEADS // 2):
                v_t = v_ref[j * LANES:(j + 1) * LANES, pair * LANES:(pair + 1) * LANES].T
                vt_ref[2 * pair, j] = v_t[0:HEAD_DIM].astype(BF16)
                vt_ref[2 * pair + 1, j] = v_t[HEAD_DIM:].astype(BF16)
        kib_ref[...] = kik_ref[:, 0:HEAD_DIM].astype(BF16)

    ck = chunk * LANES
    krow = lax.broadcasted_iota(I32, (ck, tq), 0)
    tpos = i * tq + lax.broadcasted_iota(I32, (ck, tq), 1)

    w_t = wq_ref[...].T
    qi_heads = [(qi_ref[:, h * HEAD_DIM:(h + 1) * HEAD_DIM] * 0.125).astype(BF16)
                for h in range(IDX_HEADS)]
    w_heads = [w_t[HEAD_DIM + h:HEAD_DIM + h + 1, :] * (IDX_HEADS ** -0.5)
               for h in range(IDX_HEADS)]

    def score_chunk(c, _):
        kt = kib_ref[pl.ds(pl.multiple_of(c * ck, ck), ck), :]
        dots = [lax.dot_general(kt, qi_heads[h], NT_DIMS, preferred_element_type=F32)
                for h in range(IDX_HEADS)]
        sc = jnp.zeros((ck, tq), F32)
        for h in range(IDX_HEADS):
            sc = sc + w_heads[h] * jnp.maximum(dots[h], 0.0)
        sc = jnp.where(sc == 0.0, 0.0, sc)
        sc = jnp.where(c * ck + krow <= tpos, sc, -jnp.inf)
        bits = pltpu.bitcast(sc, I32)
        key_ref[c] = bits ^ ((bits >> 31) & 0x7FFFFFFF)
        return 0

    lax.fori_loop(0, n_chunk, score_chunk, 0)

    def count_keys(preds):
        def body(c, cs):
            key = key_ref[c]
            return tuple(
                cnt + jnp.sum(pred(c, key).astype(I32).reshape(ck // 8, 8, tq), axis=0)
                for cnt, pred in zip(cs, preds))
        cs = lax.fori_loop(0, n_chunk, body,
                           tuple(jnp.zeros((8, tq), I32) for _ in preds))
        return [jnp.sum(cnt, axis=0, keepdims=True) for cnt in cs]

    def thr_bit(b, thr):
        cand = thr + lax.shift_left(jnp.int32(1), 31 - b)
        cand_b = jnp.broadcast_to(cand, (ck, tq))
        (cnt,) = count_keys([lambda c, key: key >= cand_b])
        return jnp.where(cnt >= TOPK, cand, thr)

    thr = lax.fori_loop(0, 32, thr_bit, jnp.full((1, tq), INT_MIN, I32))
    thr_b = jnp.broadcast_to(thr, (ck, tq))
    n_gt, n_eq = count_keys([lambda c, key: key > thr_b, lambda c, key: key == thr_b])
    need = TOPK - n_gt
    tied = (n_eq > need) & (thr != KEY_NEG_INF)
    any_tied = jnp.max(tied.astype(I32)) > 0

    def tie_bit(b, cut):
        cand = cut + lax.shift_left(jnp.int32(1), 10 - b)
        cand_b = jnp.broadcast_to(cand, (ck, tq))
        (cnt,) = count_keys([lambda c, key: (key == thr_b) & (c * ck + krow < cand_b)])
        return jnp.where(cnt < need, cand, cut)

    cut = lax.cond(any_tied,
                   lambda: lax.fori_loop(0, 11, tie_bit, jnp.zeros((1, tq), I32)),
                   lambda: jnp.full((1, tq), s_len, I32))

    def selection_mask(key, spos, tpos):
        thr_k = jnp.broadcast_to(thr, key.shape)
        cut_k = jnp.broadcast_to(cut, key.shape)
        sel = (key > thr_k) | ((key == thr_k) & (spos <= cut_k))
        return jnp.where(sel & (spos <= tpos), 0.0, NEG)

    def mask_chunk(c, _):
        spos = c * ck + krow
        madd = selection_mask(key_ref[c], spos, tpos)
        mfar_ref[c] = jnp.where(spos < (i - 1) * LANES, madd, NEG)
        return 0

    lax.fori_loop(0, n_far, mask_chunk, 0)

    def tile_mask(tile):
        c, u = tile // chunk, tile % chunk
        key = key_ref[c, pl.ds(pl.multiple_of(u * LANES, LANES), LANES), :]
        krow_t = lax.broadcasted_iota(I32, (LANES, tq), 0)
        tpos_t = i * tq + lax.broadcasted_iota(I32, (LANES, tq), 1)
        return selection_mask(key, tile * LANES + krow_t, tpos_t)

    qg = qg_ref[...]
    qs = [jnp.concatenate(
        [(_rms(q_ref[:, h * HEAD_DIM:(h + 1) * HEAD_DIM], qg) * 0.125).astype(BF16)
         for h in range(g * GQA_REP, (g + 1) * GQA_REP)], axis=0)
        for g in range(KV_HEADS)]

    def attend(sts, kts, v_ts, adds):
        groups = range(KV_HEADS)
        ss = [lax.dot_general(kts[g], qs[g], NT_DIMS, preferred_element_type=F32)
              for g in groups]
        ps, stats = [], []
        for g in groups:
            m, l, _ = sts[g]
            s = jnp.concatenate(
                [ss[g][:, r * tq:(r + 1) * tq] + adds[g][r] for r in range(GQA_REP)], axis=1)
            m_new = jnp.maximum(m, jnp.max(s, axis=0, keepdims=True))
            alpha = jnp.exp(m - m_new)
            p = jnp.exp(s - m_new)
            stats.append((m_new, alpha * l + jnp.sum(p, axis=0, keepdims=True), alpha))
            ps.append(p.astype(BF16))
        return tuple(
            (stats[g][0], stats[g][1],
             stats[g][2] * sts[g][2] + jnp.dot(v_ts[g], ps[g], preferred_element_type=F32))
            for g in groups)

    def far_chunk(c, sts):
        madd = mfar_ref[c]
        kts = [kn_ref[g, pl.ds(pl.multiple_of(c * ck, ck), ck), :] for g in range(KV_HEADS)]
        v_ts = [jnp.concatenate([vt_ref[g, c * chunk + u] for u in range(chunk)], axis=1)
                for g in range(KV_HEADS)]
        return attend(sts, kts, v_ts, [[madd] * GQA_REP] * KV_HEADS)

    init = (jnp.full((1, GQA_REP * tq), NEG, F32),
            jnp.zeros((1, GQA_REP * tq), F32),
            jnp.zeros((HEAD_DIM, GQA_REP * tq), F32))
    sts = lax.fori_loop(0, n_far, far_chunk, (init,) * KV_HEADS)

    win_mask = jnp.concatenate([tile_mask(w0), tile_mask(w0 + 1)], axis=0)
    first = i == 0
    kts = [kn_ref[g, pl.ds(pl.multiple_of(w0 * LANES, LANES), 2 * LANES), :]
           for g in range(KV_HEADS)]
    v_ts = [jnp.concatenate([vt_ref[g, w0], vt_ref[g, w0 + 1]], axis=1) for g in range(KV_HEADS)]
    adds = []
    for g in range(KV_HEADS):
        add = []
        for r in range(GQA_REP):
            bt = bt_ref[g * GQA_REP + r]
            shifted = jnp.concatenate([bt[LANES:], bt[:LANES]], axis=0)
            add.append(jnp.where(first, shifted, bt) + win_mask)
        adds.append(add)
    sts = attend(sts, kts, v_ts, adds)
    for g in range(KV_HEADS):
        m, l, acc = sts[g]
        out_t = acc / l
        for pair in range(GQA_REP // 2):
            two = jnp.concatenate(
                [out_t[:, (2 * pair) * tq:(2 * pair + 1) * tq],
                 out_t[:, (2 * pair + 1) * tq:(2 * pair + 2) * tq]], axis=0)
            col = (g * GQA_REP + 2 * pair) * HEAD_DIM
            o_ref[:, col:col + LANES] = two.T


def _dsa(proj, q_gain, k_gain, bias_tiles, *, tq=128, chunk=4):
    b, s, _ = proj.shape
    n_t = s // LANES
    return pl.pallas_call(
        functools.partial(_dsa_kernel, tq=tq, chunk=chunk),
        out_shape=jax.ShapeDtypeStruct((b, s, ATT_HEADS * HEAD_DIM), F32),
        grid=(b, s // tq),
        in_specs=[
            pl.BlockSpec((None, tq, 1024), lambda bi, i: (bi, i, 0)),
            pl.BlockSpec((None, s, 256), lambda bi, i: (bi, 0, 4)),
            pl.BlockSpec((None, s, 256), lambda bi, i: (bi, 0, 5)),
            pl.BlockSpec((None, tq, 512), lambda bi, i: (bi, i, 3)),
            pl.BlockSpec((None, tq, LANES), lambda bi, i: (bi, i, 16)),
            pl.BlockSpec((None, s, LANES), lambda bi, i: (bi, 0, 16)),
            pl.BlockSpec((1, HEAD_DIM), lambda bi, i: (0, 0)),
            pl.BlockSpec((1, HEAD_DIM), lambda bi, i: (0, 0)),
            pl.BlockSpec((ATT_HEADS, 2 * LANES, LANES), lambda bi, i: (0, 0, 0)),
        ],
        out_specs=pl.BlockSpec((None, tq, ATT_HEADS * HEAD_DIM), lambda bi, i: (bi, i, 0)),
        scratch_shapes=[
            pltpu.VMEM((KV_HEADS, s, HEAD_DIM), BF16),
            pltpu.VMEM((KV_HEADS, n_t, HEAD_DIM, LANES), BF16),
            pltpu.VMEM((s, HEAD_DIM), BF16),
            pltpu.VMEM((n_t // chunk, chunk * LANES, tq), I32),
            pltpu.VMEM((n_t // chunk, chunk * LANES, tq), F32),
        ],
        compiler_params=pltpu.CompilerParams(
            dimension_semantics=("parallel", "arbitrary"), vmem_limit_bytes=VMEM_LIMIT),
        name="dsa_attention",
    )(proj, proj, proj, proj, proj, proj, q_gain, k_gain, bias_tiles)


def kernel(x, norm_mix, norm_ffn, ev_w_in, ev_conv_w, ev_w_out, od_w_in, od_q_gain,
           od_k_gain, od_w_out, rel_bias, ffn_w_gate, ffn_w_up, ffn_w_down):
    b, s, d = x.shape
    n_tok = b * s
    bf = lambda w: w.astype(BF16)

    proj0 = _norm_matmul(x.reshape(n_tok, d), norm_mix[0:1], bf(ev_w_in[0]))
    proj0 = proj0.reshape(b, s, -1)
    a_out = _stickbreak(proj0)
    x1 = _mix0(a_out, proj0, ev_conv_w[0], bf(ev_w_out[0]), x)
    x2 = _swiglu(x1.reshape(n_tok, d), norm_ffn[0:1], bf(ffn_w_gate[0]), bf(ffn_w_up[0]),
                 bf(ffn_w_down[0]))

    w_in1 = jnp.pad(od_w_in[0], ((0, 0), (0, ODD_IN_PAD - od_w_in.shape[2])))
    proj1 = _norm_matmul(x2, norm_mix[1:2], bf(w_in1)).reshape(b, s, ODD_IN_PAD)
    att = _dsa(proj1, od_q_gain[0:1], od_k_gain[0:1], _bias_tiles(rel_bias))
    x3 = _proj_res(att.reshape(n_tok, -1), bf(od_w_out[0]), x2)
    x4 = _swiglu(x3, norm_ffn[1:2], bf(ffn_w_gate[1]), bf(ffn_w_up[1]), bf(ffn_w_down[1]))
    return x4.reshape(b, s, d)
```

```python
import functools
import math

import jax
import jax.numpy as jnp
from jax import lax
from jax.experimental import pallas as pl
from jax.experimental.pallas import tpu as pltpu

F32 = jnp.float32
BF16 = jnp.bfloat16
I32 = jnp.int32

D_MODEL = 1024
SEQ = 2048
HEAD_DIM = 64
SB_WIDTH = 512
CONV_CH = 512
ATT_HEADS = 16
KV_HEADS = 4
GQA_REP = 4
IDX_HEADS = 8
TOPK = 256
REL_BUCKETS = 32
REL_MAX_DIST = 128
FFN_HIDDEN = 2816
EPS = 1e-6
ODD_IN_PAD = 2176

LOG2E = 1.4426950408889634
LANES = 128
VMEM_LIMIT = 56 * 1024 * 1024
NEG = -1e30
INT_MIN = -(2 ** 31)
KEY_NEG_INF = -(2 ** 31) + 0x7FFFFF

NT_DIMS = (((1,), (1,)), ((), ()))


def _rms(x, g):
    ms = jnp.mean(x * x, axis=-1, keepdims=True)
    return x * lax.rsqrt(ms + EPS) * g


def _norm_matmul_kernel(x_ref, g_ref, w_ref, o_ref, *, tn):
    h = _rms(x_ref[...], g_ref[...]).astype(BF16)
    n = w_ref.shape[1]
    for start in range(0, n, tn):
        size = min(tn, n - start)
        o_ref[:, start:start + size] = jnp.dot(
            h, w_ref[:, start:start + size], preferred_element_type=F32)


def _norm_matmul(x, g, w, *, tm=512, tn=512):
    m, d = x.shape
    n = w.shape[1]
    return pl.pallas_call(
        functools.partial(_norm_matmul_kernel, tn=tn),
        out_shape=jax.ShapeDtypeStruct((m, n), F32),
        grid=(m // tm,),
        in_specs=[
            pl.BlockSpec((tm, d), lambda i: (i, 0)),
            pl.BlockSpec((1, d), lambda i: (0, 0)),
            pl.BlockSpec((d, n), lambda i: (0, 0)),
        ],
        out_specs=pl.BlockSpec((tm, n), lambda i: (i, 0)),
        compiler_params=pltpu.CompilerParams(
            dimension_semantics=("parallel",), vmem_limit_bytes=VMEM_LIMIT),
        name="norm_matmul",
    )(x, g, w)


def _sb_kernel(q_ref, k_ref, v_ref, o_ref, vt_ref, *, tq, pairs):
    qi = pl.program_id(2)
    n_sub = tq // LANES
    s_len = k_ref.shape[0]

    @pl.when(qi == 0)
    def _transpose_values():
        for p in range(pairs):
            for j in range(s_len // LANES):
                vt_ref[p, j] = v_ref[j * LANES:(j + 1) * LANES,
                                     p * LANES:(p + 1) * LANES].T.astype(BF16)

    lane = lax.broadcasted_iota(I32, (tq, LANES), 1)
    krow = lax.broadcasted_iota(I32, (tq, tq), 0)
    qcol = lax.broadcasted_iota(I32, (tq, tq), 1)
    from_here = (qcol >= krow).astype(BF16)
    causal = krow < qcol
    q_heads = []
    for p in range(pairs):
        q = q_ref[:, p * LANES:(p + 1) * LANES] * (HEAD_DIM ** -0.5 * LOG2E)
        for h in range(2):
            in_head = (lane >= HEAD_DIM * h) & (lane < HEAD_DIM * (h + 1))
            q_heads.append(jnp.where(in_head, q, 0.0).astype(BF16))

    def block(kb, st, masked):
        start = pl.multiple_of(kb * tq, tq)
        heads = range(2 * pairs)
        kblks = [k_ref[pl.ds(start, tq), p * LANES:(p + 1) * LANES].astype(BF16)
                 for p in range(pairs)]
        v_ts = [jnp.concatenate([vt_ref[p, kb * n_sub + u] for u in range(n_sub)], axis=1)
                for p in range(pairs)]
        zs = [lax.dot_general(kblks[idx // 2], q_heads[idx], NT_DIMS, preferred_element_type=F32)
              for idx in heads]
        softplus, his, los = [], [], []
        for z in zs:
            sp = jnp.maximum(z, 0.0) + jnp.log2(1.0 + jnp.exp2(-jnp.abs(z)))
            if masked:
                sp = jnp.where(causal, sp, 0.0)
            hi = sp.astype(BF16)
            softplus.append(sp)
            his.append(hi)
            los.append((sp - hi.astype(F32)).astype(BF16))
        since = [jnp.dot(from_here, his[idx], preferred_element_type=F32)
                 + jnp.dot(from_here, los[idx], preferred_element_type=F32) for idx in heads]
        ws = []
        for idx in heads:
            w = jnp.exp2(zs[idx] - since[idx] - st[2 * idx])
            if masked:
                w = jnp.where(causal, w, 0.0)
            ws.append(w.astype(BF16))
        new = []
        for idx in heads:
            h = idx % 2
            acc = st[2 * idx + 1] + jnp.dot(v_ts[idx // 2][h * HEAD_DIM:(h + 1) * HEAD_DIM, :],
                                            ws[idx], preferred_element_type=F32)
            carry = st[2 * idx] + jnp.sum(softplus[idx], axis=0, keepdims=True)
            new += [carry, acc]
        return tuple(new)

    st = (jnp.zeros((1, tq), F32), jnp.zeros((HEAD_DIM, tq), F32)) * (2 * pairs)
    st = block(qi, st, True)
    st = lax.fori_loop(0, qi, lambda i, st: block(qi - 1 - i, st, False), st)
    for p in range(pairs):
        out_t = jnp.concatenate([st[4 * p + 1], st[4 * p + 3]], axis=0)
        for u in range(n_sub):
            o_ref[u * LANES:(u + 1) * LANES, p * LANES:(p + 1) * LANES] = (
                out_t[:, u * LANES:(u + 1) * LANES].T.astype(o_ref.dtype))


def _stickbreak(proj, *, tq=256, pairs=4):
    b, s, _ = proj.shape
    width = pairs * LANES
    n_steps = SB_WIDTH // width
    return pl.pallas_call(
        functools.partial(_sb_kernel, tq=tq, pairs=pairs),
        out_shape=jax.ShapeDtypeStruct((b, s, SB_WIDTH), BF16),
        grid=(b, n_steps, s // tq),
        in_specs=[
            pl.BlockSpec((None, tq, width), lambda bi, hp, qi: (bi, qi, hp)),
            pl.BlockSpec((None, s, width), lambda bi, hp, qi: (bi, 0, n_steps + hp)),
            pl.BlockSpec((None, s, width), lambda bi, hp, qi: (bi, 0, 2 * n_steps + hp)),
        ],
        out_specs=pl.BlockSpec((None, tq, width), lambda bi, hp, qi: (bi, qi, hp)),
        scratch_shapes=[pltpu.VMEM((pairs, s // LANES, LANES, LANES), BF16)],
        compiler_params=pltpu.CompilerParams(
            dimension_semantics=("parallel", "parallel", "arbitrary"),
            vmem_limit_bytes=VMEM_LIMIT),
        name="stickbreak",
    )(proj, proj, proj)


def _mix0_kernel(a_ref, bg_ref, cg_ref, u_ref, cgh_ref, uh_ref, cw_ref, w_ref, x_ref, o_ref,
                 *, tm):
    i = pl.program_id(1)
    g = cg_ref[...] * u_ref[...]
    gh = cgh_ref[...] * uh_ref[...]
    gh = jnp.where(i == 0, 0.0, gh)
    row = lax.broadcasted_iota(I32, g.shape, 0)
    g1 = jnp.where(row == 0, gh[7:8, :], pltpu.roll(g, 1, axis=0))
    g2 = pltpu.roll(g, 2, axis=0)
    g2 = jnp.where(row == 0, gh[6:7, :], jnp.where(row == 1, gh[7:8, :], g2))
    cw = cw_ref[...]
    y = bg_ref[...] * (cw[0:1, :] * g2 + cw[1:2, :] * g1 + cw[2:3, :] * g)
    acc = jnp.dot(a_ref[...].astype(BF16), w_ref[0:SB_WIDTH, :], preferred_element_type=F32)
    acc = acc + jnp.dot(y.astype(BF16), w_ref[SB_WIDTH:, :], preferred_element_type=F32)
    o_ref[...] = x_ref[...] + acc


def _mix0(a_out, proj, conv_w, w_out, x, *, tm=512):
    b, s, d = x.shape
    halo = lambda col: (lambda bi, i: (bi, jnp.maximum(i * (tm // 8) - 1, 0), col))
    return pl.pallas_call(
        functools.partial(_mix0_kernel, tm=tm),
        out_shape=jax.ShapeDtypeStruct((b, s, d), F32),
        grid=(b, s // tm),
        in_specs=[
            pl.BlockSpec((None, tm, SB_WIDTH), lambda bi, i: (bi, i, 0)),
            pl.BlockSpec((None, tm, CONV_CH), lambda bi, i: (bi, i, 3)),
            pl.BlockSpec((None, tm, CONV_CH), lambda bi, i: (bi, i, 4)),
            pl.BlockSpec((None, tm, CONV_CH), lambda bi, i: (bi, i, 5)),
            pl.BlockSpec((None, 8, CONV_CH), halo(4)),
            pl.BlockSpec((None, 8, CONV_CH), halo(5)),
            pl.BlockSpec((3, CONV_CH), lambda bi, i: (0, 0)),
            pl.BlockSpec((SB_WIDTH + CONV_CH, d), lambda bi, i: (0, 0)),
            pl.BlockSpec((None, tm, d), lambda bi, i: (bi, i, 0)),
        ],
        out_specs=pl.BlockSpec((None, tm, d), lambda bi, i: (bi, i, 0)),
        compiler_params=pltpu.CompilerParams(
            dimension_semantics=("parallel", "arbitrary"), vmem_limit_bytes=VMEM_LIMIT),
        name="mix0_out",
    )(a_out, proj, proj, proj, proj, proj, conv_w, w_out, x)


def _proj_res_kernel(a_ref, w_ref, x_ref, o_ref):
    o_ref[...] = x_ref[...] + jnp.dot(a_ref[...].astype(BF16), w_ref[...],
                                       preferred_element_type=F32)


def _proj_res(a, w, x, *, tm=512):
    m, k = a.shape
    n = w.shape[1]
    return pl.pallas_call(
        _proj_res_kernel,
        out_shape=jax.ShapeDtypeStruct((m, n), F32),
        grid=(m // tm,),
        in_specs=[
            pl.BlockSpec((tm, k), lambda i: (i, 0)),
            pl.BlockSpec((k, n), lambda i: (0, 0)),
            pl.BlockSpec((tm, n), lambda i: (i, 0)),
        ],
        out_specs=pl.BlockSpec((tm, n), lambda i: (i, 0)),
        compiler_params=pltpu.CompilerParams(
            dimension_semantics=("parallel",), vmem_limit_bytes=VMEM_LIMIT),
        name="proj_residual",
    )(a, w, x)


def _swiglu_kernel(x_ref, g_ref, wg_ref, wu_ref, wd_ref, o_ref, act_ref, *, th):
    x = x_ref[...]
    h = _rms(x, g_ref[...]).astype(BF16)
    for start in range(0, FFN_HIDDEN, th):
        hg = jnp.dot(h, wg_ref[:, start:start + th], preferred_element_type=F32)
        hu = jnp.dot(h, wu_ref[:, start:start + th], preferred_element_type=F32)
        sig = 1.0 / (1.0 + jnp.exp(-hg))
        act_ref[:, start:start + th] = (hg * sig * hu).astype(BF16)
    o_ref[...] = x + jnp.dot(act_ref[...], wd_ref[...], preferred_element_type=F32)


def _swiglu(x, g, wg, wu, wd, *, tm=512, th=256):
    m, d = x.shape
    resident = pl.Buffered(1)
    return pl.pallas_call(
        functools.partial(_swiglu_kernel, th=th),
        out_shape=jax.ShapeDtypeStruct((m, d), F32),
        grid=(m // tm,),
        in_specs=[
            pl.BlockSpec((tm, d), lambda i: (i, 0)),
            pl.BlockSpec((1, d), lambda i: (0, 0)),
            pl.BlockSpec((d, FFN_HIDDEN), lambda i: (0, 0), pipeline_mode=resident),
            pl.BlockSpec((d, FFN_HIDDEN), lambda i: (0, 0), pipeline_mode=resident),
            pl.BlockSpec((FFN_HIDDEN, d), lambda i: (0, 0), pipeline_mode=resident),
        ],
        out_specs=pl.BlockSpec((tm, d), lambda i: (i, 0)),
        scratch_shapes=[pltpu.VMEM((tm, FFN_HIDDEN), BF16)],
        compiler_params=pltpu.CompilerParams(
            dimension_semantics=("parallel",), vmem_limit_bytes=VMEM_LIMIT),
        name="norm_swiglu",
    )(x, g, wg, wu, wd)


BIAS_ROWS = 3 * LANES


def _bias_tile_kernel(rb_ref, o_ref):
    h = pl.program_id(0)
    c = lax.broadcasted_iota(I32, (BIAS_ROWS, LANES), 0)
    t = lax.broadcasted_iota(I32, (BIAS_ROWS, LANES), 1)
    dist = jnp.maximum(LANES + t - c, 0)
    exact = REL_BUCKETS // 2
    d_f = jnp.maximum(dist, 1).astype(F32)
    large = exact + (jnp.log(d_f / exact) / math.log(REL_MAX_DIST / exact)
                     * (REL_BUCKETS - exact)).astype(I32)
    large = jnp.minimum(large, REL_BUCKETS - 1)
    bucket = jnp.where(dist < exact, dist, large)
    out = jnp.zeros((BIAS_ROWS, LANES), F32)
    for b in range(REL_BUCKETS):
        out = jnp.where(bucket == b, rb_ref[b, h], out)
    o_ref[...] = (out - rb_ref[REL_BUCKETS - 1, h]) * LOG2E


def _bias_tiles(rel_bias):
    return pl.pallas_call(
        _bias_tile_kernel,
        out_shape=jax.ShapeDtypeStruct((ATT_HEADS, BIAS_ROWS, LANES), F32),
        grid=(ATT_HEADS,),
        in_specs=[pl.BlockSpec(memory_space=pltpu.SMEM)],
        out_specs=pl.BlockSpec((None, BIAS_ROWS, LANES), lambda h: (h, 0, 0)),
        name="rel_bias_tiles",
    )(rel_bias)


def _dsa_kernel(q_ref, k_ref, v_ref, qi_ref, wq_ref, kik_ref, qg_ref, kg_ref, bt_ref, seg_ref,
                segt_ref, o_ref, kn_ref, vt_ref, kib_ref, key_ref, mfar_ref, s_ref, *, tq, chunk):
    i = pl.program_id(1)
    s_len = k_ref.shape[0]
    n_chunk = (i + chunk) // chunk
    n_far = (i - 1 + chunk - 1) // chunk
    w0 = jnp.maximum(i - 1, 0)

    @pl.when(i == 0)
    def _prepare_keys():
        kg = kg_ref[...]
        for g in range(KV_HEADS):
            kn = _rms(k_ref[:, g * HEAD_DIM:(g + 1) * HEAD_DIM], kg).astype(BF16)
            kn_ref[g] = jnp.concatenate([kn, kn], axis=1)
        for j in range(s_len // LANES):
            for pair in range(KV_HEADS // 2):
                v_t = v_ref[j * LANES:(j + 1) * LANES, pair * LANES:(pair + 1) * LANES].T
                vt_ref[2 * pair, j] = v_t[0:HEAD_DIM].astype(BF16)
                vt_ref[2 * pair + 1, j] = v_t[HEAD_DIM:].astype(BF16)
        ki = kik_ref[:, 0:HEAD_DIM].astype(BF16)
        kib_ref[...] = jnp.concatenate([ki, ki], axis=1)

    ck = chunk * LANES
    krow = lax.broadcasted_iota(I32, (ck, tq), 0)
    tpos = i * tq + lax.broadcasted_iota(I32, (ck, tq), 1)
    low_half = lax.broadcasted_iota(I32, (tq, LANES), 1) < HEAD_DIM

    def split_pair(x):
        return [jnp.where(low_half, x, 0.0).astype(BF16), jnp.where(low_half, 0.0, x).astype(BF16)]

    w_t = wq_ref[...].T
    qi_heads = []
    for pair in range(IDX_HEADS // 2):
        qi_heads += split_pair(qi_ref[:, pair * LANES:(pair + 1) * LANES] * 0.125)
    w_heads = [w_t[HEAD_DIM + h:HEAD_DIM + h + 1, :] * (IDX_HEADS ** -0.5)
               for h in range(IDX_HEADS)]

    def score_chunk(c, _):
        kt = kib_ref[pl.ds(pl.multiple_of(c * ck, ck), ck), :]
        dots = [lax.dot_general(kt, qi_heads[h], NT_DIMS, preferred_element_type=F32)
                for h in range(IDX_HEADS)]
        sc = jnp.zeros((ck, tq), F32)
        for h in range(IDX_HEADS):
            sc = sc + w_heads[h] * jnp.maximum(dots[h], 0.0)
        sc = jnp.where(sc == 0.0, 0.0, sc)
        sc = jnp.where(c * ck + krow <= tpos, sc, -jnp.inf)
        bits = pltpu.bitcast(sc, I32)
        key_ref[c] = bits ^ ((bits >> 31) & 0x7FFFFFFF)
        return 0

    lax.fori_loop(0, n_chunk, score_chunk, 0)

    def count_keys(preds):
        def body(c, cs):
            key = key_ref[c]
            return tuple(
                cnt + jnp.sum(pred(c, key).astype(I32).reshape(ck // 8, 8, tq), axis=0)
                for cnt, pred in zip(cs, preds))
        cs = lax.fori_loop(0, n_chunk, body,
                           tuple(jnp.zeros((8, tq), I32) for _ in preds))
        return [jnp.sum(cnt, axis=0, keepdims=True) for cnt in cs]

    def thr_bit(b, thr):
        cand = thr + lax.shift_left(jnp.int32(1), 31 - b)
        cand_b = jnp.broadcast_to(cand, (ck, tq))
        (cnt,) = count_keys([lambda c, key: key >= cand_b])
        return jnp.where(cnt >= TOPK, cand, thr)

    thr = lax.fori_loop(0, 32, thr_bit, jnp.full((1, tq), INT_MIN, I32))
    thr_b = jnp.broadcast_to(thr, (ck, tq))
    n_gt, n_eq = count_keys([lambda c, key: key > thr_b, lambda c, key: key == thr_b])
    need = TOPK - n_gt
    tied = (n_eq > need) & (thr != KEY_NEG_INF)
    any_tied = jnp.max(tied.astype(I32)) > 0

    def tie_bit(b, cut):
        cand = cut + lax.shift_left(jnp.int32(1), 10 - b)
        cand_b = jnp.broadcast_to(cand, (ck, tq))
        (cnt,) = count_keys([lambda c, key: (key == thr_b) & (c * ck + krow < cand_b)])
        return jnp.where(cnt < need, cand, cut)

    cut = lax.cond(any_tied,
                   lambda: lax.fori_loop(0, 11, tie_bit, jnp.zeros((1, tq), I32)),
                   lambda: jnp.full((1, tq), s_len, I32))

    def selection_mask(key, spos, tpos):
        thr_k = jnp.broadcast_to(thr, key.shape)
        cut_k = jnp.broadcast_to(cut, key.shape)
        sel = (key > thr_k) | ((key == thr_k) & (spos <= cut_k))
        return jnp.where(sel & (spos <= tpos), 0.0, NEG)

    def mask_chunk(c, _):
        spos = c * ck + krow
        madd = selection_mask(key_ref[c], spos, tpos)
        mfar_ref[c] = jnp.where(spos < (i - 1) * LANES, madd, NEG)
        return 0

    lax.fori_loop(0, n_far, mask_chunk, 0)

    def tile_mask(tile):
        c, u = tile // chunk, tile % chunk
        key = key_ref[c, pl.ds(pl.multiple_of(u * LANES, LANES), LANES), :]
        krow_t = lax.broadcasted_iota(I32, (LANES, tq), 0)
        tpos_t = i * tq + lax.broadcasted_iota(I32, (LANES, tq), 1)
        return selection_mask(key, tile * LANES + krow_t, tpos_t)

    q = q_ref[...]
    sq = q * q
    sq_hi = sq.astype(BF16)
    sq_lo = (sq - sq_hi.astype(F32)).astype(BF16)
    seg = seg_ref[...]
    ssum = (jnp.dot(sq_hi, seg, preferred_element_type=F32)
            + jnp.dot(sq_lo, seg, preferred_element_type=F32))
    rinv = lax.rsqrt(ssum * (1.0 / HEAD_DIM) + EPS)
    pieces, rest = [], rinv
    for _ in range(3):
        piece = rest.astype(BF16)
        pieces.append(piece)
        rest = rest - piece.astype(F32)
    segt = segt_ref[...]
    rinv_b = sum(jnp.dot(piece, segt, preferred_element_type=F32) for piece in pieces)
    qn = q * rinv_b * (qg_ref[...] * (HEAD_DIM ** -0.5 * LOG2E))
    qs = []
    for g in range(KV_HEADS):
        rows = []
        for pair in range(2 * g, 2 * g + 2):
            rows += split_pair(qn[:, pair * LANES:(pair + 1) * LANES])
        qs.append(jnp.concatenate(rows, axis=0))

    def attend(sts, n_keys, kts, v_ts, add_fn):
        groups = range(KV_HEADS)
        for g in groups:
            s_ref[g, 0:n_keys, :] = lax.dot_general(kts[g], qs[g], NT_DIMS,
                                                    preferred_element_type=F32)
        ps, stats = [], []
        for g in groups:
            m, l, _ = sts[g]
            add = jnp.concatenate([add_fn(g * GQA_REP + r) for r in range(GQA_REP)], axis=1)
            m_new = jnp.maximum(m, jnp.max(s_ref[g, 0:n_keys, :] + add, axis=0, keepdims=True))
            alpha = jnp.exp2(m - m_new)
            p = jnp.exp2((s_ref[g, 0:n_keys, :] - m_new) + add)
            stats.append((m_new, alpha * l + jnp.sum(p, axis=0, keepdims=True), alpha))
            ps.append(p.astype(BF16))
        return tuple(
            (stats[g][0], stats[g][1],
             stats[g][2] * sts[g][2] + jnp.dot(v_ts[g], ps[g], preferred_element_type=F32))
            for g in groups)

    def far_chunk(c, sts):
        madd = mfar_ref[c]
        kts = [kn_ref[g, pl.ds(pl.multiple_of(c * ck, ck), ck), :] for g in range(KV_HEADS)]
        v_ts = [jnp.concatenate([vt_ref[g, c * chunk + u] for u in range(chunk)], axis=1)
                for g in range(KV_HEADS)]
        return attend(sts, ck, kts, v_ts, lambda head: madd)

    init = (jnp.full((1, GQA_REP * tq), NEG, F32),
            jnp.zeros((1, GQA_REP * tq), F32),
            jnp.zeros((HEAD_DIM, GQA_REP * tq), F32))
    sts = lax.fori_loop(0, n_far, far_chunk, (init,) * KV_HEADS)

    win_mask = jnp.concatenate([tile_mask(w0), tile_mask(w0 + 1)], axis=0)
    bias_row = pl.multiple_of(jnp.where(i == 0, LANES, 0), LANES)
    kts = [kn_ref[g, pl.ds(pl.multiple_of(w0 * LANES, LANES), 2 * LANES), :]
           for g in range(KV_HEADS)]
    v_ts = [jnp.concatenate([vt_ref[g, w0], vt_ref[g, w0 + 1]], axis=1) for g in range(KV_HEADS)]
    sts = attend(sts, 2 * LANES, kts, v_ts,
                 lambda head: bt_ref[head, pl.ds(bias_row, 2 * LANES), :] + win_mask)
    for g in range(KV_HEADS):
        m, l, acc = sts[g]
        out_t = acc / l
        for pair in range(GQA_REP // 2):
            two = jnp.concatenate(
                [out_t[:, (2 * pair) * tq:(2 * pair + 1) * tq],
                 out_t[:, (2 * pair + 1) * tq:(2 * pair + 2) * tq]], axis=0)
            col = (g * GQA_REP + 2 * pair) * HEAD_DIM
            o_ref[:, col:col + LANES] = two.T.astype(o_ref.dtype)


def _dsa(proj, q_gain, k_gain, bias_tiles, *, tq=128, chunk=4):
    b, s, _ = proj.shape
    n_t = s // LANES
    att_width = ATT_HEADS * HEAD_DIM
    seg = (jnp.arange(att_width)[:, None] // HEAD_DIM == jnp.arange(LANES)[None, :]).astype(BF16)
    q_gain_all = jnp.tile(q_gain, (1, ATT_HEADS))
    return pl.pallas_call(
        functools.partial(_dsa_kernel, tq=tq, chunk=chunk),
        out_shape=jax.ShapeDtypeStruct((b, s, att_width), BF16),
        grid=(b, s // tq),
        in_specs=[
            pl.BlockSpec((None, tq, 1024), lambda bi, i: (bi, i, 0)),
            pl.BlockSpec((None, s, 256), lambda bi, i: (bi, 0, 4)),
            pl.BlockSpec((None, s, 256), lambda bi, i: (bi, 0, 5)),
            pl.BlockSpec((None, tq, 512), lambda bi, i: (bi, i, 3)),
            pl.BlockSpec((None, tq, LANES), lambda bi, i: (bi, i, 16)),
            pl.BlockSpec((None, s, LANES), lambda bi, i: (bi, 0, 16)),
            pl.BlockSpec((1, att_width), lambda bi, i: (0, 0)),
            pl.BlockSpec((1, HEAD_DIM), lambda bi, i: (0, 0)),
            pl.BlockSpec((ATT_HEADS, BIAS_ROWS, LANES), lambda bi, i: (0, 0, 0)),
            pl.BlockSpec((att_width, LANES), lambda bi, i: (0, 0)),
            pl.BlockSpec((LANES, att_width), lambda bi, i: (0, 0)),
        ],
        out_specs=pl.BlockSpec((None, tq, att_width), lambda bi, i: (bi, i, 0)),
        scratch_shapes=[
            pltpu.VMEM((KV_HEADS, s, LANES), BF16),
            pltpu.VMEM((KV_HEADS, n_t, HEAD_DIM, LANES), BF16),
            pltpu.VMEM((s, LANES), BF16),
            pltpu.VMEM((n_t // chunk, chunk * LANES, tq), I32),
            pltpu.VMEM((n_t // chunk, chunk * LANES, tq), F32),
            pltpu.VMEM((KV_HEADS, chunk * LANES, GQA_REP * tq), F32),
        ],
        compiler_params=pltpu.CompilerParams(
            dimension_semantics=("parallel", "arbitrary"), vmem_limit_bytes=VMEM_LIMIT),
        name="dsa_attention",
    )(proj, proj, proj, proj, proj, proj, q_gain_all, k_gain, bias_tiles, seg, seg.T)


def kernel(x, norm_mix, norm_ffn, ev_w_in, ev_conv_w, ev_w_out, od_w_in, od_q_gain,
           od_k_gain, od_w_out, rel_bias, ffn_w_gate, ffn_w_up, ffn_w_down):
    b, s, d = x.shape
    n_tok = b * s
    bf = lambda w: w.astype(BF16)

    proj0 = _norm_matmul(x.reshape(n_tok, d), norm_mix[0:1], bf(ev_w_in[0]))
    proj0 = proj0.reshape(b, s, -1)
    a_out = _stickbreak(proj0)
    x1 = _mix0(a_out, proj0, ev_conv_w[0], bf(ev_w_out[0]), x)
    x2 = _swiglu(x1.reshape(n_tok, d), norm_ffn[0:1], bf(ffn_w_gate[0]), bf(ffn_w_up[0]),
                 bf(ffn_w_down[0]))

    w_in1 = jnp.pad(od_w_in[0], ((0, 0), (0, ODD_IN_PAD - od_w_in.shape[2])))
    proj1 = _norm_matmul(x2, norm_mix[1:2], bf(w_in1)).reshape(b, s, ODD_IN_PAD)
    att = _dsa(proj1, od_q_gain[0:1], od_k_gain[0:1], _bias_tiles(rel_bias))
    x3 = _proj_res(att.reshape(n_tok, -1), bf(od_w_out[0]), x2)
    x4 = _swiglu(x3, norm_ffn[1:2], bf(ffn_w_gate[1]), bf(ffn_w_up[1]), bf(ffn_w_down[1]))
    return x4.reshape(b, s, d)
```

```python
import functools
import math

import jax
import jax.numpy as jnp
from jax import lax
from jax.experimental import pallas as pl
from jax.experimental.pallas import tpu as pltpu

F32 = jnp.float32
BF16 = jnp.bfloat16
I32 = jnp.int32
I16 = jnp.int16

D_MODEL = 1024
SEQ = 2048
HEAD_DIM = 64
SB_WIDTH = 512
CONV_CH = 512
ATT_HEADS = 16
KV_HEADS = 4
GQA_REP = 4
IDX_HEADS = 8
TOPK = 256
REL_BUCKETS = 32
REL_MAX_DIST = 128
FFN_HIDDEN = 2816
EPS = 1e-6
ODD_IN_PAD = 2176

LOG2E = 1.4426950408889634
LANES = 128
VMEM_LIMIT = 56 * 1024 * 1024
NEG = -1e30
INT_MIN = -(2 ** 31)
UNDERFLOW_MARGIN = 160.0
NORM_SLACK = 1.03
I16_MIN = -(2 ** 15)
PACKED_ROWS = 16
KEY_NEG_INF = -(2 ** 31) + 0x7FFFFF

NT_DIMS = (((1,), (1,)), ((), ()))


def _rms(x, g):
    ms = jnp.mean(x * x, axis=-1, keepdims=True)
    return x * lax.rsqrt(ms + EPS) * g


def _norm_matmul_kernel(x_ref, g_ref, w_ref, o_ref, *, tn):
    h = _rms(x_ref[...], g_ref[...]).astype(BF16)
    n = w_ref.shape[1]
    for start in range(0, n, tn):
        size = min(tn, n - start)
        o_ref[:, start:start + size] = jnp.dot(
            h, w_ref[:, start:start + size], preferred_element_type=F32)


def _norm_matmul(x, g, w, *, tm=512, tn=512):
    m, d = x.shape
    n = w.shape[1]
    return pl.pallas_call(
        functools.partial(_norm_matmul_kernel, tn=tn),
        out_shape=jax.ShapeDtypeStruct((m, n), F32),
        grid=(m // tm,),
        in_specs=[
            pl.BlockSpec((tm, d), lambda i: (i, 0)),
            pl.BlockSpec((1, d), lambda i: (0, 0)),
            pl.BlockSpec((d, n), lambda i: (0, 0)),
        ],
        out_specs=pl.BlockSpec((tm, n), lambda i: (i, 0)),
        compiler_params=pltpu.CompilerParams(
            dimension_semantics=("parallel",), vmem_limit_bytes=VMEM_LIMIT),
        name="norm_matmul",
    )(x, g, w)


def _sb_kernel(q_ref, k_ref, v_ref, o_ref, vt_ref, kmax_ref, *, tq, pairs):
    qi = pl.program_id(2)
    n_sub = tq // LANES
    s_len = k_ref.shape[0]
    width = pairs * LANES
    head_of = (lax.broadcasted_iota(I32, (width, LANES), 0) // HEAD_DIM
               == lax.broadcasted_iota(I32, (width, LANES), 1)).astype(BF16)

    def max_sq_norm(x):
        sq = (x * x).astype(BF16)
        return jnp.max(jnp.dot(sq, head_of, preferred_element_type=F32), axis=0, keepdims=True)

    @pl.when(qi == 0)
    def _per_sequence_setup():
        for p in range(pairs):
            for j in range(s_len // LANES):
                vt_ref[p, j] = v_ref[j * LANES:(j + 1) * LANES,
                                     p * LANES:(p + 1) * LANES].T.astype(BF16)
        kmax = jnp.zeros((1, LANES), F32)
        for r in range(0, s_len, tq):
            kmax = jnp.maximum(kmax, max_sq_norm(k_ref[r:r + tq, :]))
        kmax_ref[...] = kmax

    lane = lax.broadcasted_iota(I32, (tq, LANES), 1)
    krow = lax.broadcasted_iota(I32, (tq, tq), 0)
    qcol = lax.broadcasted_iota(I32, (tq, tq), 1)
    from_here = (qcol >= krow).astype(BF16)
    causal = krow < qcol
    q_heads = []
    for p in range(pairs):
        q = q_ref[:, p * LANES:(p + 1) * LANES] * (HEAD_DIM ** -0.5 * LOG2E)
        for h in range(2):
            in_head = (lane >= HEAD_DIM * h) & (lane < HEAD_DIM * (h + 1))
            q_heads.append(jnp.where(in_head, q, 0.0).astype(BF16))

    def block(kb, st, masked):
        start = pl.multiple_of(kb * tq, tq)
        heads = range(2 * pairs)
        kblks = [k_ref[pl.ds(start, tq), p * LANES:(p + 1) * LANES].astype(BF16)
                 for p in range(pairs)]
        v_ts = [jnp.concatenate([vt_ref[p, kb * n_sub + u] for u in range(n_sub)], axis=1)
                for p in range(pairs)]
        zs = [lax.dot_general(kblks[idx // 2], q_heads[idx], NT_DIMS, preferred_element_type=F32)
              for idx in heads]
        softplus, his, los = [], [], []
        for z in zs:
            sp = jnp.maximum(z, 0.0) + jnp.log2(1.0 + jnp.exp2(-jnp.abs(z)))
            if masked:
                sp = jnp.where(causal, sp, 0.0)
            hi = sp.astype(BF16)
            softplus.append(sp)
            his.append(hi)
            los.append((sp - hi.astype(F32)).astype(BF16))
        since = [jnp.dot(from_here, his[idx], preferred_element_type=F32)
                 + jnp.dot(from_here, los[idx], preferred_element_type=F32) for idx in heads]
        ws = []
        for idx in heads:
            w = jnp.exp2(zs[idx] - since[idx] - st[2 * idx])
            if masked:
                w = jnp.where(causal, w, 0.0)
            ws.append(w.astype(BF16))
        new = []
        for idx in heads:
            h = idx % 2
            acc = st[2 * idx + 1] + jnp.dot(v_ts[idx // 2][h * HEAD_DIM:(h + 1) * HEAD_DIM, :],
                                            ws[idx], preferred_element_type=F32)
            carry = st[2 * idx] + jnp.sum(softplus[idx], axis=0, keepdims=True)
            new += [carry, acc]
        return tuple(new)

    z_bound = jnp.sqrt(max_sq_norm(q_ref[...]) * kmax_ref[...]) * (HEAD_DIM ** -0.5 * LOG2E * NORM_SLACK)
    z_bounds = [jnp.broadcast_to(z_bound[:, idx:idx + 1], (1, tq)) for idx in range(2 * pairs)]

    def all_underflow(st):
        slack = st[0] - z_bounds[0]
        for idx in range(1, 2 * pairs):
            slack = jnp.minimum(slack, st[2 * idx] - z_bounds[idx])
        return jnp.min(slack) > UNDERFLOW_MARGIN

    def walk(state):
        n, _, st = state
        st = block(qi - 1 - n, st, False)
        return n + 1, all_underflow(st), st

    st = (jnp.zeros((1, tq), F32), jnp.zeros((HEAD_DIM, tq), F32)) * (2 * pairs)
    st = block(qi, st, True)
    _, _, st = lax.while_loop(lambda state: (state[0] < qi) & jnp.logical_not(state[1]), walk,
                              (jnp.int32(0), all_underflow(st), st))
    for p in range(pairs):
        out_t = jnp.concatenate([st[4 * p + 1], st[4 * p + 3]], axis=0)
        for u in range(n_sub):
            o_ref[u * LANES:(u + 1) * LANES, p * LANES:(p + 1) * LANES] = (
                out_t[:, u * LANES:(u + 1) * LANES].T.astype(o_ref.dtype))


def _stickbreak(proj, *, tq=256, pairs=4):
    b, s, _ = proj.shape
    width = pairs * LANES
    n_steps = SB_WIDTH // width
    return pl.pallas_call(
        functools.partial(_sb_kernel, tq=tq, pairs=pairs),
        out_shape=jax.ShapeDtypeStruct((b, s, SB_WIDTH), BF16),
        grid=(b, n_steps, s // tq),
        in_specs=[
            pl.BlockSpec((None, tq, width), lambda bi, hp, qi: (bi, qi, hp)),
            pl.BlockSpec((None, s, width), lambda bi, hp, qi: (bi, 0, n_steps + hp)),
            pl.BlockSpec((None, s, width), lambda bi, hp, qi: (bi, 0, 2 * n_steps + hp)),
        ],
        out_specs=pl.BlockSpec((None, tq, width), lambda bi, hp, qi: (bi, qi, hp)),
        scratch_shapes=[pltpu.VMEM((pairs, s // LANES, LANES, LANES), BF16),
                        pltpu.VMEM((1, LANES), F32)],
        compiler_params=pltpu.CompilerParams(
            dimension_semantics=("parallel", "parallel", "arbitrary"),
            vmem_limit_bytes=VMEM_LIMIT),
        name="stickbreak",
    )(proj, proj, proj)


def _mix0_kernel(a_ref, bg_ref, cg_ref, u_ref, cgh_ref, uh_ref, cw_ref, w_ref, x_ref, o_ref,
                 *, tm):
    i = pl.program_id(1)
    g = cg_ref[...] * u_ref[...]
    gh = cgh_ref[...] * uh_ref[...]
    gh = jnp.where(i == 0, 0.0, gh)
    row = lax.broadcasted_iota(I32, g.shape, 0)
    g1 = jnp.where(row == 0, gh[7:8, :], pltpu.roll(g, 1, axis=0))
    g2 = pltpu.roll(g, 2, axis=0)
    g2 = jnp.where(row == 0, gh[6:7, :], jnp.where(row == 1, gh[7:8, :], g2))
    cw = cw_ref[...]
    y = bg_ref[...] * (cw[0:1, :] * g2 + cw[1:2, :] * g1 + cw[2:3, :] * g)
    acc = jnp.dot(a_ref[...].astype(BF16), w_ref[0:SB_WIDTH, :], preferred_element_type=F32)
    acc = acc + jnp.dot(y.astype(BF16), w_ref[SB_WIDTH:, :], preferred_element_type=F32)
    o_ref[...] = x_ref[...] + acc


def _mix0(a_out, proj, conv_w, w_out, x, *, tm=512):
    b, s, d = x.shape
    halo = lambda col: (lambda bi, i: (bi, jnp.maximum(i * (tm // 8) - 1, 0), col))
    return pl.pallas_call(
        functools.partial(_mix0_kernel, tm=tm),
        out_shape=jax.ShapeDtypeStruct((b, s, d), F32),
        grid=(b, s // tm),
        in_specs=[
            pl.BlockSpec((None, tm, SB_WIDTH), lambda bi, i: (bi, i, 0)),
            pl.BlockSpec((None, tm, CONV_CH), lambda bi, i: (bi, i, 3)),
            pl.BlockSpec((None, tm, CONV_CH), lambda bi, i: (bi, i, 4)),
            pl.BlockSpec((None, tm, CONV_CH), lambda bi, i: (bi, i, 5)),
            pl.BlockSpec((None, 8, CONV_CH), halo(4)),
            pl.BlockSpec((None, 8, CONV_CH), halo(5)),
            pl.BlockSpec((3, CONV_CH), lambda bi, i: (0, 0)),
            pl.BlockSpec((SB_WIDTH + CONV_CH, d), lambda bi, i: (0, 0)),
            pl.BlockSpec((None, tm, d), lambda bi, i: (bi, i, 0)),
        ],
        out_specs=pl.BlockSpec((None, tm, d), lambda bi, i: (bi, i, 0)),
        compiler_params=pltpu.CompilerParams(
            dimension_semantics=("parallel", "arbitrary"), vmem_limit_bytes=VMEM_LIMIT),
        name="mix0_out",
    )(a_out, proj, proj, proj, proj, proj, conv_w, w_out, x)


def _proj_res_kernel(a_ref, w_ref, x_ref, o_ref):
    o_ref[...] = x_ref[...] + jnp.dot(a_ref[...].astype(BF16), w_ref[...],
                                       preferred_element_type=F32)


def _proj_res(a, w, x, *, tm=512):
    m, k = a.shape
    n = w.shape[1]
    return pl.pallas_call(
        _proj_res_kernel,
        out_shape=jax.ShapeDtypeStruct((m, n), F32),
        grid=(m // tm,),
        in_specs=[
            pl.BlockSpec((tm, k), lambda i: (i, 0)),
            pl.BlockSpec((k, n), lambda i: (0, 0)),
            pl.BlockSpec((tm, n), lambda i: (i, 0)),
        ],
        out_specs=pl.BlockSpec((tm, n), lambda i: (i, 0)),
        compiler_params=pltpu.CompilerParams(
            dimension_semantics=("parallel",), vmem_limit_bytes=VMEM_LIMIT),
        name="proj_residual",
    )(a, w, x)


def _swiglu_kernel(x_ref, g_ref, wg_ref, wu_ref, wd_ref, o_ref, act_ref, *, th):
    x = x_ref[...]
    h = _rms(x, g_ref[...]).astype(BF16)
    for start in range(0, FFN_HIDDEN, th):
        hg = jnp.dot(h, wg_ref[:, start:start + th], preferred_element_type=F32)
        hu = jnp.dot(h, wu_ref[:, start:start + th], preferred_element_type=F32)
        sig = 1.0 / (1.0 + jnp.exp(-hg))
        act_ref[:, start:start + th] = (hg * sig * hu).astype(BF16)
    o_ref[...] = x + jnp.dot(act_ref[...], wd_ref[...], preferred_element_type=F32)


def _swiglu(x, g, wg, wu, wd, *, tm=512, th=256):
    m, d = x.shape
    resident = pl.Buffered(1)
    return pl.pallas_call(
        functools.partial(_swiglu_kernel, th=th),
        out_shape=jax.ShapeDtypeStruct((m, d), F32),
        grid=(m // tm,),
        in_specs=[
            pl.BlockSpec((tm, d), lambda i: (i, 0)),
            pl.BlockSpec((1, d), lambda i: (0, 0)),
            pl.BlockSpec((d, FFN_HIDDEN), lambda i: (0, 0), pipeline_mode=resident),
            pl.BlockSpec((d, FFN_HIDDEN), lambda i: (0, 0), pipeline_mode=resident),
            pl.BlockSpec((FFN_HIDDEN, d), lambda i: (0, 0), pipeline_mode=resident),
        ],
        out_specs=pl.BlockSpec((tm, d), lambda i: (i, 0)),
        scratch_shapes=[pltpu.VMEM((tm, FFN_HIDDEN), BF16)],
        compiler_params=pltpu.CompilerParams(
            dimension_semantics=("parallel",), vmem_limit_bytes=VMEM_LIMIT),
        name="norm_swiglu",
    )(x, g, wg, wu, wd)


BIAS_ROWS = 3 * LANES


def _bias_tile_kernel(rb_ref, o_ref):
    h = pl.program_id(0)
    c = lax.broadcasted_iota(I32, (BIAS_ROWS, LANES), 0)
    t = lax.broadcasted_iota(I32, (BIAS_ROWS, LANES), 1)
    dist = jnp.maximum(LANES + t - c, 0)
    exact = REL_BUCKETS // 2
    d_f = jnp.maximum(dist, 1).astype(F32)
    large = exact + (jnp.log(d_f / exact) / math.log(REL_MAX_DIST / exact)
                     * (REL_BUCKETS - exact)).astype(I32)
    large = jnp.minimum(large, REL_BUCKETS - 1)
    bucket = jnp.where(dist < exact, dist, large)
    out = jnp.zeros((BIAS_ROWS, LANES), F32)
    for b in range(REL_BUCKETS):
        out = jnp.where(bucket == b, rb_ref[b, h], out)
    o_ref[...] = (out - rb_ref[REL_BUCKETS - 1, h]) * LOG2E


def _bias_tiles(rel_bias):
    return pl.pallas_call(
        _bias_tile_kernel,
        out_shape=jax.ShapeDtypeStruct((ATT_HEADS, BIAS_ROWS, LANES), F32),
        grid=(ATT_HEADS,),
        in_specs=[pl.BlockSpec(memory_space=pltpu.SMEM)],
        out_specs=pl.BlockSpec((None, BIAS_ROWS, LANES), lambda h: (h, 0, 0)),
        name="rel_bias_tiles",
    )(rel_bias)


def _dsa_kernel(q_ref, k_ref, v_ref, qi_ref, wq_ref, kik_ref, qg_ref, kg_ref, bt_ref, seg_ref,
                segt_ref, o_ref, kn_ref, vt_ref, kib_ref, key_ref, khalf_ref, mfar_ref, s_ref, *, tq, chunk):
    i = pl.program_id(1)
    s_len = k_ref.shape[0]
    n_chunk = (i + chunk) // chunk
    n_far = (i - 1 + chunk - 1) // chunk
    w0 = jnp.maximum(i - 1, 0)

    @pl.when(i == 0)
    def _prepare_keys():
        kg = kg_ref[...]
        for g in range(KV_HEADS):
            kn = _rms(k_ref[:, g * HEAD_DIM:(g + 1) * HEAD_DIM], kg).astype(BF16)
            kn_ref[g] = jnp.concatenate([kn, kn], axis=1)
        for j in range(s_len // LANES):
            for pair in range(KV_HEADS // 2):
                v_t = v_ref[j * LANES:(j + 1) * LANES, pair * LANES:(pair + 1) * LANES].T
                vt_ref[2 * pair, j] = v_t[0:HEAD_DIM].astype(BF16)
                vt_ref[2 * pair + 1, j] = v_t[HEAD_DIM:].astype(BF16)
        ki = kik_ref[:, 0:HEAD_DIM].astype(BF16)
        kib_ref[...] = jnp.concatenate([ki, ki], axis=1)

    ck = chunk * LANES
    krow = lax.broadcasted_iota(I32, (ck, tq), 0)
    tpos = i * tq + lax.broadcasted_iota(I32, (ck, tq), 1)
    low_half = lax.broadcasted_iota(I32, (tq, LANES), 1) < HEAD_DIM

    def split_pair(x):
        return [jnp.where(low_half, x, 0.0).astype(BF16), jnp.where(low_half, 0.0, x).astype(BF16)]

    w_t = wq_ref[...].T
    qi_heads = []
    for pair in range(IDX_HEADS // 2):
        qi_heads += split_pair(qi_ref[:, pair * LANES:(pair + 1) * LANES] * 0.125)
    w_heads = [w_t[HEAD_DIM + h:HEAD_DIM + h + 1, :] * (IDX_HEADS ** -0.5)
               for h in range(IDX_HEADS)]

    def score_chunk(c, _):
        kt = kib_ref[pl.ds(pl.multiple_of(c * ck, ck), ck), :]
        dots = [lax.dot_general(kt, qi_heads[h], NT_DIMS, preferred_element_type=F32)
                for h in range(IDX_HEADS)]
        sc = jnp.zeros((ck, tq), F32)
        for h in range(IDX_HEADS):
            sc = sc + w_heads[h] * jnp.maximum(dots[h], 0.0)
        sc = jnp.where(sc == 0.0, 0.0, sc)
        sc = jnp.where(c * ck + krow <= tpos, sc, -jnp.inf)
        bits = pltpu.bitcast(sc, I32)
        key = bits ^ ((bits >> 31) & 0x7FFFFFFF)
        key_ref[c] = key
        khalf_ref[c] = (key >> 16).astype(I16)
        return 0

    lax.fori_loop(0, n_chunk, score_chunk, 0)

    def count_keys(preds):
        def body(c, cs):
            key = key_ref[c]
            return tuple(
                cnt + jnp.sum(pred(c, key).astype(I32).reshape(ck // 8, 8, tq), axis=0)
                for cnt, pred in zip(cs, preds))
        cs = lax.fori_loop(0, n_chunk, body,
                           tuple(jnp.zeros((8, tq), I32) for _ in preds))
        return [jnp.sum(cnt, axis=0, keepdims=True) for cnt in cs]

    def count_half_ge(cand):
        cand_b = jnp.broadcast_to(cand.astype(I16), (PACKED_ROWS, tq))
        one, zero = jnp.ones((), I16), jnp.zeros((), I16)

        def body(c, cnts):
            half = khalf_ref[c].reshape(ck // PACKED_ROWS, PACKED_ROWS, tq)
            cnts = list(cnts)
            for k in range(ck // PACKED_ROWS):
                cnts[k % len(cnts)] = cnts[k % len(cnts)] + jnp.where(half[k] >= cand_b, one, zero)
            return tuple(cnts)

        cnts = lax.fori_loop(0, n_chunk, body, (jnp.zeros((PACKED_ROWS, tq), I16),) * 4)
        cnt = (cnts[0] + cnts[1]) + (cnts[2] + cnts[3])
        return jnp.sum(cnt.astype(I32), axis=0, keepdims=True)

    def search_half(n_wanted):
        def bit(b, best):
            cand = best + lax.shift_left(jnp.int32(1), 15 - b)
            return jnp.where(count_half_ge(cand) >= n_wanted, cand, best)
        return lax.fori_loop(0, 16, bit, jnp.full((1, tq), I16_MIN, I32))

    thr_hi = search_half(TOPK)
    n_above = count_half_ge(thr_hi + 1)

    def low_halves(c, _):
        key = key_ref[c]
        low = ((key & 0xFFFF) + I16_MIN).astype(I16)
        khalf_ref[c] = jnp.where((key >> 16) == thr_hi, low, jnp.int16(I16_MIN))
        return 0

    lax.fori_loop(0, n_chunk, low_halves, 0)
    thr_lo = search_half(TOPK - n_above)
    thr = (thr_hi << 16) | ((thr_lo - I16_MIN) & 0xFFFF)
    thr_b = jnp.broadcast_to(thr, (ck, tq))
    n_gt, n_eq = count_keys([lambda c, key: key > thr_b, lambda c, key: key == thr_b])
    need = TOPK - n_gt
    tied = (n_eq > need) & (thr != KEY_NEG_INF)
    any_tied = jnp.max(tied.astype(I32)) > 0

    def tie_bit(b, cut):
        cand = cut + lax.shift_left(jnp.int32(1), 10 - b)
        cand_b = jnp.broadcast_to(cand, (ck, tq))
        (cnt,) = count_keys([lambda c, key: (key == thr_b) & (c * ck + krow < cand_b)])
        return jnp.where(cnt < need, cand, cut)

    cut = lax.cond(any_tied,
                   lambda: lax.fori_loop(0, 11, tie_bit, jnp.zeros((1, tq), I32)),
                   lambda: jnp.full((1, tq), s_len, I32))

    def selection_mask(key, spos, tpos):
        thr_k = jnp.broadcast_to(thr, key.shape)
        cut_k = jnp.broadcast_to(cut, key.shape)
        sel = (key > thr_k) | ((key == thr_k) & (spos <= cut_k))
        return jnp.where(sel & (spos <= tpos), 0.0, NEG)

    def mask_chunk(c, _):
        spos = c * ck + krow
        madd = selection_mask(key_ref[c], spos, tpos)
        mfar_ref[c] = jnp.where(spos < (i - 1) * LANES, madd, NEG)
        return 0

    lax.fori_loop(0, n_far, mask_chunk, 0)

    def tile_mask(tile):
        c, u = tile // chunk, tile % chunk
        key = key_ref[c, pl.ds(pl.multiple_of(u * LANES, LANES), LANES), :]
        krow_t = lax.broadcasted_iota(I32, (LANES, tq), 0)
        tpos_t = i * tq + lax.broadcasted_iota(I32, (LANES, tq), 1)
        return selection_mask(key, tile * LANES + krow_t, tpos_t)

    q = q_ref[...]
    sq = q * q
    sq_hi = sq.astype(BF16)
    sq_lo = (sq - sq_hi.astype(F32)).astype(BF16)
    seg = seg_ref[...]
    ssum = (jnp.dot(sq_hi, seg, preferred_element_type=F32)
            + jnp.dot(sq_lo, seg, preferred_element_type=F32))
    rinv = lax.rsqrt(ssum * (1.0 / HEAD_DIM) + EPS)
    pieces, rest = [], rinv
    for _ in range(3):
        piece = rest.astype(BF16)
        pieces.append(piece)
        rest = rest - piece.astype(F32)
    segt = segt_ref[...]
    rinv_b = sum(jnp.dot(piece, segt, preferred_element_type=F32) for piece in pieces)
    qn = q * rinv_b * (qg_ref[...] * (HEAD_DIM ** -0.5 * LOG2E))
    qs = []
    for g in range(KV_HEADS):
        rows = []
        for pair in range(2 * g, 2 * g + 2):
            rows += split_pair(qn[:, pair * LANES:(pair + 1) * LANES])
        qs.append(jnp.concatenate(rows, axis=0))

    def attend(sts, n_keys, kts, v_ts, add_fn):
        groups = range(KV_HEADS)
        for g in groups:
            s_ref[g, 0:n_keys, :] = lax.dot_general(kts[g], qs[g], NT_DIMS,
                                                    preferred_element_type=F32)
        ps, stats = [], []
        for g in groups:
            m, l, _ = sts[g]
            add = jnp.concatenate([add_fn(g * GQA_REP + r) for r in range(GQA_REP)], axis=1)
            m_new = jnp.maximum(m, jnp.max(s_ref[g, 0:n_keys, :] + add, axis=0, keepdims=True))
            alpha = jnp.exp2(m - m_new)
            p = jnp.exp2((s_ref[g, 0:n_keys, :] - m_new) + add)
            stats.append((m_new, alpha * l + jnp.sum(p, axis=0, keepdims=True), alpha))
            ps.append(p.astype(BF16))
        return tuple(
            (stats[g][0], stats[g][1],
             stats[g][2] * sts[g][2] + jnp.dot(v_ts[g], ps[g], preferred_element_type=F32))
            for g in groups)

    def far_chunk(c, sts):
        madd = mfar_ref[c]
        kts = [kn_ref[g, pl.ds(pl.multiple_of(c * ck, ck), ck), :] for g in range(KV_HEADS)]
        v_ts = [jnp.concatenate([vt_ref[g, c * chunk + u] for u in range(chunk)], axis=1)
                for g in range(KV_HEADS)]
        return attend(sts, ck, kts, v_ts, lambda head: madd)

    init = (jnp.full((1, GQA_REP * tq), NEG, F32),
            jnp.zeros((1, GQA_REP * tq), F32),
            jnp.zeros((HEAD_DIM, GQA_REP * tq), F32))
    sts = lax.fori_loop(0, n_far, far_chunk, (init,) * KV_HEADS)

    win_mask = jnp.concatenate([tile_mask(w0), tile_mask(w0 + 1)], axis=0)
    bias_row = pl.multiple_of(jnp.where(i == 0, LANES, 0), LANES)
    kts = [kn_ref[g, pl.ds(pl.multiple_of(w0 * LANES, LANES), 2 * LANES), :]
           for g in range(KV_HEADS)]
    v_ts = [jnp.concatenate([vt_ref[g, w0], vt_ref[g, w0 + 1]], axis=1) for g in range(KV_HEADS)]
    sts = attend(sts, 2 * LANES, kts, v_ts,
                 lambda head: bt_ref[head, pl.ds(bias_row, 2 * LANES), :] + win_mask)
    for g in range(KV_HEADS):
        m, l, acc = sts[g]
        out_t = acc / l
        for pair in range(GQA_REP // 2):
            two = jnp.concatenate(
                [out_t[:, (2 * pair) * tq:(2 * pair + 1) * tq],
                 out_t[:, (2 * pair + 1) * tq:(2 * pair + 2) * tq]], axis=0)
            col = (g * GQA_REP + 2 * pair) * HEAD_DIM
            o_ref[:, col:col + LANES] = two.T.astype(o_ref.dtype)


def _dsa(proj, q_gain, k_gain, bias_tiles, *, tq=128, chunk=4):
    b, s, _ = proj.shape
    n_t = s // LANES
    att_width = ATT_HEADS * HEAD_DIM
    seg = (jnp.arange(att_width)[:, None] // HEAD_DIM == jnp.arange(LANES)[None, :]).astype(BF16)
    q_gain_all = jnp.tile(q_gain, (1, ATT_HEADS))
    return pl.pallas_call(
        functools.partial(_dsa_kernel, tq=tq, chunk=chunk),
        out_shape=jax.ShapeDtypeStruct((b, s, att_width), BF16),
        grid=(b, s // tq),
        in_specs=[
            pl.BlockSpec((None, tq, 1024), lambda bi, i: (bi, i, 0)),
            pl.BlockSpec((None, s, 256), lambda bi, i: (bi, 0, 4)),
            pl.BlockSpec((None, s, 256), lambda bi, i: (bi, 0, 5)),
            pl.BlockSpec((None, tq, 512), lambda bi, i: (bi, i, 3)),
            pl.BlockSpec((None, tq, LANES), lambda bi, i: (bi, i, 16)),
            pl.BlockSpec((None, s, LANES), lambda bi, i: (bi, 0, 16)),
            pl.BlockSpec((1, att_width), lambda bi, i: (0, 0)),
            pl.BlockSpec((1, HEAD_DIM), lambda bi, i: (0, 0)),
            pl.BlockSpec((ATT_HEADS, BIAS_ROWS, LANES), lambda bi, i: (0, 0, 0)),
            pl.BlockSpec((att_width, LANES), lambda bi, i: (0, 0)),
            pl.BlockSpec((LANES, att_width), lambda bi, i: (0, 0)),
        ],
        out_specs=pl.BlockSpec((None, tq, att_width), lambda bi, i: (bi, i, 0)),
        scratch_shapes=[
            pltpu.VMEM((KV_HEADS, s, LANES), BF16),
            pltpu.VMEM((KV_HEADS, n_t, HEAD_DIM, LANES), BF16),
            pltpu.VMEM((s, LANES), BF16),
            pltpu.VMEM((n_t // chunk, chunk * LANES, tq), I32),
            pltpu.VMEM((n_t // chunk, chunk * LANES, tq), I16),
            pltpu.VMEM((n_t // chunk, chunk * LANES, tq), F32),
            pltpu.VMEM((KV_HEADS, chunk * LANES, GQA_REP * tq), F32),
        ],
        compiler_params=pltpu.CompilerParams(
            dimension_semantics=("parallel", "arbitrary"), vmem_limit_bytes=VMEM_LIMIT),
        name="dsa_attention",
    )(proj, proj, proj, proj, proj, proj, q_gain_all, k_gain, bias_tiles, seg, seg.T)


def kernel(x, norm_mix, norm_ffn, ev_w_in, ev_conv_w, ev_w_out, od_w_in, od_q_gain,
           od_k_gain, od_w_out, rel_bias, ffn_w_gate, ffn_w_up, ffn_w_down):
    b, s, d = x.shape
    n_tok = b * s
    bf = lambda w: w.astype(BF16)

    proj0 = _norm_matmul(x.reshape(n_tok, d), norm_mix[0:1], bf(ev_w_in[0]))
    proj0 = proj0.reshape(b, s, -1)
    a_out = _stickbreak(proj0)
    x1 = _mix0(a_out, proj0, ev_conv_w[0], bf(ev_w_out[0]), x)
    x2 = _swiglu(x1.reshape(n_tok, d), norm_ffn[0:1], bf(ffn_w_gate[0]), bf(ffn_w_up[0]),
                 bf(ffn_w_down[0]))

    w_in1 = jnp.pad(od_w_in[0], ((0, 0), (0, ODD_IN_PAD - od_w_in.shape[2])))
    proj1 = _norm_matmul(x2, norm_mix[1:2], bf(w_in1)).reshape(b, s, ODD_IN_PAD)
    att = _dsa(proj1, od_q_gain[0:1], od_k_gain[0:1], _bias_tiles(rel_bias))
    x3 = _proj_res(att.reshape(n_tok, -1), bf(od_w_out[0]), x2)
    x4 = _swiglu(x3, norm_ffn[1:2], bf(ffn_w_gate[1]), bf(ffn_w_up[1]), bf(ffn_w_down[1]))
    return x4.reshape(b, s, d)
```

```python
import functools
import math

import jax
import jax.numpy as jnp
from jax import lax
from jax.experimental import pallas as pl
from jax.experimental.pallas import tpu as pltpu

F32 = jnp.float32
BF16 = jnp.bfloat16
I32 = jnp.int32

D_MODEL = 1024
SEQ = 2048
HEAD_DIM = 64
SB_WIDTH = 512
CONV_CH = 512
ATT_HEADS = 16
KV_HEADS = 4
GQA_REP = 4
IDX_HEADS = 8
TOPK = 256
REL_BUCKETS = 32
REL_MAX_DIST = 128
FFN_HIDDEN = 2816
EPS = 1e-6
ODD_IN_PAD = 2176

LOG2E = 1.4426950408889634
LANES = 128
VMEM_LIMIT = 56 * 1024 * 1024
NEG = -1e30
INT_MIN = -(2 ** 31)
V_ROWS = HEAD_DIM + 16
UNDERFLOW_MARGIN = 160.0
NORM_SLACK = 1.03
KEY_NEG_INF = -(2 ** 31) + 0x7FFFFF

NT_DIMS = (((1,), (1,)), ((), ()))


def _rms(x, g):
    ms = jnp.mean(x * x, axis=-1, keepdims=True)
    return x * lax.rsqrt(ms + EPS) * g


def _norm_matmul_kernel(x_ref, g_ref, w_ref, o_ref, *, tn):
    h = _rms(x_ref[...], g_ref[...]).astype(BF16)
    n = w_ref.shape[1]
    for start in range(0, n, tn):
        size = min(tn, n - start)
        o_ref[:, start:start + size] = jnp.dot(
            h, w_ref[:, start:start + size], preferred_element_type=F32)


def _norm_matmul(x, g, w, *, tm=512, tn=512):
    m, d = x.shape
    n = w.shape[1]
    return pl.pallas_call(
        functools.partial(_norm_matmul_kernel, tn=tn),
        out_shape=jax.ShapeDtypeStruct((m, n), F32),
        grid=(m // tm,),
        in_specs=[
            pl.BlockSpec((tm, d), lambda i: (i, 0)),
            pl.BlockSpec((1, d), lambda i: (0, 0)),
            pl.BlockSpec((d, n), lambda i: (0, 0)),
        ],
        out_specs=pl.BlockSpec((tm, n), lambda i: (i, 0)),
        compiler_params=pltpu.CompilerParams(
            dimension_semantics=("parallel",), vmem_limit_bytes=VMEM_LIMIT),
        name="norm_matmul",
    )(x, g, w)


def _sb_kernel(q_ref, k_ref, v_ref, o_ref, vt_ref, kmax_ref, *, tq, pairs):
    qi = pl.program_id(2)
    n_sub = tq // LANES
    s_len = k_ref.shape[0]
    width = pairs * LANES
    head_of = (lax.broadcasted_iota(I32, (width, LANES), 0) // HEAD_DIM
               == lax.broadcasted_iota(I32, (width, LANES), 1)).astype(BF16)

    def max_sq_norm(x):
        sq = (x * x).astype(BF16)
        return jnp.max(jnp.dot(sq, head_of, preferred_element_type=F32), axis=0, keepdims=True)

    @pl.when(qi == 0)
    def _per_sequence_setup():
        for p in range(pairs):
            for j in range(s_len // LANES):
                vt_ref[p, j] = v_ref[j * LANES:(j + 1) * LANES,
                                     p * LANES:(p + 1) * LANES].T.astype(BF16)
        kmax = jnp.zeros((1, LANES), F32)
        for r in range(0, s_len, tq):
            kmax = jnp.maximum(kmax, max_sq_norm(k_ref[r:r + tq, :]))
        kmax_ref[...] = kmax

    lane = lax.broadcasted_iota(I32, (tq, LANES), 1)
    krow = lax.broadcasted_iota(I32, (tq, tq), 0)
    qcol = lax.broadcasted_iota(I32, (tq, tq), 1)
    from_here = (qcol >= krow).astype(BF16)
    causal = krow < qcol
    q_heads = []
    for p in range(pairs):
        q = q_ref[:, p * LANES:(p + 1) * LANES] * (HEAD_DIM ** -0.5 * LOG2E)
        for h in range(2):
            in_head = (lane >= HEAD_DIM * h) & (lane < HEAD_DIM * (h + 1))
            q_heads.append(jnp.where(in_head, q, 0.0).astype(BF16))

    def block(kb, st, masked):
        start = pl.multiple_of(kb * tq, tq)
        heads = range(2 * pairs)
        kblks = [k_ref[pl.ds(start, tq), p * LANES:(p + 1) * LANES].astype(BF16)
                 for p in range(pairs)]
        v_ts = [jnp.concatenate([vt_ref[p, kb * n_sub + u] for u in range(n_sub)], axis=1)
                for p in range(pairs)]
        zs = [lax.dot_general(kblks[idx // 2], q_heads[idx], NT_DIMS, preferred_element_type=F32)
              for idx in heads]
        softplus, his, los = [], [], []
        for z in zs:
            sp = jnp.maximum(z, 0.0) + jnp.log2(1.0 + jnp.exp2(-jnp.abs(z)))
            if masked:
                sp = jnp.where(causal, sp, 0.0)
            hi = sp.astype(BF16)
            softplus.append(sp)
            his.append(hi)
            los.append((sp - hi.astype(F32)).astype(BF16))
        since = [jnp.dot(from_here, his[idx], preferred_element_type=F32)
                 + jnp.dot(from_here, los[idx], preferred_element_type=F32) for idx in heads]
        ws = []
        for idx in heads:
            w = jnp.exp2(zs[idx] - since[idx] - st[2 * idx])
            if masked:
                w = jnp.where(causal, w, 0.0)
            ws.append(w.astype(BF16))
        new = []
        for idx in heads:
            h = idx % 2
            acc = st[2 * idx + 1] + jnp.dot(v_ts[idx // 2][h * HEAD_DIM:(h + 1) * HEAD_DIM, :],
                                            ws[idx], preferred_element_type=F32)
            carry = st[2 * idx] + jnp.sum(softplus[idx], axis=0, keepdims=True)
            new += [carry, acc]
        return tuple(new)

    z_bound = jnp.sqrt(max_sq_norm(q_ref[...]) * kmax_ref[...]) * (HEAD_DIM ** -0.5 * LOG2E * NORM_SLACK)
    z_bounds = [jnp.broadcast_to(z_bound[:, idx:idx + 1], (1, tq)) for idx in range(2 * pairs)]

    def all_underflow(st):
        slack = st[0] - z_bounds[0]
        for idx in range(1, 2 * pairs):
            slack = jnp.minimum(slack, st[2 * idx] - z_bounds[idx])
        return jnp.min(slack) > UNDERFLOW_MARGIN

    def walk(state):
        n, _, st = state
        st = block(qi - 1 - n, st, False)
        return n + 1, all_underflow(st), st

    st = (jnp.zeros((1, tq), F32), jnp.zeros((HEAD_DIM, tq), F32)) * (2 * pairs)
    st = block(qi, st, True)
    _, _, st = lax.while_loop(lambda state: (state[0] < qi) & jnp.logical_not(state[1]), walk,
                              (jnp.int32(0), all_underflow(st), st))
    for p in range(pairs):
        out_t = jnp.concatenate([st[4 * p + 1], st[4 * p + 3]], axis=0)
        for u in range(n_sub):
            o_ref[u * LANES:(u + 1) * LANES, p * LANES:(p + 1) * LANES] = (
                out_t[:, u * LANES:(u + 1) * LANES].T.astype(o_ref.dtype))


def _stickbreak(proj, *, tq=256, pairs=4):
    b, s, _ = proj.shape
    width = pairs * LANES
    n_steps = SB_WIDTH // width
    return pl.pallas_call(
        functools.partial(_sb_kernel, tq=tq, pairs=pairs),
        out_shape=jax.ShapeDtypeStruct((b, s, SB_WIDTH), BF16),
        grid=(b, n_steps, s // tq),
        in_specs=[
            pl.BlockSpec((None, tq, width), lambda bi, hp, qi: (bi, qi, hp)),
            pl.BlockSpec((None, s, width), lambda bi, hp, qi: (bi, 0, n_steps + hp)),
            pl.BlockSpec((None, s, width), lambda bi, hp, qi: (bi, 0, 2 * n_steps + hp)),
        ],
        out_specs=pl.BlockSpec((None, tq, width), lambda bi, hp, qi: (bi, qi, hp)),
        scratch_shapes=[pltpu.VMEM((pairs, s // LANES, LANES, LANES), BF16),
                        pltpu.VMEM((1, LANES), F32)],
        compiler_params=pltpu.CompilerParams(
            dimension_semantics=("parallel", "parallel", "arbitrary"),
            vmem_limit_bytes=VMEM_LIMIT),
        name="stickbreak",
    )(proj, proj, proj)


def _mix0_kernel(a_ref, bg_ref, cg_ref, u_ref, cgh_ref, uh_ref, cw_ref, w_ref, x_ref, o_ref,
                 *, tm):
    i = pl.program_id(1)
    g = cg_ref[...] * u_ref[...]
    gh = cgh_ref[...] * uh_ref[...]
    gh = jnp.where(i == 0, 0.0, gh)
    row = lax.broadcasted_iota(I32, g.shape, 0)
    g1 = jnp.where(row == 0, gh[7:8, :], pltpu.roll(g, 1, axis=0))
    g2 = pltpu.roll(g, 2, axis=0)
    g2 = jnp.where(row == 0, gh[6:7, :], jnp.where(row == 1, gh[7:8, :], g2))
    cw = cw_ref[...]
    y = bg_ref[...] * (cw[0:1, :] * g2 + cw[1:2, :] * g1 + cw[2:3, :] * g)
    acc = jnp.dot(a_ref[...].astype(BF16), w_ref[0:SB_WIDTH, :], preferred_element_type=F32)
    acc = acc + jnp.dot(y.astype(BF16), w_ref[SB_WIDTH:, :], preferred_element_type=F32)
    o_ref[...] = x_ref[...] + acc


def _mix0(a_out, proj, conv_w, w_out, x, *, tm=512):
    b, s, d = x.shape
    halo = lambda col: (lambda bi, i: (bi, jnp.maximum(i * (tm // 8) - 1, 0), col))
    return pl.pallas_call(
        functools.partial(_mix0_kernel, tm=tm),
        out_shape=jax.ShapeDtypeStruct((b, s, d), F32),
        grid=(b, s // tm),
        in_specs=[
            pl.BlockSpec((None, tm, SB_WIDTH), lambda bi, i: (bi, i, 0)),
            pl.BlockSpec((None, tm, CONV_CH), lambda bi, i: (bi, i, 3)),
            pl.BlockSpec((None, tm, CONV_CH), lambda bi, i: (bi, i, 4)),
            pl.BlockSpec((None, tm, CONV_CH), lambda bi, i: (bi, i, 5)),
            pl.BlockSpec((None, 8, CONV_CH), halo(4)),
            pl.BlockSpec((None, 8, CONV_CH), halo(5)),
            pl.BlockSpec((3, CONV_CH), lambda bi, i: (0, 0)),
            pl.BlockSpec((SB_WIDTH + CONV_CH, d), lambda bi, i: (0, 0)),
            pl.BlockSpec((None, tm, d), lambda bi, i: (bi, i, 0)),
        ],
        out_specs=pl.BlockSpec((None, tm, d), lambda bi, i: (bi, i, 0)),
        compiler_params=pltpu.CompilerParams(
            dimension_semantics=("parallel", "arbitrary"), vmem_limit_bytes=VMEM_LIMIT),
        name="mix0_out",
    )(a_out, proj, proj, proj, proj, proj, conv_w, w_out, x)


def _proj_res_kernel(a_ref, w_ref, x_ref, o_ref):
    o_ref[...] = x_ref[...] + jnp.dot(a_ref[...].astype(BF16), w_ref[...],
                                       preferred_element_type=F32)


def _proj_res(a, w, x, *, tm=512):
    m, k = a.shape
    n = w.shape[1]
    return pl.pallas_call(
        _proj_res_kernel,
        out_shape=jax.ShapeDtypeStruct((m, n), F32),
        grid=(m // tm,),
        in_specs=[
            pl.BlockSpec((tm, k), lambda i: (i, 0)),
            pl.BlockSpec((k, n), lambda i: (0, 0)),
            pl.BlockSpec((tm, n), lambda i: (i, 0)),
        ],
        out_specs=pl.BlockSpec((tm, n), lambda i: (i, 0)),
        compiler_params=pltpu.CompilerParams(
            dimension_semantics=("parallel",), vmem_limit_bytes=VMEM_LIMIT),
        name="proj_residual",
    )(a, w, x)


def _swiglu_kernel(x_ref, g_ref, wg_ref, wu_ref, wd_ref, o_ref, act_ref, *, th):
    x = x_ref[...]
    h = _rms(x, g_ref[...]).astype(BF16)
    for start in range(0, FFN_HIDDEN, th):
        hg = jnp.dot(h, wg_ref[:, start:start + th], preferred_element_type=F32)
        hu = jnp.dot(h, wu_ref[:, start:start + th], preferred_element_type=F32)
        sig = 1.0 / (1.0 + jnp.exp(-hg))
        act_ref[:, start:start + th] = (hg * sig * hu).astype(BF16)
    o_ref[...] = x + jnp.dot(act_ref[...], wd_ref[...], preferred_element_type=F32)


def _swiglu(x, g, wg, wu, wd, *, tm=512, th=256):
    m, d = x.shape
    resident = pl.Buffered(1)
    return pl.pallas_call(
        functools.partial(_swiglu_kernel, th=th),
        out_shape=jax.ShapeDtypeStruct((m, d), F32),
        grid=(m // tm,),
        in_specs=[
            pl.BlockSpec((tm, d), lambda i: (i, 0)),
            pl.BlockSpec((1, d), lambda i: (0, 0)),
            pl.BlockSpec((d, FFN_HIDDEN), lambda i: (0, 0), pipeline_mode=resident),
            pl.BlockSpec((d, FFN_HIDDEN), lambda i: (0, 0), pipeline_mode=resident),
            pl.BlockSpec((FFN_HIDDEN, d), lambda i: (0, 0), pipeline_mode=resident),
        ],
        out_specs=pl.BlockSpec((tm, d), lambda i: (i, 0)),
        scratch_shapes=[pltpu.VMEM((tm, FFN_HIDDEN), BF16)],
        compiler_params=pltpu.CompilerParams(
            dimension_semantics=("parallel",), vmem_limit_bytes=VMEM_LIMIT),
        name="norm_swiglu",
    )(x, g, wg, wu, wd)


BIAS_ROWS = 3 * LANES


def _bias_tile_kernel(rb_ref, o_ref):
    h = pl.program_id(0)
    c = lax.broadcasted_iota(I32, (BIAS_ROWS, LANES), 0)
    t = lax.broadcasted_iota(I32, (BIAS_ROWS, LANES), 1)
    dist = jnp.maximum(LANES + t - c, 0)
    exact = REL_BUCKETS // 2
    d_f = jnp.maximum(dist, 1).astype(F32)
    large = exact + (jnp.log(d_f / exact) / math.log(REL_MAX_DIST / exact)
                     * (REL_BUCKETS - exact)).astype(I32)
    large = jnp.minimum(large, REL_BUCKETS - 1)
    bucket = jnp.where(dist < exact, dist, large)
    out = jnp.zeros((BIAS_ROWS, LANES), F32)
    for b in range(REL_BUCKETS):
        out = jnp.where(bucket == b, rb_ref[b, h], out)
    o_ref[...] = (out - rb_ref[REL_BUCKETS - 1, h]) * LOG2E


def _bias_tiles(rel_bias):
    return pl.pallas_call(
        _bias_tile_kernel,
        out_shape=jax.ShapeDtypeStruct((ATT_HEADS, BIAS_ROWS, LANES), F32),
        grid=(ATT_HEADS,),
        in_specs=[pl.BlockSpec(memory_space=pltpu.SMEM)],
        out_specs=pl.BlockSpec((None, BIAS_ROWS, LANES), lambda h: (h, 0, 0)),
        name="rel_bias_tiles",
    )(rel_bias)


def _dsa_kernel(q_ref, k_ref, v_ref, qi_ref, wq_ref, kik_ref, qg_ref, kg_ref, bt_ref, seg_ref,
                segt_ref, o_ref, kn_ref, vt_ref, kib_ref, key_ref, mfar_ref, s_ref, *, tq, chunk):
    i = pl.program_id(1)
    s_len = k_ref.shape[0]
    n_chunk = (i + chunk) // chunk
    n_far = (i - 1 + chunk - 1) // chunk
    w0 = jnp.maximum(i - 1, 0)

    @pl.when(i == 0)
    def _prepare_keys():
        kg = kg_ref[...]
        for g in range(KV_HEADS):
            kn = _rms(k_ref[:, g * HEAD_DIM:(g + 1) * HEAD_DIM], kg).astype(BF16)
            kn_ref[g] = jnp.concatenate([kn, kn], axis=1)
        for j in range(s_len // LANES):
            for pair in range(KV_HEADS // 2):
                v_t = v_ref[j * LANES:(j + 1) * LANES, pair * LANES:(pair + 1) * LANES].T
                vt_ref[2 * pair, j, 0:HEAD_DIM] = v_t[0:HEAD_DIM].astype(BF16)
                vt_ref[2 * pair + 1, j, 0:HEAD_DIM] = v_t[HEAD_DIM:].astype(BF16)
        vt_ref[:, :, HEAD_DIM:V_ROWS] = jnp.ones((KV_HEADS, s_len // LANES, V_ROWS - HEAD_DIM, LANES),
                                                 BF16)
        ki = kik_ref[:, 0:HEAD_DIM].astype(BF16)
        kib_ref[...] = jnp.concatenate([ki, ki], axis=1)

    ck = chunk * LANES
    krow = lax.broadcasted_iota(I32, (ck, tq), 0)
    tpos = i * tq + lax.broadcasted_iota(I32, (ck, tq), 1)
    low_half = lax.broadcasted_iota(I32, (tq, LANES), 1) < HEAD_DIM

    def split_pair(x):
        return [jnp.where(low_half, x, 0.0).astype(BF16), jnp.where(low_half, 0.0, x).astype(BF16)]

    w_t = wq_ref[...].T
    qi_pairs = [jnp.concatenate(split_pair(qi_ref[:, pair * LANES:(pair + 1) * LANES] * 0.125), axis=0)
                for pair in range(IDX_HEADS // 2)]
    w_heads = [w_t[HEAD_DIM + h:HEAD_DIM + h + 1, :] * (IDX_HEADS ** -0.5)
               for h in range(IDX_HEADS)]

    def score_chunk(c, _):
        kt = kib_ref[pl.ds(pl.multiple_of(c * ck, ck), ck), :]
        dots = [lax.dot_general(kt, qi_pair, NT_DIMS, preferred_element_type=F32)
                for qi_pair in qi_pairs]
        sc = jnp.zeros((ck, tq), F32)
        for h in range(IDX_HEADS):
            d = dots[h // 2][:, (h % 2) * tq:(h % 2 + 1) * tq]
            sc = sc + w_heads[h] * jnp.maximum(d, 0.0)
        sc = jnp.where(sc == 0.0, 0.0, sc)
        sc = jnp.where(c * ck + krow <= tpos, sc, -jnp.inf)
        bits = pltpu.bitcast(sc, I32)
        key_ref[c] = bits ^ ((bits >> 31) & 0x7FFFFFFF)
        return 0

    lax.fori_loop(0, n_chunk, score_chunk, 0)

    def count_keys(preds):
        def body(c, cs):
            key = key_ref[c]
            return tuple(
                cnt + jnp.sum(pred(c, key).astype(I32).reshape(ck // 8, 8, tq), axis=0)
                for cnt, pred in zip(cs, preds))
        cs = lax.fori_loop(0, n_chunk, body,
                           tuple(jnp.zeros((8, tq), I32) for _ in preds))
        return [jnp.sum(cnt, axis=0, keepdims=True) for cnt in cs]

    def thr_bit(b, thr):
        cand = thr + lax.shift_left(jnp.int32(1), 31 - b)
        cand_b = jnp.broadcast_to(cand, (ck, tq))
        (cnt,) = count_keys([lambda c, key: key >= cand_b])
        return jnp.where(cnt >= TOPK, cand, thr)

    thr = lax.fori_loop(0, 32, thr_bit, jnp.full((1, tq), INT_MIN, I32))
    thr_b = jnp.broadcast_to(thr, (ck, tq))
    n_gt, n_eq = count_keys([lambda c, key: key > thr_b, lambda c, key: key == thr_b])
    need = TOPK - n_gt
    tied = (n_eq > need) & (thr != KEY_NEG_INF)
    any_tied = jnp.max(tied.astype(I32)) > 0

    def tie_bit(b, cut):
        cand = cut + lax.shift_left(jnp.int32(1), 10 - b)
        cand_b = jnp.broadcast_to(cand, (ck, tq))
        (cnt,) = count_keys([lambda c, key: (key == thr_b) & (c * ck + krow < cand_b)])
        return jnp.where(cnt < need, cand, cut)

    cut = lax.cond(any_tied,
                   lambda: lax.fori_loop(0, 11, tie_bit, jnp.zeros((1, tq), I32)),
                   lambda: jnp.full((1, tq), s_len, I32))

    def selection_mask(key, spos, tpos):
        thr_k = jnp.broadcast_to(thr, key.shape)
        cut_k = jnp.broadcast_to(cut, key.shape)
        sel = (key > thr_k) | ((key == thr_k) & (spos <= cut_k))
        return jnp.where(sel & (spos <= tpos), 0.0, NEG)

    def mask_chunk(c, _):
        spos = c * ck + krow
        madd = selection_mask(key_ref[c], spos, tpos)
        mfar_ref[c] = jnp.where(spos < (i - 1) * LANES, madd, NEG)
        return 0

    lax.fori_loop(0, n_far, mask_chunk, 0)

    def tile_mask(tile):
        c, u = tile // chunk, tile % chunk
        key = key_ref[c, pl.ds(pl.multiple_of(u * LANES, LANES), LANES), :]
        krow_t = lax.broadcasted_iota(I32, (LANES, tq), 0)
        tpos_t = i * tq + lax.broadcasted_iota(I32, (LANES, tq), 1)
        return selection_mask(key, tile * LANES + krow_t, tpos_t)

    q = q_ref[...]
    sq = q * q
    sq_hi = sq.astype(BF16)
    sq_lo = (sq - sq_hi.astype(F32)).astype(BF16)
    seg = seg_ref[...]
    ssum = (jnp.dot(sq_hi, seg, preferred_element_type=F32)
            + jnp.dot(sq_lo, seg, preferred_element_type=F32))
    rinv = lax.rsqrt(ssum * (1.0 / HEAD_DIM) + EPS)
    pieces, rest = [], rinv
    for _ in range(3):
        piece = rest.astype(BF16)
        pieces.append(piece)
        rest = rest - piece.astype(F32)
    segt = segt_ref[...]
    rinv_b = sum(jnp.dot(piece, segt, preferred_element_type=F32) for piece in pieces)
    qn = q * rinv_b * (qg_ref[...] * (HEAD_DIM ** -0.5 * LOG2E))
    qs = []
    for g in range(KV_HEADS):
        rows = []
        for pair in range(2 * g, 2 * g + 2):
            rows += split_pair(qn[:, pair * LANES:(pair + 1) * LANES])
        qs.append(jnp.concatenate(rows, axis=0))

    groups = range(KV_HEADS)

    def scores_into(slot, n_keys, key_start, add_fn):
        for g in groups:
            kt = kn_ref[g, pl.ds(key_start, n_keys), :]
            add = jnp.concatenate([add_fn(g * GQA_REP + r) for r in range(GQA_REP)], axis=1)
            s_ref[slot, g, 0:n_keys, :] = lax.dot_general(kt, qs[g], NT_DIMS,
                                                          preferred_element_type=F32) + add

    def softmax_update(slot, n_keys, first_tile, sts):
        ps, stats = [], []
        for g in groups:
            m, _ = sts[g]
            m_new = jnp.maximum(m, jnp.max(s_ref[slot, g, 0:n_keys, :], axis=0, keepdims=True))
            stats.append((m_new, jnp.exp2(m - m_new)))
            ps.append(jnp.exp2(s_ref[slot, g, 0:n_keys, :] - m_new).astype(BF16))
        new = []
        for g in groups:
            v_t = jnp.concatenate([vt_ref[g, first_tile + u] for u in range(n_keys // LANES)],
                                  axis=1)
            new.append((stats[g][0], stats[g][1] * sts[g][1]
                        + jnp.dot(v_t, ps[g], preferred_element_type=F32)))
        return tuple(new)

    def far_scores(slot, c):
        madd = mfar_ref[c]
        scores_into(slot, ck, pl.multiple_of(c * ck, ck), lambda head: madd)

    def far_chunk(c, sts):
        far_scores(0, c)
        return softmax_update(0, ck, c * chunk, sts)

    init = (jnp.full((1, GQA_REP * tq), NEG, F32), jnp.zeros((V_ROWS, GQA_REP * tq), F32))
    sts = lax.fori_loop(0, n_far, far_chunk, (init,) * KV_HEADS)

    win_mask = jnp.concatenate([tile_mask(w0), tile_mask(w0 + 1)], axis=0)
    bias_row = pl.multiple_of(jnp.where(i == 0, LANES, 0), LANES)
    scores_into(0, 2 * LANES, pl.multiple_of(w0 * LANES, LANES),
                lambda head: bt_ref[head, pl.ds(bias_row, 2 * LANES), :] + win_mask)
    sts = softmax_update(0, 2 * LANES, w0, sts)
    for g in range(KV_HEADS):
        acc = sts[g][1]
        out_t = acc[0:HEAD_DIM] / acc[HEAD_DIM:HEAD_DIM + 1]
        for pair in range(GQA_REP // 2):
            two = jnp.concatenate(
                [out_t[:, (2 * pair) * tq:(2 * pair + 1) * tq],
                 out_t[:, (2 * pair + 1) * tq:(2 * pair + 2) * tq]], axis=0)
            col = (g * GQA_REP + 2 * pair) * HEAD_DIM
            o_ref[:, col:col + LANES] = two.T.astype(o_ref.dtype)


def _dsa(proj, q_gain, k_gain, bias_tiles, *, tq=128, chunk=4):
    b, s, _ = proj.shape
    n_t = s // LANES
    att_width = ATT_HEADS * HEAD_DIM
    seg = (jnp.arange(att_width)[:, None] // HEAD_DIM == jnp.arange(LANES)[None, :]).astype(BF16)
    q_gain_all = jnp.tile(q_gain, (1, ATT_HEADS))
    return pl.pallas_call(
        functools.partial(_dsa_kernel, tq=tq, chunk=chunk),
        out_shape=jax.ShapeDtypeStruct((b, s, att_width), BF16),
        grid=(b, s // tq),
        in_specs=[
            pl.BlockSpec((None, tq, 1024), lambda bi, i: (bi, i, 0)),
            pl.BlockSpec((None, s, 256), lambda bi, i: (bi, 0, 4)),
            pl.BlockSpec((None, s, 256), lambda bi, i: (bi, 0, 5)),
            pl.BlockSpec((None, tq, 512), lambda bi, i: (bi, i, 3)),
            pl.BlockSpec((None, tq, LANES), lambda bi, i: (bi, i, 16)),
            pl.BlockSpec((None, s, LANES), lambda bi, i: (bi, 0, 16)),
            pl.BlockSpec((1, att_width), lambda bi, i: (0, 0)),
            pl.BlockSpec((1, HEAD_DIM), lambda bi, i: (0, 0)),
            pl.BlockSpec((ATT_HEADS, BIAS_ROWS, LANES), lambda bi, i: (0, 0, 0)),
            pl.BlockSpec((att_width, LANES), lambda bi, i: (0, 0)),
            pl.BlockSpec((LANES, att_width), lambda bi, i: (0, 0)),
        ],
        out_specs=pl.BlockSpec((None, tq, att_width), lambda bi, i: (bi, i, 0)),
        scratch_shapes=[
            pltpu.VMEM((KV_HEADS, s, LANES), BF16),
            pltpu.VMEM((KV_HEADS, n_t, V_ROWS, LANES), BF16),
            pltpu.VMEM((s, LANES), BF16),
            pltpu.VMEM((n_t // chunk, chunk * LANES, tq), I32),
            pltpu.VMEM((n_t // chunk, chunk * LANES, tq), F32),
            pltpu.VMEM((1, KV_HEADS, chunk * LANES, GQA_REP * tq), F32),
        ],
        compiler_params=pltpu.CompilerParams(
            dimension_semantics=("parallel", "arbitrary"), vmem_limit_bytes=VMEM_LIMIT),
        name="dsa_attention",
    )(proj, proj, proj, proj, proj, proj, q_gain_all, k_gain, bias_tiles, seg, seg.T)


def kernel(x, norm_mix, norm_ffn, ev_w_in, ev_conv_w, ev_w_out, od_w_in, od_q_gain,
           od_k_gain, od_w_out, rel_bias, ffn_w_gate, ffn_w_up, ffn_w_down):
    b, s, d = x.shape
    n_tok = b * s
    bf = lambda w: w.astype(BF16)

    proj0 = _norm_matmul(x.reshape(n_tok, d), norm_mix[0:1], bf(ev_w_in[0]))
    proj0 = proj0.reshape(b, s, -1)
    a_out = _stickbreak(proj0)
    x1 = _mix0(a_out, proj0, ev_conv_w[0], bf(ev_w_out[0]), x)
    x2 = _swiglu(x1.reshape(n_tok, d), norm_ffn[0:1], bf(ffn_w_gate[0]), bf(ffn_w_up[0]),
                 bf(ffn_w_down[0]))

    w_in1 = jnp.pad(od_w_in[0], ((0, 0), (0, ODD_IN_PAD - od_w_in.shape[2])))
    proj1 = _norm_matmul(x2, norm_mix[1:2], bf(w_in1)).reshape(b, s, ODD_IN_PAD)
    att = _dsa(proj1, od_q_gain[0:1], od_k_gain[0:1], _bias_tiles(rel_bias))
    x3 = _proj_res(att.reshape(n_tok, -1), bf(od_w_out[0]), x2)
    x4 = _swiglu(x3, norm_ffn[1:2], bf(ffn_w_gate[1]), bf(ffn_w_up[1]), bf(ffn_w_down[1]))
    return x4.reshape(b, s, d)
```

```python
import functools
import math

import jax
import jax.numpy as jnp
from jax import lax
from jax.experimental import pallas as pl
from jax.experimental.pallas import tpu as pltpu

F32 = jnp.float32
BF16 = jnp.bfloat16
I32 = jnp.int32

D_MODEL = 1024
SEQ = 2048
HEAD_DIM = 64
SB_WIDTH = 512
CONV_CH = 512
ATT_HEADS = 16
KV_HEADS = 4
GQA_REP = 4
IDX_HEADS = 8
TOPK = 256
REL_BUCKETS = 32
REL_MAX_DIST = 128
FFN_HIDDEN = 2816
EPS = 1e-6
ODD_IN_PAD = 2176

LOG2E = 1.4426950408889634
LANES = 128
VMEM_LIMIT = 56 * 1024 * 1024
NEG = -1e30
INT_MIN = -(2 ** 31)
V_ROWS = HEAD_DIM + 16
UNDERFLOW_MARGIN = 160.0
NORM_SLACK = 1.03
KEY_NEG_INF = -(2 ** 31) + 0x7FFFFF

NT_DIMS = (((1,), (1,)), ((), ()))


def _rms(x, g):
    ms = jnp.mean(x * x, axis=-1, keepdims=True)
    return x * lax.rsqrt(ms + EPS) * g


def _norm_matmul_kernel(x_ref, g_ref, w_ref, o_ref, *, tn):
    h = _rms(x_ref[...], g_ref[...]).astype(BF16)
    n = w_ref.shape[1]
    for start in range(0, n, tn):
        size = min(tn, n - start)
        o_ref[:, start:start + size] = jnp.dot(
            h, w_ref[:, start:start + size], preferred_element_type=F32)


def _norm_matmul(x, g, w, *, tm=512, tn=512):
    m, d = x.shape
    n = w.shape[1]
    return pl.pallas_call(
        functools.partial(_norm_matmul_kernel, tn=tn),
        out_shape=jax.ShapeDtypeStruct((m, n), F32),
        grid=(m // tm,),
        in_specs=[
            pl.BlockSpec((tm, d), lambda i: (i, 0)),
            pl.BlockSpec((1, d), lambda i: (0, 0)),
            pl.BlockSpec((d, n), lambda i: (0, 0)),
        ],
        out_specs=pl.BlockSpec((tm, n), lambda i: (i, 0)),
        compiler_params=pltpu.CompilerParams(
            dimension_semantics=("parallel",), vmem_limit_bytes=VMEM_LIMIT),
        name="norm_matmul",
    )(x, g, w)


def _sb_kernel(q_ref, k_ref, v_ref, o_ref, vt_ref, kmax_ref, *, tq, pairs):
    qi = pl.program_id(2)
    n_sub = tq // LANES
    s_len = k_ref.shape[0]
    width = pairs * LANES
    head_of = (lax.broadcasted_iota(I32, (width, LANES), 0) // HEAD_DIM
               == lax.broadcasted_iota(I32, (width, LANES), 1)).astype(BF16)

    def max_sq_norm(x):
        sq = (x * x).astype(BF16)
        return jnp.max(jnp.dot(sq, head_of, preferred_element_type=F32), axis=0, keepdims=True)

    @pl.when(qi == 0)
    def _per_sequence_setup():
        for p in range(pairs):
            for j in range(s_len // LANES):
                vt_ref[p, j] = v_ref[j * LANES:(j + 1) * LANES,
                                     p * LANES:(p + 1) * LANES].T.astype(BF16)
        kmax = jnp.zeros((1, LANES), F32)
        for r in range(0, s_len, tq):
            kmax = jnp.maximum(kmax, max_sq_norm(k_ref[r:r + tq, :]))
        kmax_ref[...] = kmax

    lane = lax.broadcasted_iota(I32, (tq, LANES), 1)
    krow = lax.broadcasted_iota(I32, (tq, tq), 0)
    qcol = lax.broadcasted_iota(I32, (tq, tq), 1)
    from_here = (qcol >= krow).astype(BF16)
    causal = krow < qcol
    q_heads = []
    for p in range(pairs):
        q = q_ref[:, p * LANES:(p + 1) * LANES] * (HEAD_DIM ** -0.5 * LOG2E)
        for h in range(2):
            in_head = (lane >= HEAD_DIM * h) & (lane < HEAD_DIM * (h + 1))
            q_heads.append(jnp.where(in_head, q, 0.0).astype(BF16))

    def block(kb, st, masked):
        start = pl.multiple_of(kb * tq, tq)
        heads = range(2 * pairs)
        kblks = [k_ref[pl.ds(start, tq), p * LANES:(p + 1) * LANES].astype(BF16)
                 for p in range(pairs)]
        v_ts = [jnp.concatenate([vt_ref[p, kb * n_sub + u] for u in range(n_sub)], axis=1)
                for p in range(pairs)]
        zs = [lax.dot_general(kblks[idx // 2], q_heads[idx], NT_DIMS, preferred_element_type=F32)
              for idx in heads]
        softplus, his, los = [], [], []
        for z in zs:
            sp = jnp.maximum(z, 0.0) + jnp.log2(1.0 + jnp.exp2(-jnp.abs(z)))
            if masked:
                sp = jnp.where(causal, sp, 0.0)
            hi = sp.astype(BF16)
            softplus.append(sp)
            his.append(hi)
            los.append((sp - hi.astype(F32)).astype(BF16))
        since = [jnp.dot(from_here, his[idx], preferred_element_type=F32)
                 + jnp.dot(from_here, los[idx], preferred_element_type=F32) for idx in heads]
        ws = []
        for idx in heads:
            w = jnp.exp2(zs[idx] - since[idx] - st[2 * idx])
            if masked:
                w = jnp.where(causal, w, 0.0)
            ws.append(w.astype(BF16))
        new = []
        for idx in heads:
            h = idx % 2
            acc = st[2 * idx + 1] + jnp.dot(v_ts[idx // 2][h * HEAD_DIM:(h + 1) * HEAD_DIM, :],
                                            ws[idx], preferred_element_type=F32)
            carry = st[2 * idx] + jnp.sum(softplus[idx], axis=0, keepdims=True)
            new += [carry, acc]
        return tuple(new)

    z_bound = jnp.sqrt(max_sq_norm(q_ref[...]) * kmax_ref[...]) * (HEAD_DIM ** -0.5 * LOG2E * NORM_SLACK)
    z_bounds = [jnp.broadcast_to(z_bound[:, idx:idx + 1], (1, tq)) for idx in range(2 * pairs)]

    def all_underflow(st):
        slack = st[0] - z_bounds[0]
        for idx in range(1, 2 * pairs):
            slack = jnp.minimum(slack, st[2 * idx] - z_bounds[idx])
        return jnp.min(slack) > UNDERFLOW_MARGIN

    def walk(state):
        n, _, st = state
        st = block(qi - 1 - n, st, False)
        return n + 1, all_underflow(st), st

    st = (jnp.zeros((1, tq), F32), jnp.zeros((HEAD_DIM, tq), F32)) * (2 * pairs)
    st = block(qi, st, True)
    _, _, st = lax.while_loop(lambda state: (state[0] < qi) & jnp.logical_not(state[1]), walk,
                              (jnp.int32(0), all_underflow(st), st))
    for p in range(pairs):
        out_t = jnp.concatenate([st[4 * p + 1], st[4 * p + 3]], axis=0)
        for u in range(n_sub):
            o_ref[u * LANES:(u + 1) * LANES, p * LANES:(p + 1) * LANES] = (
                out_t[:, u * LANES:(u + 1) * LANES].T.astype(o_ref.dtype))


def _stickbreak(proj, *, tq=256, pairs=4):
    b, s, _ = proj.shape
    width = pairs * LANES
    n_steps = SB_WIDTH // width
    return pl.pallas_call(
        functools.partial(_sb_kernel, tq=tq, pairs=pairs),
        out_shape=jax.ShapeDtypeStruct((b, s, SB_WIDTH), BF16),
        grid=(b, n_steps, s // tq),
        in_specs=[
            pl.BlockSpec((None, tq, width), lambda bi, hp, qi: (bi, qi, hp)),
            pl.BlockSpec((None, s, width), lambda bi, hp, qi: (bi, 0, n_steps + hp)),
            pl.BlockSpec((None, s, width), lambda bi, hp, qi: (bi, 0, 2 * n_steps + hp)),
        ],
        out_specs=pl.BlockSpec((None, tq, width), lambda bi, hp, qi: (bi, qi, hp)),
        scratch_shapes=[pltpu.VMEM((pairs, s // LANES, LANES, LANES), BF16),
                        pltpu.VMEM((1, LANES), F32)],
        compiler_params=pltpu.CompilerParams(
            dimension_semantics=("parallel", "parallel", "arbitrary"),
            vmem_limit_bytes=VMEM_LIMIT),
        name="stickbreak",
    )(proj, proj, proj)


def _ffn(x, g_ref, wg_ref, wu_ref, wd_ref, act_ref, th):
    h = _rms(x, g_ref[...]).astype(BF16)
    for start in range(0, FFN_HIDDEN, th):
        hg = jnp.dot(h, wg_ref[:, start:start + th], preferred_element_type=F32)
        hu = jnp.dot(h, wu_ref[:, start:start + th], preferred_element_type=F32)
        sig = 1.0 / (1.0 + jnp.exp(-hg))
        act_ref[:, start:start + th] = (hg * sig * hu).astype(BF16)
    return x + jnp.dot(act_ref[...], wd_ref[...], preferred_element_type=F32)


def _ffn_specs(d, index_map):
    resident = pl.Buffered(1)
    return [
        pl.BlockSpec((1, d), index_map),
        pl.BlockSpec((d, FFN_HIDDEN), index_map, pipeline_mode=resident),
        pl.BlockSpec((d, FFN_HIDDEN), index_map, pipeline_mode=resident),
        pl.BlockSpec((FFN_HIDDEN, d), index_map, pipeline_mode=resident),
    ]


def _mix0_ffn_kernel(a_ref, bg_ref, cg_ref, u_ref, cgh_ref, uh_ref, cw_ref, w_ref, x_ref,
                     g_ref, wg_ref, wu_ref, wd_ref, o_ref, act_ref, *, th):
    i = pl.program_id(1)
    g = cg_ref[...] * u_ref[...]
    gh = cgh_ref[...] * uh_ref[...]
    gh = jnp.where(i == 0, 0.0, gh)
    row = lax.broadcasted_iota(I32, g.shape, 0)
    g1 = jnp.where(row == 0, gh[7:8, :], pltpu.roll(g, 1, axis=0))
    g2 = pltpu.roll(g, 2, axis=0)
    g2 = jnp.where(row == 0, gh[6:7, :], jnp.where(row == 1, gh[7:8, :], g2))
    cw = cw_ref[...]
    y = bg_ref[...] * (cw[0:1, :] * g2 + cw[1:2, :] * g1 + cw[2:3, :] * g)
    acc = jnp.dot(a_ref[...], w_ref[0:SB_WIDTH, :], preferred_element_type=F32)
    acc = acc + jnp.dot(y.astype(BF16), w_ref[SB_WIDTH:, :], preferred_element_type=F32)
    o_ref[...] = _ffn(x_ref[...] + acc, g_ref, wg_ref, wu_ref, wd_ref, act_ref, th)


def _mix0_ffn(a_out, proj, conv_w, w_out, x, g, wg, wu, wd, *, tm=512, th=256):
    b, s, d = x.shape
    halo = lambda col: (lambda bi, i: (bi, jnp.maximum(i * (tm // 8) - 1, 0), col))
    return pl.pallas_call(
        functools.partial(_mix0_ffn_kernel, th=th),
        out_shape=jax.ShapeDtypeStruct((b, s, d), F32),
        grid=(b, s // tm),
        in_specs=[
            pl.BlockSpec((None, tm, SB_WIDTH), lambda bi, i: (bi, i, 0)),
            pl.BlockSpec((None, tm, CONV_CH), lambda bi, i: (bi, i, 3)),
            pl.BlockSpec((None, tm, CONV_CH), lambda bi, i: (bi, i, 4)),
            pl.BlockSpec((None, tm, CONV_CH), lambda bi, i: (bi, i, 5)),
            pl.BlockSpec((None, 8, CONV_CH), halo(4)),
            pl.BlockSpec((None, 8, CONV_CH), halo(5)),
            pl.BlockSpec((3, CONV_CH), lambda bi, i: (0, 0)),
            pl.BlockSpec((SB_WIDTH + CONV_CH, d), lambda bi, i: (0, 0)),
            pl.BlockSpec((None, tm, d), lambda bi, i: (bi, i, 0)),
        ] + _ffn_specs(d, lambda bi, i: (0, 0)),
        out_specs=pl.BlockSpec((None, tm, d), lambda bi, i: (bi, i, 0)),
        scratch_shapes=[pltpu.VMEM((tm, FFN_HIDDEN), BF16)],
        compiler_params=pltpu.CompilerParams(
            dimension_semantics=("parallel", "arbitrary"), vmem_limit_bytes=VMEM_LIMIT),
        name="mix0_ffn",
    )(a_out, proj, proj, proj, proj, proj, conv_w, w_out, x, g, wg, wu, wd)


def _proj_ffn_kernel(a_ref, w_ref, x_ref, g_ref, wg_ref, wu_ref, wd_ref, o_ref, act_ref, *, th):
    x1 = x_ref[...] + jnp.dot(a_ref[...], w_ref[...], preferred_element_type=F32)
    o_ref[...] = _ffn(x1, g_ref, wg_ref, wu_ref, wd_ref, act_ref, th)


def _proj_ffn(a, w, x, g, wg, wu, wd, *, tm=512, th=256):
    m, k = a.shape
    d = w.shape[1]
    return pl.pallas_call(
        functools.partial(_proj_ffn_kernel, th=th),
        out_shape=jax.ShapeDtypeStruct((m, d), F32),
        grid=(m // tm,),
        in_specs=[
            pl.BlockSpec((tm, k), lambda i: (i, 0)),
            pl.BlockSpec((k, d), lambda i: (0, 0)),
            pl.BlockSpec((tm, d), lambda i: (i, 0)),
        ] + _ffn_specs(d, lambda i: (0, 0)),
        out_specs=pl.BlockSpec((tm, d), lambda i: (i, 0)),
        scratch_shapes=[pltpu.VMEM((tm, FFN_HIDDEN), BF16)],
        compiler_params=pltpu.CompilerParams(
            dimension_semantics=("parallel",), vmem_limit_bytes=VMEM_LIMIT),
        name="proj_ffn",
    )(a, w, x, g, wg, wu, wd)


BIAS_ROWS = 3 * LANES


def _bias_tile_kernel(rb_ref, o_ref):
    h = pl.program_id(0)
    c = lax.broadcasted_iota(I32, (BIAS_ROWS, LANES), 0)
    t = lax.broadcasted_iota(I32, (BIAS_ROWS, LANES), 1)
    dist = jnp.maximum(LANES + t - c, 0)
    exact = REL_BUCKETS // 2
    d_f = jnp.maximum(dist, 1).astype(F32)
    large = exact + (jnp.log(d_f / exact) / math.log(REL_MAX_DIST / exact)
                     * (REL_BUCKETS - exact)).astype(I32)
    large = jnp.minimum(large, REL_BUCKETS - 1)
    bucket = jnp.where(dist < exact, dist, large)
    out = jnp.zeros((BIAS_ROWS, LANES), F32)
    for b in range(REL_BUCKETS):
        out = jnp.where(bucket == b, rb_ref[b, h], out)
    o_ref[...] = (out - rb_ref[REL_BUCKETS - 1, h]) * LOG2E


def _bias_tiles(rel_bias):
    return pl.pallas_call(
        _bias_tile_kernel,
        out_shape=jax.ShapeDtypeStruct((ATT_HEADS, BIAS_ROWS, LANES), F32),
        grid=(ATT_HEADS,),
        in_specs=[pl.BlockSpec(memory_space=pltpu.SMEM)],
        out_specs=pl.BlockSpec((None, BIAS_ROWS, LANES), lambda h: (h, 0, 0)),
        name="rel_bias_tiles",
    )(rel_bias)


def _dsa_kernel(q_ref, k_ref, v_ref, qi_ref, wq_ref, kik_ref, qg_ref, kg_ref, bt_ref, seg_ref,
                segt_ref, o_ref, kn_ref, vt_ref, kib_ref, key_ref, mfar_ref, s_ref, *, tq, chunk):
    i = pl.program_id(1)
    s_len = k_ref.shape[0]
    n_chunk = (i + chunk) // chunk
    n_far = (i - 1 + chunk - 1) // chunk
    w0 = jnp.maximum(i - 1, 0)

    @pl.when(i == 0)
    def _prepare_keys():
        kg = kg_ref[...]
        for g in range(KV_HEADS):
            kn = _rms(k_ref[:, g * HEAD_DIM:(g + 1) * HEAD_DIM], kg).astype(BF16)
            kn_ref[g] = jnp.concatenate([kn, kn], axis=1)
        for j in range(s_len // LANES):
            for pair in range(KV_HEADS // 2):
                v_t = v_ref[j * LANES:(j + 1) * LANES, pair * LANES:(pair + 1) * LANES].T
                vt_ref[2 * pair, j, 0:HEAD_DIM] = v_t[0:HEAD_DIM].astype(BF16)
                vt_ref[2 * pair + 1, j, 0:HEAD_DIM] = v_t[HEAD_DIM:].astype(BF16)
        vt_ref[:, :, HEAD_DIM:V_ROWS] = jnp.ones((KV_HEADS, s_len // LANES, V_ROWS - HEAD_DIM, LANES),
                                                 BF16)
        ki = kik_ref[:, 0:HEAD_DIM].astype(BF16)
        kib_ref[...] = jnp.concatenate([ki, ki], axis=1)

    ck = chunk * LANES
    krow = lax.broadcasted_iota(I32, (ck, tq), 0)
    tpos = i * tq + lax.broadcasted_iota(I32, (ck, tq), 1)
    low_half = lax.broadcasted_iota(I32, (tq, LANES), 1) < HEAD_DIM

    def split_pair(x):
        return [jnp.where(low_half, x, 0.0).astype(BF16), jnp.where(low_half, 0.0, x).astype(BF16)]

    w_t = wq_ref[...].T
    qi_pairs = [jnp.concatenate(split_pair(qi_ref[:, pair * LANES:(pair + 1) * LANES] * 0.125), axis=0)
                for pair in range(IDX_HEADS // 2)]
    w_heads = [w_t[HEAD_DIM + h:HEAD_DIM + h + 1, :] * (IDX_HEADS ** -0.5)
               for h in range(IDX_HEADS)]

    def score_chunk(c, _):
        kt = kib_ref[pl.ds(pl.multiple_of(c * ck, ck), ck), :]
        dots = [lax.dot_general(kt, qi_pair, NT_DIMS, preferred_element_type=F32)
                for qi_pair in qi_pairs]
        sc = jnp.zeros((ck, tq), F32)
        for h in range(IDX_HEADS):
            d = dots[h // 2][:, (h % 2) * tq:(h % 2 + 1) * tq]
            sc = sc + w_heads[h] * jnp.maximum(d, 0.0)
        sc = jnp.where(sc == 0.0, 0.0, sc)
        sc = jnp.where(c * ck + krow <= tpos, sc, -jnp.inf)
        bits = pltpu.bitcast(sc, I32)
        key_ref[c] = bits ^ ((bits >> 31) & 0x7FFFFFFF)
        return 0

    lax.fori_loop(0, n_chunk, score_chunk, 0)

    def count_keys(preds):
        def body(c, cs):
            key = key_ref[c]
            return tuple(
                cnt + jnp.sum(pred(c, key).astype(I32).reshape(ck // 8, 8, tq), axis=0)
                for cnt, pred in zip(cs, preds))
        cs = lax.fori_loop(0, n_chunk, body,
                           tuple(jnp.zeros((8, tq), I32) for _ in preds))
        return [jnp.sum(cnt, axis=0, keepdims=True) for cnt in cs]

    def thr_bit(b, thr):
        cand = thr + lax.shift_left(jnp.int32(1), 31 - b)
        cand_b = jnp.broadcast_to(cand, (ck, tq))
        (cnt,) = count_keys([lambda c, key: key >= cand_b])
        return jnp.where(cnt >= TOPK, cand, thr)

    thr = lax.cond((i + 1) * tq <= TOPK,
                   lambda: jnp.full((1, tq), KEY_NEG_INF, I32),
                   lambda: lax.fori_loop(0, 32, thr_bit, jnp.full((1, tq), INT_MIN, I32)))
    thr_b = jnp.broadcast_to(thr, (ck, tq))
    n_gt, n_eq = count_keys([lambda c, key: key > thr_b, lambda c, key: key == thr_b])
    need = TOPK - n_gt
    tied = (n_eq > need) & (thr != KEY_NEG_INF)
    any_tied = jnp.max(tied.astype(I32)) > 0

    def tie_bit(b, cut):
        cand = cut + lax.shift_left(jnp.int32(1), 10 - b)
        cand_b = jnp.broadcast_to(cand, (ck, tq))
        (cnt,) = count_keys([lambda c, key: (key == thr_b) & (c * ck + krow < cand_b)])
        return jnp.where(cnt < need, cand, cut)

    cut = lax.cond(any_tied,
                   lambda: lax.fori_loop(0, 11, tie_bit, jnp.zeros((1, tq), I32)),
                   lambda: jnp.full((1, tq), s_len, I32))

    def selection_mask(key, spos, tpos):
        thr_k = jnp.broadcast_to(thr, key.shape)
        cut_k = jnp.broadcast_to(cut, key.shape)
        sel = (key > thr_k) | ((key == thr_k) & (spos <= cut_k))
        return jnp.where(sel & (spos <= tpos), 0.0, NEG)

    def mask_chunk(c, _):
        spos = c * ck + krow
        madd = selection_mask(key_ref[c], spos, tpos)
        mfar_ref[c] = jnp.where(spos < (i - 1) * LANES, madd, NEG)
        return 0

    lax.fori_loop(0, n_far, mask_chunk, 0)

    def tile_mask(tile):
        c, u = tile // chunk, tile % chunk
        key = key_ref[c, pl.ds(pl.multiple_of(u * LANES, LANES), LANES), :]
        krow_t = lax.broadcasted_iota(I32, (LANES, tq), 0)
        tpos_t = i * tq + lax.broadcasted_iota(I32, (LANES, tq), 1)
        return selection_mask(key, tile * LANES + krow_t, tpos_t)

    q = q_ref[...]
    sq = q * q
    sq_hi = sq.astype(BF16)
    sq_lo = (sq - sq_hi.astype(F32)).astype(BF16)
    seg = seg_ref[...]
    ssum = (jnp.dot(sq_hi, seg, preferred_element_type=F32)
            + jnp.dot(sq_lo, seg, preferred_element_type=F32))
    rinv = lax.rsqrt(ssum * (1.0 / HEAD_DIM) + EPS)
    pieces, rest = [], rinv
    for _ in range(3):
        piece = rest.astype(BF16)
        pieces.append(piece)
        rest = rest - piece.astype(F32)
    segt = segt_ref[...]
    rinv_b = sum(jnp.dot(piece, segt, preferred_element_type=F32) for piece in pieces)
    qn = q * rinv_b * (qg_ref[...] * (HEAD_DIM ** -0.5 * LOG2E))
    qs = []
    for g in range(KV_HEADS):
        rows = []
        for pair in range(2 * g, 2 * g + 2):
            rows += split_pair(qn[:, pair * LANES:(pair + 1) * LANES])
        qs.append(jnp.concatenate(rows, axis=0))

    groups = range(KV_HEADS)

    def attend(sts, n_keys, first_tile, add_fn):
        key_start = pl.multiple_of(first_tile * LANES, LANES)
        for g in groups:
            kt = kn_ref[g, pl.ds(key_start, n_keys), :]
            add = jnp.concatenate([add_fn(g * GQA_REP + r) for r in range(GQA_REP)], axis=1)
            s_ref[g, 0:n_keys, :] = lax.dot_general(kt, qs[g], NT_DIMS,
                                                    preferred_element_type=F32) + add
        ps, stats = [], []
        for g in groups:
            m, _ = sts[g]
            m_new = jnp.maximum(m, jnp.max(s_ref[g, 0:n_keys, :], axis=0, keepdims=True))
            stats.append((m_new, jnp.exp2(m - m_new)))
            ps.append(jnp.exp2(s_ref[g, 0:n_keys, :] - m_new).astype(BF16))
        new = []
        for g in groups:
            v_t = jnp.concatenate([vt_ref[g, first_tile + u] for u in range(n_keys // LANES)],
                                  axis=1)
            new.append((stats[g][0], stats[g][1] * sts[g][1]
                        + jnp.dot(v_t, ps[g], preferred_element_type=F32)))
        return tuple(new)

    def far_chunk(c, sts):
        madd = mfar_ref[c]
        return attend(sts, ck, c * chunk, lambda head: madd)

    init = (jnp.full((1, GQA_REP * tq), NEG, F32), jnp.zeros((V_ROWS, GQA_REP * tq), F32))
    sts = lax.fori_loop(0, n_far, far_chunk, (init,) * KV_HEADS)

    win_mask = jnp.concatenate([tile_mask(w0), tile_mask(w0 + 1)], axis=0)
    bias_row = pl.multiple_of(jnp.where(i == 0, LANES, 0), LANES)
    sts = attend(sts, 2 * LANES, w0,
                 lambda head: bt_ref[head, pl.ds(bias_row, 2 * LANES), :] + win_mask)
    for g in range(KV_HEADS):
        acc = sts[g][1]
        out_t = acc[0:HEAD_DIM] / acc[HEAD_DIM:HEAD_DIM + 1]
        for pair in range(GQA_REP // 2):
            two = jnp.concatenate(
                [out_t[:, (2 * pair) * tq:(2 * pair + 1) * tq],
                 out_t[:, (2 * pair + 1) * tq:(2 * pair + 2) * tq]], axis=0)
            col = (g * GQA_REP + 2 * pair) * HEAD_DIM
            o_ref[:, col:col + LANES] = two.T.astype(o_ref.dtype)


def _dsa(proj, q_gain, k_gain, bias_tiles, *, tq=128, chunk=4):
    b, s, _ = proj.shape
    n_t = s // LANES
    att_width = ATT_HEADS * HEAD_DIM
    seg = (jnp.arange(att_width)[:, None] // HEAD_DIM == jnp.arange(LANES)[None, :]).astype(BF16)
    q_gain_all = jnp.tile(q_gain, (1, ATT_HEADS))
    return pl.pallas_call(
        functools.partial(_dsa_kernel, tq=tq, chunk=chunk),
        out_shape=jax.ShapeDtypeStruct((b, s, att_width), BF16),
        grid=(b, s // tq),
        in_specs=[
            pl.BlockSpec((None, tq, 1024), lambda bi, i: (bi, i, 0)),
            pl.BlockSpec((None, s, 256), lambda bi, i: (bi, 0, 4)),
            pl.BlockSpec((None, s, 256), lambda bi, i: (bi, 0, 5)),
            pl.BlockSpec((None, tq, 512), lambda bi, i: (bi, i, 3)),
            pl.BlockSpec((None, tq, LANES), lambda bi, i: (bi, i, 16)),
            pl.BlockSpec((None, s, LANES), lambda bi, i: (bi, 0, 16)),
            pl.BlockSpec((1, att_width), lambda bi, i: (0, 0)),
            pl.BlockSpec((1, HEAD_DIM), lambda bi, i: (0, 0)),
            pl.BlockSpec((ATT_HEADS, BIAS_ROWS, LANES), lambda bi, i: (0, 0, 0)),
            pl.BlockSpec((att_width, LANES), lambda bi, i: (0, 0)),
            pl.BlockSpec((LANES, att_width), lambda bi, i: (0, 0)),
        ],
        out_specs=pl.BlockSpec((None, tq, att_width), lambda bi, i: (bi, i, 0)),
        scratch_shapes=[
            pltpu.VMEM((KV_HEADS, s, LANES), BF16),
            pltpu.VMEM((KV_HEADS, n_t, V_ROWS, LANES), BF16),
            pltpu.VMEM((s, LANES), BF16),
            pltpu.VMEM((n_t // chunk, chunk * LANES, tq), I32),
            pltpu.VMEM((n_t // chunk, chunk * LANES, tq), F32),
            pltpu.VMEM((KV_HEADS, chunk * LANES, GQA_REP * tq), F32),
        ],
        compiler_params=pltpu.CompilerParams(
            dimension_semantics=("parallel", "arbitrary"), vmem_limit_bytes=VMEM_LIMIT),
        name="dsa_attention",
    )(proj, proj, proj, proj, proj, proj, q_gain_all, k_gain, bias_tiles, seg, seg.T)


def kernel(x, norm_mix, norm_ffn, ev_w_in, ev_conv_w, ev_w_out, od_w_in, od_q_gain,
           od_k_gain, od_w_out, rel_bias, ffn_w_gate, ffn_w_up, ffn_w_down):
    b, s, d = x.shape
    n_tok = b * s
    bf = lambda w: w.astype(BF16)

    proj0 = _norm_matmul(x.reshape(n_tok, d), norm_mix[0:1], bf(ev_w_in[0]))
    proj0 = proj0.reshape(b, s, -1)
    a_out = _stickbreak(proj0)
    x2 = _mix0_ffn(a_out, proj0, ev_conv_w[0], bf(ev_w_out[0]), x, norm_ffn[0:1],
                   bf(ffn_w_gate[0]), bf(ffn_w_up[0]), bf(ffn_w_down[0])).reshape(n_tok, d)

    w_in1 = jnp.pad(od_w_in[0], ((0, 0), (0, ODD_IN_PAD - od_w_in.shape[2])))
    proj1 = _norm_matmul(x2, norm_mix[1:2], bf(w_in1)).reshape(b, s, ODD_IN_PAD)
    att = _dsa(proj1, od_q_gain[0:1], od_k_gain[0:1], _bias_tiles(rel_bias))
    x4 = _proj_ffn(att.reshape(n_tok, -1), bf(od_w_out[0]), x2, norm_ffn[1:2],
                   bf(ffn_w_gate[1]), bf(ffn_w_up[1]), bf(ffn_w_down[1]))
    return x4.reshape(b, s, d)
```

```python
import functools
import math

import jax
import jax.numpy as jnp
from jax import lax
from jax.experimental import pallas as pl
from jax.experimental.pallas import tpu as pltpu

F32 = jnp.float32
BF16 = jnp.bfloat16
I32 = jnp.int32

D_MODEL = 1024
SEQ = 2048
HEAD_DIM = 64
SB_WIDTH = 512
CONV_CH = 512
ATT_HEADS = 16
KV_HEADS = 4
GQA_REP = 4
IDX_HEADS = 8
TOPK = 256
REL_BUCKETS = 32
REL_MAX_DIST = 128
FFN_HIDDEN = 2816
EPS = 1e-6
ODD_IN_PAD = 2176

LOG2E = 1.4426950408889634
LANES = 128
VMEM_LIMIT = 56 * 1024 * 1024
NEG = -1e30
INT_MIN = -(2 ** 31)
GROUP_KEYS = 32 * 8
V_ROWS = HEAD_DIM + 16
UNDERFLOW_MARGIN = 160.0
NORM_SLACK = 1.03
KEY_NEG_INF = -(2 ** 31) + 0x7FFFFF

NT_DIMS = (((1,), (1,)), ((), ()))


def _rms(x, g):
    ms = jnp.mean(x * x, axis=-1, keepdims=True)
    return x * lax.rsqrt(ms + EPS) * g


def _norm_matmul_kernel(x_ref, g_ref, w_ref, o_ref, *, tn):
    h = _rms(x_ref[...], g_ref[...]).astype(BF16)
    n = w_ref.shape[1]
    for start in range(0, n, tn):
        size = min(tn, n - start)
        o_ref[:, start:start + size] = jnp.dot(
            h, w_ref[:, start:start + size], preferred_element_type=F32)


def _norm_matmul(x, g, w, *, tm=512, tn=512):
    m, d = x.shape
    n = w.shape[1]
    return pl.pallas_call(
        functools.partial(_norm_matmul_kernel, tn=tn),
        out_shape=jax.ShapeDtypeStruct((m, n), F32),
        grid=(m // tm,),
        in_specs=[
            pl.BlockSpec((tm, d), lambda i: (i, 0)),
            pl.BlockSpec((1, d), lambda i: (0, 0)),
            pl.BlockSpec((d, n), lambda i: (0, 0)),
        ],
        out_specs=pl.BlockSpec((tm, n), lambda i: (i, 0)),
        compiler_params=pltpu.CompilerParams(
            dimension_semantics=("parallel",), vmem_limit_bytes=VMEM_LIMIT),
        name="norm_matmul",
    )(x, g, w)


def _sb_kernel(q_ref, k_ref, v_ref, o_ref, vt_ref, kmax_ref, *, tq, pairs):
    qi = pl.program_id(2)
    n_sub = tq // LANES
    s_len = k_ref.shape[0]
    width = pairs * LANES
    head_of = (lax.broadcasted_iota(I32, (width, LANES), 0) // HEAD_DIM
               == lax.broadcasted_iota(I32, (width, LANES), 1)).astype(BF16)

    def max_sq_norm(x):
        sq = (x * x).astype(BF16)
        return jnp.max(jnp.dot(sq, head_of, preferred_element_type=F32), axis=0, keepdims=True)

    @pl.when(qi == 0)
    def _per_sequence_setup():
        for p in range(pairs):
            for j in range(s_len // LANES):
                vt_ref[p, j] = v_ref[j * LANES:(j + 1) * LANES,
                                     p * LANES:(p + 1) * LANES].T.astype(BF16)
        kmax = jnp.zeros((1, LANES), F32)
        for r in range(0, s_len, tq):
            kmax = jnp.maximum(kmax, max_sq_norm(k_ref[r:r + tq, :]))
        kmax_ref[...] = kmax

    lane = lax.broadcasted_iota(I32, (tq, LANES), 1)
    krow = lax.broadcasted_iota(I32, (tq, tq), 0)
    qcol = lax.broadcasted_iota(I32, (tq, tq), 1)
    from_here = (qcol >= krow).astype(BF16)
    causal = krow < qcol
    q_heads = []
    for p in range(pairs):
        q = q_ref[:, p * LANES:(p + 1) * LANES] * (HEAD_DIM ** -0.5 * LOG2E)
        for h in range(2):
            in_head = (lane >= HEAD_DIM * h) & (lane < HEAD_DIM * (h + 1))
            q_heads.append(jnp.where(in_head, q, 0.0).astype(BF16))

    def block(kb, st, masked):
        start = pl.multiple_of(kb * tq, tq)
        heads = range(2 * pairs)
        kblks = [k_ref[pl.ds(start, tq), p * LANES:(p + 1) * LANES].astype(BF16)
                 for p in range(pairs)]
        v_ts = [jnp.concatenate([vt_ref[p, kb * n_sub + u] for u in range(n_sub)], axis=1)
                for p in range(pairs)]
        zs = [lax.dot_general(kblks[idx // 2], q_heads[idx], NT_DIMS, preferred_element_type=F32)
              for idx in heads]
        softplus, his, los = [], [], []
        for z in zs:
            sp = jnp.maximum(z, 0.0) + jnp.log2(1.0 + jnp.exp2(-jnp.abs(z)))
            if masked:
                sp = jnp.where(causal, sp, 0.0)
            hi = sp.astype(BF16)
            softplus.append(sp)
            his.append(hi)
            los.append((sp - hi.astype(F32)).astype(BF16))
        since = [jnp.dot(from_here, his[idx], preferred_element_type=F32)
                 + jnp.dot(from_here, los[idx], preferred_element_type=F32) for idx in heads]
        ws = []
        for idx in heads:
            w = jnp.exp2(zs[idx] - since[idx] - st[2 * idx])
            if masked:
                w = jnp.where(causal, w, 0.0)
            ws.append(w.astype(BF16))
        new = []
        for idx in heads:
            h = idx % 2
            acc = st[2 * idx + 1] + jnp.dot(v_ts[idx // 2][h * HEAD_DIM:(h + 1) * HEAD_DIM, :],
                                            ws[idx], preferred_element_type=F32)
            carry = st[2 * idx] + jnp.sum(softplus[idx], axis=0, keepdims=True)
            new += [carry, acc]
        return tuple(new)

    z_bound = jnp.sqrt(max_sq_norm(q_ref[...]) * kmax_ref[...]) * (HEAD_DIM ** -0.5 * LOG2E * NORM_SLACK)
    z_bounds = [jnp.broadcast_to(z_bound[:, idx:idx + 1], (1, tq)) for idx in range(2 * pairs)]

    def all_underflow(st):
        slack = st[0] - z_bounds[0]
        for idx in range(1, 2 * pairs):
            slack = jnp.minimum(slack, st[2 * idx] - z_bounds[idx])
        return jnp.min(slack) > UNDERFLOW_MARGIN

    def walk(state):
        n, _, st = state
        st = block(qi - 1 - n, st, False)
        return n + 1, all_underflow(st), st

    st = (jnp.zeros((1, tq), F32), jnp.zeros((HEAD_DIM, tq), F32)) * (2 * pairs)
    st = block(qi, st, True)
    _, _, st = lax.while_loop(lambda state: (state[0] < qi) & jnp.logical_not(state[1]), walk,
                              (jnp.int32(0), all_underflow(st), st))
    for p in range(pairs):
        out_t = jnp.concatenate([st[4 * p + 1], st[4 * p + 3]], axis=0)
        for u in range(n_sub):
            o_ref[u * LANES:(u + 1) * LANES, p * LANES:(p + 1) * LANES] = (
                out_t[:, u * LANES:(u + 1) * LANES].T.astype(o_ref.dtype))


def _stickbreak(proj, *, tq=256, pairs=4):
    b, s, _ = proj.shape
    width = pairs * LANES
    n_steps = SB_WIDTH // width
    return pl.pallas_call(
        functools.partial(_sb_kernel, tq=tq, pairs=pairs),
        out_shape=jax.ShapeDtypeStruct((b, s, SB_WIDTH), BF16),
        grid=(b, n_steps, s // tq),
        in_specs=[
            pl.BlockSpec((None, tq, width), lambda bi, hp, qi: (bi, qi, hp)),
            pl.BlockSpec((None, s, width), lambda bi, hp, qi: (bi, 0, n_steps + hp)),
            pl.BlockSpec((None, s, width), lambda bi, hp, qi: (bi, 0, 2 * n_steps + hp)),
        ],
        out_specs=pl.BlockSpec((None, tq, width), lambda bi, hp, qi: (bi, qi, hp)),
        scratch_shapes=[pltpu.VMEM((pairs, s // LANES, LANES, LANES), BF16),
                        pltpu.VMEM((1, LANES), F32)],
        compiler_params=pltpu.CompilerParams(
            dimension_semantics=("parallel", "parallel", "arbitrary"),
            vmem_limit_bytes=VMEM_LIMIT),
        name="stickbreak",
    )(proj, proj, proj)


def _ffn(x, g_ref, wg_ref, wu_ref, wd_ref, act_ref, th):
    h = _rms(x, g_ref[...]).astype(BF16)
    for start in range(0, FFN_HIDDEN, th):
        hg = jnp.dot(h, wg_ref[:, start:start + th], preferred_element_type=F32)
        hu = jnp.dot(h, wu_ref[:, start:start + th], preferred_element_type=F32)
        sig = 1.0 / (1.0 + jnp.exp(-hg))
        act_ref[:, start:start + th] = (hg * sig * hu).astype(BF16)
    return x + jnp.dot(act_ref[...], wd_ref[...], preferred_element_type=F32)


def _ffn_specs(d, index_map):
    resident = pl.Buffered(1)
    return [
        pl.BlockSpec((1, d), index_map),
        pl.BlockSpec((d, FFN_HIDDEN), index_map, pipeline_mode=resident),
        pl.BlockSpec((d, FFN_HIDDEN), index_map, pipeline_mode=resident),
        pl.BlockSpec((FFN_HIDDEN, d), index_map, pipeline_mode=resident),
    ]


def _mix0_ffn_kernel(a_ref, bg_ref, cg_ref, u_ref, cgh_ref, uh_ref, cw_ref, w_ref, x_ref,
                     g_ref, wg_ref, wu_ref, wd_ref, o_ref, act_ref, *, th):
    i = pl.program_id(1)
    g = cg_ref[...] * u_ref[...]
    gh = cgh_ref[...] * uh_ref[...]
    gh = jnp.where(i == 0, 0.0, gh)
    row = lax.broadcasted_iota(I32, g.shape, 0)
    g1 = jnp.where(row == 0, gh[7:8, :], pltpu.roll(g, 1, axis=0))
    g2 = pltpu.roll(g, 2, axis=0)
    g2 = jnp.where(row == 0, gh[6:7, :], jnp.where(row == 1, gh[7:8, :], g2))
    cw = cw_ref[...]
    y = bg_ref[...] * (cw[0:1, :] * g2 + cw[1:2, :] * g1 + cw[2:3, :] * g)
    acc = jnp.dot(a_ref[...], w_ref[0:SB_WIDTH, :], preferred_element_type=F32)
    acc = acc + jnp.dot(y.astype(BF16), w_ref[SB_WIDTH:, :], preferred_element_type=F32)
    o_ref[...] = _ffn(x_ref[...] + acc, g_ref, wg_ref, wu_ref, wd_ref, act_ref, th)


def _mix0_ffn(a_out, proj, conv_w, w_out, x, g, wg, wu, wd, *, tm=512, th=256):
    b, s, d = x.shape
    halo = lambda col: (lambda bi, i: (bi, jnp.maximum(i * (tm // 8) - 1, 0), col))
    return pl.pallas_call(
        functools.partial(_mix0_ffn_kernel, th=th),
        out_shape=jax.ShapeDtypeStruct((b, s, d), F32),
        grid=(b, s // tm),
        in_specs=[
            pl.BlockSpec((None, tm, SB_WIDTH), lambda bi, i: (bi, i, 0)),
            pl.BlockSpec((None, tm, CONV_CH), lambda bi, i: (bi, i, 3)),
            pl.BlockSpec((None, tm, CONV_CH), lambda bi, i: (bi, i, 4)),
            pl.BlockSpec((None, tm, CONV_CH), lambda bi, i: (bi, i, 5)),
            pl.BlockSpec((None, 8, CONV_CH), halo(4)),
            pl.BlockSpec((None, 8, CONV_CH), halo(5)),
            pl.BlockSpec((3, CONV_CH), lambda bi, i: (0, 0)),
            pl.BlockSpec((SB_WIDTH + CONV_CH, d), lambda bi, i: (0, 0)),
            pl.BlockSpec((None, tm, d), lambda bi, i: (bi, i, 0)),
        ] + _ffn_specs(d, lambda bi, i: (0, 0)),
        out_specs=pl.BlockSpec((None, tm, d), lambda bi, i: (bi, i, 0)),
        scratch_shapes=[pltpu.VMEM((tm, FFN_HIDDEN), BF16)],
        compiler_params=pltpu.CompilerParams(
            dimension_semantics=("parallel", "arbitrary"), vmem_limit_bytes=VMEM_LIMIT),
        name="mix0_ffn",
    )(a_out, proj, proj, proj, proj, proj, conv_w, w_out, x, g, wg, wu, wd)


def _proj_ffn_kernel(a_ref, w_ref, x_ref, g_ref, wg_ref, wu_ref, wd_ref, o_ref, act_ref, *, th):
    x1 = x_ref[...] + jnp.dot(a_ref[...], w_ref[...], preferred_element_type=F32)
    o_ref[...] = _ffn(x1, g_ref, wg_ref, wu_ref, wd_ref, act_ref, th)


def _proj_ffn(a, w, x, g, wg, wu, wd, *, tm=512, th=256):
    m, k = a.shape
    d = w.shape[1]
    return pl.pallas_call(
        functools.partial(_proj_ffn_kernel, th=th),
        out_shape=jax.ShapeDtypeStruct((m, d), F32),
        grid=(m // tm,),
        in_specs=[
            pl.BlockSpec((tm, k), lambda i: (i, 0)),
            pl.BlockSpec((k, d), lambda i: (0, 0)),
            pl.BlockSpec((tm, d), lambda i: (i, 0)),
        ] + _ffn_specs(d, lambda i: (0, 0)),
        out_specs=pl.BlockSpec((tm, d), lambda i: (i, 0)),
        scratch_shapes=[pltpu.VMEM((tm, FFN_HIDDEN), BF16)],
        compiler_params=pltpu.CompilerParams(
            dimension_semantics=("parallel",), vmem_limit_bytes=VMEM_LIMIT),
        name="proj_ffn",
    )(a, w, x, g, wg, wu, wd)


BIAS_ROWS = 3 * LANES


def _bias_tile_kernel(rb_ref, o_ref):
    h = pl.program_id(0)
    c = lax.broadcasted_iota(I32, (BIAS_ROWS, LANES), 0)
    t = lax.broadcasted_iota(I32, (BIAS_ROWS, LANES), 1)
    dist = jnp.maximum(LANES + t - c, 0)
    exact = REL_BUCKETS // 2
    d_f = jnp.maximum(dist, 1).astype(F32)
    large = exact + (jnp.log(d_f / exact) / math.log(REL_MAX_DIST / exact)
                     * (REL_BUCKETS - exact)).astype(I32)
    large = jnp.minimum(large, REL_BUCKETS - 1)
    bucket = jnp.where(dist < exact, dist, large)
    out = jnp.zeros((BIAS_ROWS, LANES), F32)
    for b in range(REL_BUCKETS):
        out = jnp.where(bucket == b, rb_ref[b, h], out)
    o_ref[...] = (out - rb_ref[REL_BUCKETS - 1, h]) * LOG2E


def _bias_tiles(rel_bias):
    return pl.pallas_call(
        _bias_tile_kernel,
        out_shape=jax.ShapeDtypeStruct((ATT_HEADS, BIAS_ROWS, LANES), F32),
        grid=(ATT_HEADS,),
        in_specs=[pl.BlockSpec(memory_space=pltpu.SMEM)],
        out_specs=pl.BlockSpec((None, BIAS_ROWS, LANES), lambda h: (h, 0, 0)),
        name="rel_bias_tiles",
    )(rel_bias)


def _dsa_kernel(q_ref, k_ref, v_ref, qi_ref, wq_ref, kik_ref, qg_ref, kg_ref, bt_ref, seg_ref,
                segt_ref, o_ref, kn_ref, vt_ref, kib_ref, key_ref, planes_ref, mfar_ref, s_ref, *, tq, chunk):
    i = pl.program_id(1)
    s_len = k_ref.shape[0]
    n_chunk = (i + chunk) // chunk
    n_far = (i - 1 + chunk - 1) // chunk
    w0 = jnp.maximum(i - 1, 0)

    @pl.when(i == 0)
    def _prepare_keys():
        kg = kg_ref[...]
        for g in range(KV_HEADS):
            kn = _rms(k_ref[:, g * HEAD_DIM:(g + 1) * HEAD_DIM], kg).astype(BF16)
            kn_ref[g] = jnp.concatenate([kn, kn], axis=1)
        for j in range(s_len // LANES):
            for pair in range(KV_HEADS // 2):
                v_t = v_ref[j * LANES:(j + 1) * LANES, pair * LANES:(pair + 1) * LANES].T
                vt_ref[2 * pair, j, 0:HEAD_DIM] = v_t[0:HEAD_DIM].astype(BF16)
                vt_ref[2 * pair + 1, j, 0:HEAD_DIM] = v_t[HEAD_DIM:].astype(BF16)
        vt_ref[:, :, HEAD_DIM:V_ROWS] = jnp.ones((KV_HEADS, s_len // LANES, V_ROWS - HEAD_DIM, LANES),
                                                 BF16)
        ki = kik_ref[:, 0:HEAD_DIM].astype(BF16)
        kib_ref[...] = jnp.concatenate([ki, ki], axis=1)
        planes_ref[...] = jnp.zeros(planes_ref.shape, I32)

    ck = chunk * LANES
    krow = lax.broadcasted_iota(I32, (ck, tq), 0)
    tpos = i * tq + lax.broadcasted_iota(I32, (ck, tq), 1)
    low_half = lax.broadcasted_iota(I32, (tq, LANES), 1) < HEAD_DIM

    def split_pair(x):
        return [jnp.where(low_half, x, 0.0).astype(BF16), jnp.where(low_half, 0.0, x).astype(BF16)]

    w_t = wq_ref[...].T
    qi_pairs = [jnp.concatenate(split_pair(qi_ref[:, pair * LANES:(pair + 1) * LANES] * 0.125), axis=0)
                for pair in range(IDX_HEADS // 2)]
    w_heads = [w_t[HEAD_DIM + h:HEAD_DIM + h + 1, :] * (IDX_HEADS ** -0.5)
               for h in range(IDX_HEADS)]

    def score_chunk(c, _):
        kt = kib_ref[pl.ds(pl.multiple_of(c * ck, ck), ck), :]
        dots = [lax.dot_general(kt, qi_pair, NT_DIMS, preferred_element_type=F32)
                for qi_pair in qi_pairs]
        sc = jnp.zeros((ck, tq), F32)
        for h in range(IDX_HEADS):
            d = dots[h // 2][:, (h % 2) * tq:(h % 2 + 1) * tq]
            sc = sc + w_heads[h] * jnp.maximum(d, 0.0)
        sc = jnp.where(sc == 0.0, 0.0, sc)
        sc = jnp.where(c * ck + krow <= tpos, sc, -jnp.inf)
        bits = pltpu.bitcast(sc, I32)
        key_ref[c] = bits ^ ((bits >> 31) & 0x7FFFFFFF)
        return 0

    lax.fori_loop(0, n_chunk, score_chunk, 0)

    def count_keys(preds):
        def body(c, cs):
            key = key_ref[c]
            return tuple(
                cnt + jnp.sum(pred(c, key).astype(I32).reshape(ck // 8, 8, tq), axis=0)
                for cnt, pred in zip(cs, preds))
        cs = lax.fori_loop(0, n_chunk, body,
                           tuple(jnp.zeros((8, tq), I32) for _ in preds))
        return [jnp.sum(cnt, axis=0, keepdims=True) for cnt in cs]

    n_groups = s_len // GROUP_KEYS

    def bit_planes(c, _):
        for half in range(ck // GROUP_KEYS):
            words = [key_ref[c, pl.ds(half * GROUP_KEYS + 8 * r, 8), :] for r in range(32)]
            for shift, keep in ((16, 0x0000FFFF), (8, 0x00FF00FF), (4, 0x0F0F0F0F),
                                (2, 0x33333333), (1, 0x55555555)):
                amount = jnp.full((8, tq), shift, I32)
                for r in range(32):
                    if r & shift == 0:
                        a, b = words[r], words[r + shift]
                        t = (lax.shift_right_logical(a, amount) ^ b) & keep
                        words[r] = a ^ lax.shift_left(t, amount)
                        words[r + shift] = b ^ t
            words[31] = ~words[31]
            for b in range(32):
                planes_ref[c * (ck // GROUP_KEYS) + half, b] = words[b]
        return 0

    def sliced_search():
        lax.fori_loop(0, n_chunk, bit_planes, 0)
        live = n_chunk * (ck // GROUP_KEYS)

        def bit_step(step, state):
            eqs, n_gt, thr_u = state
            b = 31 - step
            planes = [planes_ref[g, b] for g in range(n_groups)]
            ones = lax.population_count(eqs[0] & planes[0])
            for g in range(1, n_groups):
                ones = ones + lax.population_count(eqs[g] & planes[g])
            n_one = jnp.sum(ones, axis=0, keepdims=True)
            take = n_gt + n_one >= TOPK
            n_gt = jnp.where(take, n_gt, n_gt + n_one)
            thr_u = thr_u | jnp.where(take, lax.shift_left(jnp.int32(1), b), 0)
            flip = jnp.broadcast_to(jnp.where(take, 0, -1), (8, tq))
            eqs = tuple(eqs[g] & (planes[g] ^ flip) for g in range(n_groups))
            return eqs, n_gt, thr_u

        eqs = tuple(jnp.full((8, tq), jnp.where(g < live, -1, 0), I32) for g in range(n_groups))
        zero = jnp.zeros((1, tq), I32)
        eqs, n_gt, thr_u = lax.fori_loop(0, 32, bit_step, (eqs, zero, zero))
        n_eq = lax.population_count(eqs[0])
        for g in range(1, n_groups):
            n_eq = n_eq + lax.population_count(eqs[g])
        return thr_u ^ INT_MIN, n_gt, jnp.sum(n_eq, axis=0, keepdims=True)

    thr, n_gt, n_eq = lax.cond(
        (i + 1) * tq <= TOPK,
        lambda: (jnp.full((1, tq), KEY_NEG_INF, I32), jnp.zeros((1, tq), I32), jnp.zeros((1, tq), I32)),
        sliced_search)
    thr_b = jnp.broadcast_to(thr, (ck, tq))
    need = TOPK - n_gt
    tied = (n_eq > need) & (thr != KEY_NEG_INF)
    any_tied = jnp.max(tied.astype(I32)) > 0

    def tie_bit(b, cut):
        cand = cut + lax.shift_left(jnp.int32(1), 10 - b)
        cand_b = jnp.broadcast_to(cand, (ck, tq))
        (cnt,) = count_keys([lambda c, key: (key == thr_b) & (c * ck + krow < cand_b)])
        return jnp.where(cnt < need, cand, cut)

    cut = lax.cond(any_tied,
                   lambda: lax.fori_loop(0, 11, tie_bit, jnp.zeros((1, tq), I32)),
                   lambda: jnp.full((1, tq), s_len, I32))

    def selection_mask(key, spos, tpos):
        thr_k = jnp.broadcast_to(thr, key.shape)
        cut_k = jnp.broadcast_to(cut, key.shape)
        sel = (key > thr_k) | ((key == thr_k) & (spos <= cut_k))
        return jnp.where(sel & (spos <= tpos), 0.0, NEG)

    def mask_chunk(c, _):
        spos = c * ck + krow
        madd = selection_mask(key_ref[c], spos, tpos)
        mfar_ref[c] = jnp.where(spos < (i - 1) * LANES, madd, NEG)
        return 0

    lax.fori_loop(0, n_far, mask_chunk, 0)

    def tile_mask(tile):
        c, u = tile // chunk, tile % chunk
        key = key_ref[c, pl.ds(pl.multiple_of(u * LANES, LANES), LANES), :]
        krow_t = lax.broadcasted_iota(I32, (LANES, tq), 0)
        tpos_t = i * tq + lax.broadcasted_iota(I32, (LANES, tq), 1)
        return selection_mask(key, tile * LANES + krow_t, tpos_t)

    q = q_ref[...]
    sq = q * q
    sq_hi = sq.astype(BF16)
    sq_lo = (sq - sq_hi.astype(F32)).astype(BF16)
    seg = seg_ref[...]
    ssum = (jnp.dot(sq_hi, seg, preferred_element_type=F32)
            + jnp.dot(sq_lo, seg, preferred_element_type=F32))
    rinv = lax.rsqrt(ssum * (1.0 / HEAD_DIM) + EPS)
    pieces, rest = [], rinv
    for _ in range(3):
        piece = rest.astype(BF16)
        pieces.append(piece)
        rest = rest - piece.astype(F32)
    segt = segt_ref[...]
    rinv_b = sum(jnp.dot(piece, segt, preferred_element_type=F32) for piece in pieces)
    qn = q * rinv_b * (qg_ref[...] * (HEAD_DIM ** -0.5 * LOG2E))
    qs = []
    for g in range(KV_HEADS):
        rows = []
        for pair in range(2 * g, 2 * g + 2):
            rows += split_pair(qn[:, pair * LANES:(pair + 1) * LANES])
        qs.append(jnp.concatenate(rows, axis=0))

    groups = range(KV_HEADS)

    def attend(sts, n_keys, first_tile, add_fn):
        key_start = pl.multiple_of(first_tile * LANES, LANES)
        for g in groups:
            kt = kn_ref[g, pl.ds(key_start, n_keys), :]
            add = jnp.concatenate([add_fn(g * GQA_REP + r) for r in range(GQA_REP)], axis=1)
            s_ref[g, 0:n_keys, :] = lax.dot_general(kt, qs[g], NT_DIMS,
                                                    preferred_element_type=F32) + add
        ps, stats = [], []
        for g in groups:
            m, _ = sts[g]
            m_new = jnp.maximum(m, jnp.max(s_ref[g, 0:n_keys, :], axis=0, keepdims=True))
            stats.append((m_new, jnp.exp2(m - m_new)))
            ps.append(jnp.exp2(s_ref[g, 0:n_keys, :] - m_new).astype(BF16))
        new = []
        for g in groups:
            v_t = jnp.concatenate([vt_ref[g, first_tile + u] for u in range(n_keys // LANES)],
                                  axis=1)
            new.append((stats[g][0], stats[g][1] * sts[g][1]
                        + jnp.dot(v_t, ps[g], preferred_element_type=F32)))
        return tuple(new)

    def far_chunk(c, sts):
        madd = mfar_ref[c]
        return attend(sts, ck, c * chunk, lambda head: madd)

    init = (jnp.full((1, GQA_REP * tq), NEG, F32), jnp.zeros((V_ROWS, GQA_REP * tq), F32))
    sts = lax.fori_loop(0, n_far, far_chunk, (init,) * KV_HEADS)

    win_mask = jnp.concatenate([tile_mask(w0), tile_mask(w0 + 1)], axis=0)
    bias_row = pl.multiple_of(jnp.where(i == 0, LANES, 0), LANES)
    sts = attend(sts, 2 * LANES, w0,
                 lambda head: bt_ref[head, pl.ds(bias_row, 2 * LANES), :] + win_mask)
    for g in range(KV_HEADS):
        acc = sts[g][1]
        out_t = acc[0:HEAD_DIM] / acc[HEAD_DIM:HEAD_DIM + 1]
        for pair in range(GQA_REP // 2):
            two = jnp.concatenate(
                [out_t[:, (2 * pair) * tq:(2 * pair + 1) * tq],
                 out_t[:, (2 * pair + 1) * tq:(2 * pair + 2) * tq]], axis=0)
            col = (g * GQA_REP + 2 * pair) * HEAD_DIM
            o_ref[:, col:col + LANES] = two.T.astype(o_ref.dtype)


def _dsa(proj, q_gain, k_gain, bias_tiles, *, tq=128, chunk=4):
    b, s, _ = proj.shape
    n_t = s // LANES
    att_width = ATT_HEADS * HEAD_DIM
    seg = (jnp.arange(att_width)[:, None] // HEAD_DIM == jnp.arange(LANES)[None, :]).astype(BF16)
    q_gain_all = jnp.tile(q_gain, (1, ATT_HEADS))
    return pl.pallas_call(
        functools.partial(_dsa_kernel, tq=tq, chunk=chunk),
        out_shape=jax.ShapeDtypeStruct((b, s, att_width), BF16),
        grid=(b, s // tq),
        in_specs=[
            pl.BlockSpec((None, tq, 1024), lambda bi, i: (bi, i, 0)),
            pl.BlockSpec((None, s, 256), lambda bi, i: (bi, 0, 4)),
            pl.BlockSpec((None, s, 256), lambda bi, i: (bi, 0, 5)),
            pl.BlockSpec((None, tq, 512), lambda bi, i: (bi, i, 3)),
            pl.BlockSpec((None, tq, LANES), lambda bi, i: (bi, i, 16)),
            pl.BlockSpec((None, s, LANES), lambda bi, i: (bi, 0, 16)),
            pl.BlockSpec((1, att_width), lambda bi, i: (0, 0)),
            pl.BlockSpec((1, HEAD_DIM), lambda bi, i: (0, 0)),
            pl.BlockSpec((ATT_HEADS, BIAS_ROWS, LANES), lambda bi, i: (0, 0, 0)),
            pl.BlockSpec((att_width, LANES), lambda bi, i: (0, 0)),
            pl.BlockSpec((LANES, att_width), lambda bi, i: (0, 0)),
        ],
        out_specs=pl.BlockSpec((None, tq, att_width), lambda bi, i: (bi, i, 0)),
        scratch_shapes=[
            pltpu.VMEM((KV_HEADS, s, LANES), BF16),
            pltpu.VMEM((KV_HEADS, n_t, V_ROWS, LANES), BF16),
            pltpu.VMEM((s, LANES), BF16),
            pltpu.VMEM((n_t // chunk, chunk * LANES, tq), I32),
            pltpu.VMEM((s // GROUP_KEYS, 32, 8, tq), I32),
            pltpu.VMEM((n_t // chunk, chunk * LANES, tq), F32),
            pltpu.VMEM((KV_HEADS, chunk * LANES, GQA_REP * tq), F32),
        ],
        compiler_params=pltpu.CompilerParams(
            dimension_semantics=("parallel", "arbitrary"), vmem_limit_bytes=VMEM_LIMIT),
        name="dsa_attention",
    )(proj, proj, proj, proj, proj, proj, q_gain_all, k_gain, bias_tiles, seg, seg.T)


def kernel(x, norm_mix, norm_ffn, ev_w_in, ev_conv_w, ev_w_out, od_w_in, od_q_gain,
           od_k_gain, od_w_out, rel_bias, ffn_w_gate, ffn_w_up, ffn_w_down):
    b, s, d = x.shape
    n_tok = b * s
    bf = lambda w: w.astype(BF16)

    proj0 = _norm_matmul(x.reshape(n_tok, d), norm_mix[0:1], bf(ev_w_in[0]))
    proj0 = proj0.reshape(b, s, -1)
    a_out = _stickbreak(proj0)
    x2 = _mix0_ffn(a_out, proj0, ev_conv_w[0], bf(ev_w_out[0]), x, norm_ffn[0:1],
                   bf(ffn_w_gate[0]), bf(ffn_w_up[0]), bf(ffn_w_down[0])).reshape(n_tok, d)

    w_in1 = jnp.pad(od_w_in[0], ((0, 0), (0, ODD_IN_PAD - od_w_in.shape[2])))
    proj1 = _norm_matmul(x2, norm_mix[1:2], bf(w_in1)).reshape(b, s, ODD_IN_PAD)
    att = _dsa(proj1, od_q_gain[0:1], od_k_gain[0:1], _bias_tiles(rel_bias))
    x4 = _proj_ffn(att.reshape(n_tok, -1), bf(od_w_out[0]), x2, norm_ffn[1:2],
                   bf(ffn_w_gate[1]), bf(ffn_w_up[1]), bf(ffn_w_down[1]))
    return x4.reshape(b, s, d)
```

```python
import functools
import math

import jax
import jax.numpy as jnp
from jax import lax
from jax.experimental import pallas as pl
from jax.experimental.pallas import tpu as pltpu

F32 = jnp.float32
BF16 = jnp.bfloat16
I32 = jnp.int32

D_MODEL = 1024
SEQ = 2048
HEAD_DIM = 64
SB_WIDTH = 512
CONV_CH = 512
ATT_HEADS = 16
KV_HEADS = 4
GQA_REP = 4
IDX_HEADS = 8
TOPK = 256
REL_BUCKETS = 32
REL_MAX_DIST = 128
FFN_HIDDEN = 2816
EPS = 1e-6
ODD_IN_PAD = 2176

LOG2E = 1.4426950408889634
LANES = 128
VMEM_LIMIT = 56 * 1024 * 1024
NEG = -1e30
INT_MIN = -(2 ** 31)
GROUP_KEYS = 32 * 8
V_ROWS = HEAD_DIM + 16
UNDERFLOW_MARGIN = 160.0
NORM_SLACK = 1.03
KEY_NEG_INF = -(2 ** 31) + 0x7FFFFF

NT_DIMS = (((1,), (1,)), ((), ()))


def _rms(x, g):
    ms = jnp.mean(x * x, axis=-1, keepdims=True)
    return x * lax.rsqrt(ms + EPS) * g


def _norm_matmul_kernel(x_ref, g_ref, w_ref, o_ref, *, tn):
    h = _rms(x_ref[...], g_ref[...]).astype(BF16)
    n = w_ref.shape[1]
    for start in range(0, n, tn):
        size = min(tn, n - start)
        o_ref[:, start:start + size] = jnp.dot(
            h, w_ref[:, start:start + size], preferred_element_type=F32)


def _norm_matmul(x, g, w, *, tm=1024, tn=512):
    m, d = x.shape
    n = w.shape[1]
    return pl.pallas_call(
        functools.partial(_norm_matmul_kernel, tn=tn),
        out_shape=jax.ShapeDtypeStruct((m, n), F32),
        grid=(m // tm,),
        in_specs=[
            pl.BlockSpec((tm, d), lambda i: (i, 0)),
            pl.BlockSpec((1, d), lambda i: (0, 0)),
            pl.BlockSpec((d, n), lambda i: (0, 0)),
        ],
        out_specs=pl.BlockSpec((tm, n), lambda i: (i, 0)),
        compiler_params=pltpu.CompilerParams(
            dimension_semantics=("parallel",), vmem_limit_bytes=VMEM_LIMIT),
        name="norm_matmul",
    )(x, g, w)


def _sb_kernel(q_ref, k_ref, v_ref, o_ref, vt_ref, kmax_ref, *, tq, pairs):
    qi = pl.program_id(2)
    n_sub = tq // LANES
    s_len = k_ref.shape[0]
    width = pairs * LANES
    head_of = (lax.broadcasted_iota(I32, (width, LANES), 0) // HEAD_DIM
               == lax.broadcasted_iota(I32, (width, LANES), 1)).astype(BF16)

    def max_sq_norm(x):
        sq = (x * x).astype(BF16)
        return jnp.max(jnp.dot(sq, head_of, preferred_element_type=F32), axis=0, keepdims=True)

    @pl.when(qi == 0)
    def _per_sequence_setup():
        for p in range(pairs):
            for j in range(s_len // LANES):
                vt_ref[p, j] = v_ref[j * LANES:(j + 1) * LANES,
                                     p * LANES:(p + 1) * LANES].T.astype(BF16)
        kmax = jnp.zeros((1, LANES), F32)
        for r in range(0, s_len, tq):
            kmax = jnp.maximum(kmax, max_sq_norm(k_ref[r:r + tq, :]))
        kmax_ref[...] = kmax

    lane = lax.broadcasted_iota(I32, (tq, LANES), 1)
    krow = lax.broadcasted_iota(I32, (tq, tq), 0)
    qcol = lax.broadcasted_iota(I32, (tq, tq), 1)
    from_here = (qcol >= krow).astype(BF16)
    causal = krow < qcol
    q_heads = []
    for p in range(pairs):
        q = q_ref[:, p * LANES:(p + 1) * LANES] * (HEAD_DIM ** -0.5 * LOG2E)
        for h in range(2):
            in_head = (lane >= HEAD_DIM * h) & (lane < HEAD_DIM * (h + 1))
            q_heads.append(jnp.where(in_head, q, 0.0).astype(BF16))

    def block(kb, st, masked):
        start = pl.multiple_of(kb * tq, tq)
        heads = range(2 * pairs)
        kblks = [k_ref[pl.ds(start, tq), p * LANES:(p + 1) * LANES].astype(BF16)
                 for p in range(pairs)]
        v_ts = [jnp.concatenate([vt_ref[p, kb * n_sub + u] for u in range(n_sub)], axis=1)
                for p in range(pairs)]
        zs = [lax.dot_general(kblks[idx // 2], q_heads[idx], NT_DIMS, preferred_element_type=F32)
              for idx in heads]
        softplus, his, los = [], [], []
        for z in zs:
            sp = jnp.maximum(z, 0.0) + jnp.log2(1.0 + jnp.exp2(-jnp.abs(z)))
            if masked:
                sp = jnp.where(causal, sp, 0.0)
            hi = sp.astype(BF16)
            softplus.append(sp)
            his.append(hi)
            los.append((sp - hi.astype(F32)).astype(BF16))
        since = [jnp.dot(from_here, his[idx], preferred_element_type=F32)
                 + jnp.dot(from_here, los[idx], preferred_element_type=F32) for idx in heads]
        ws = []
        for idx in heads:
            w = jnp.exp2(zs[idx] - since[idx] - st[2 * idx])
            if masked:
                w = jnp.where(causal, w, 0.0)
            ws.append(w.astype(BF16))
        new = []
        for idx in heads:
            h = idx % 2
            acc = st[2 * idx + 1] + jnp.dot(v_ts[idx // 2][h * HEAD_DIM:(h + 1) * HEAD_DIM, :],
                                            ws[idx], preferred_element_type=F32)
            carry = st[2 * idx] + jnp.sum(softplus[idx], axis=0, keepdims=True)
            new += [carry, acc]
        return tuple(new)

    z_bound = jnp.sqrt(max_sq_norm(q_ref[...]) * kmax_ref[...]) * (HEAD_DIM ** -0.5 * LOG2E * NORM_SLACK)
    z_bounds = [jnp.broadcast_to(z_bound[:, idx:idx + 1], (1, tq)) for idx in range(2 * pairs)]

    def all_underflow(st):
        slack = st[0] - z_bounds[0]
        for idx in range(1, 2 * pairs):
            slack = jnp.minimum(slack, st[2 * idx] - z_bounds[idx])
        return jnp.min(slack) > UNDERFLOW_MARGIN

    def walk(state):
        n, _, st = state
        st = block(qi - 1 - n, st, False)
        return n + 1, all_underflow(st), st

    st = (jnp.zeros((1, tq), F32), jnp.zeros((HEAD_DIM, tq), F32)) * (2 * pairs)
    st = block(qi, st, True)
    _, _, st = lax.while_loop(lambda state: (state[0] < qi) & jnp.logical_not(state[1]), walk,
                              (jnp.int32(0), all_underflow(st), st))
    for p in range(pairs):
        out_t = jnp.concatenate([st[4 * p + 1], st[4 * p + 3]], axis=0)
        for u in range(n_sub):
            o_ref[u * LANES:(u + 1) * LANES, p * LANES:(p + 1) * LANES] = (
                out_t[:, u * LANES:(u + 1) * LANES].T.astype(o_ref.dtype))


def _stickbreak(proj, *, tq=256, pairs=4):
    b, s, _ = proj.shape
    width = pairs * LANES
    n_steps = SB_WIDTH // width
    return pl.pallas_call(
        functools.partial(_sb_kernel, tq=tq, pairs=pairs),
        out_shape=jax.ShapeDtypeStruct((b, s, SB_WIDTH), BF16),
        grid=(b, n_steps, s // tq),
        in_specs=[
            pl.BlockSpec((None, tq, width), lambda bi, hp, qi: (bi, qi, hp)),
            pl.BlockSpec((None, s, width), lambda bi, hp, qi: (bi, 0, n_steps + hp)),
            pl.BlockSpec((None, s, width), lambda bi, hp, qi: (bi, 0, 2 * n_steps + hp)),
        ],
        out_specs=pl.BlockSpec((None, tq, width), lambda bi, hp, qi: (bi, qi, hp)),
        scratch_shapes=[pltpu.VMEM((pairs, s // LANES, LANES, LANES), BF16),
                        pltpu.VMEM((1, LANES), F32)],
        compiler_params=pltpu.CompilerParams(
            dimension_semantics=("parallel", "parallel", "arbitrary"),
            vmem_limit_bytes=VMEM_LIMIT),
        name="stickbreak",
    )(proj, proj, proj)


def _ffn(x, g_ref, wg_ref, wu_ref, wd_ref, act_ref, th):
    h = _rms(x, g_ref[...]).astype(BF16)
    for start in range(0, FFN_HIDDEN, th):
        hg = jnp.dot(h, wg_ref[:, start:start + th], preferred_element_type=F32)
        hu = jnp.dot(h, wu_ref[:, start:start + th], preferred_element_type=F32)
        sig = 1.0 / (1.0 + jnp.exp(-hg))
        act_ref[:, start:start + th] = (hg * sig * hu).astype(BF16)
    return x + jnp.dot(act_ref[...], wd_ref[...], preferred_element_type=F32)


def _ffn_specs(d, index_map):
    resident = pl.Buffered(1)
    return [
        pl.BlockSpec((1, d), index_map),
        pl.BlockSpec((d, FFN_HIDDEN), index_map, pipeline_mode=resident),
        pl.BlockSpec((d, FFN_HIDDEN), index_map, pipeline_mode=resident),
        pl.BlockSpec((FFN_HIDDEN, d), index_map, pipeline_mode=resident),
    ]


def _mix0_ffn_kernel(a_ref, bg_ref, cg_ref, u_ref, cgh_ref, uh_ref, cw_ref, w_ref, x_ref,
                     g_ref, wg_ref, wu_ref, wd_ref, o_ref, act_ref, *, th):
    i = pl.program_id(1)
    g = cg_ref[...] * u_ref[...]
    gh = cgh_ref[...] * uh_ref[...]
    gh = jnp.where(i == 0, 0.0, gh)
    row = lax.broadcasted_iota(I32, g.shape, 0)
    g1 = jnp.where(row == 0, gh[7:8, :], pltpu.roll(g, 1, axis=0))
    g2 = pltpu.roll(g, 2, axis=0)
    g2 = jnp.where(row == 0, gh[6:7, :], jnp.where(row == 1, gh[7:8, :], g2))
    cw = cw_ref[...]
    y = bg_ref[...] * (cw[0:1, :] * g2 + cw[1:2, :] * g1 + cw[2:3, :] * g)
    acc = jnp.dot(a_ref[...], w_ref[0:SB_WIDTH, :], preferred_element_type=F32)
    acc = acc + jnp.dot(y.astype(BF16), w_ref[SB_WIDTH:, :], preferred_element_type=F32)
    o_ref[...] = _ffn(x_ref[...] + acc, g_ref, wg_ref, wu_ref, wd_ref, act_ref, th)


def _mix0_ffn(a_out, proj, conv_w, w_out, x, g, wg, wu, wd, *, tm=512, th=256):
    b, s, d = x.shape
    halo = lambda col: (lambda bi, i: (bi, jnp.maximum(i * (tm // 8) - 1, 0), col))
    return pl.pallas_call(
        functools.partial(_mix0_ffn_kernel, th=th),
        out_shape=jax.ShapeDtypeStruct((b, s, d), F32),
        grid=(b, s // tm),
        in_specs=[
            pl.BlockSpec((None, tm, SB_WIDTH), lambda bi, i: (bi, i, 0)),
            pl.BlockSpec((None, tm, CONV_CH), lambda bi, i: (bi, i, 3)),
            pl.BlockSpec((None, tm, CONV_CH), lambda bi, i: (bi, i, 4)),
            pl.BlockSpec((None, tm, CONV_CH), lambda bi, i: (bi, i, 5)),
            pl.BlockSpec((None, 8, CONV_CH), halo(4)),
            pl.BlockSpec((None, 8, CONV_CH), halo(5)),
            pl.BlockSpec((3, CONV_CH), lambda bi, i: (0, 0)),
            pl.BlockSpec((SB_WIDTH + CONV_CH, d), lambda bi, i: (0, 0)),
            pl.BlockSpec((None, tm, d), lambda bi, i: (bi, i, 0)),
        ] + _ffn_specs(d, lambda bi, i: (0, 0)),
        out_specs=pl.BlockSpec((None, tm, d), lambda bi, i: (bi, i, 0)),
        scratch_shapes=[pltpu.VMEM((tm, FFN_HIDDEN), BF16)],
        compiler_params=pltpu.CompilerParams(
            dimension_semantics=("parallel", "arbitrary"), vmem_limit_bytes=VMEM_LIMIT),
        name="mix0_ffn",
    )(a_out, proj, proj, proj, proj, proj, conv_w, w_out, x, g, wg, wu, wd)


def _proj_ffn_kernel(a_ref, w_ref, x_ref, g_ref, wg_ref, wu_ref, wd_ref, o_ref, act_ref, *, th):
    x1 = x_ref[...] + jnp.dot(a_ref[...], w_ref[...], preferred_element_type=F32)
    o_ref[...] = _ffn(x1, g_ref, wg_ref, wu_ref, wd_ref, act_ref, th)


def _proj_ffn(a, w, x, g, wg, wu, wd, *, tm=512, th=256):
    m, k = a.shape
    d = w.shape[1]
    return pl.pallas_call(
        functools.partial(_proj_ffn_kernel, th=th),
        out_shape=jax.ShapeDtypeStruct((m, d), F32),
        grid=(m // tm,),
        in_specs=[
            pl.BlockSpec((tm, k), lambda i: (i, 0)),
            pl.BlockSpec((k, d), lambda i: (0, 0)),
            pl.BlockSpec((tm, d), lambda i: (i, 0)),
        ] + _ffn_specs(d, lambda i: (0, 0)),
        out_specs=pl.BlockSpec((tm, d), lambda i: (i, 0)),
        scratch_shapes=[pltpu.VMEM((tm, FFN_HIDDEN), BF16)],
        compiler_params=pltpu.CompilerParams(
            dimension_semantics=("parallel",), vmem_limit_bytes=VMEM_LIMIT),
        name="proj_ffn",
    )(a, w, x, g, wg, wu, wd)


BIAS_ROWS = 3 * LANES


def _bias_tile_kernel(rb_ref, o_ref):
    h = pl.program_id(0)
    c = lax.broadcasted_iota(I32, (BIAS_ROWS, LANES), 0)
    t = lax.broadcasted_iota(I32, (BIAS_ROWS, LANES), 1)
    dist = jnp.maximum(LANES + t - c, 0)
    exact = REL_BUCKETS // 2
    d_f = jnp.maximum(dist, 1).astype(F32)
    large = exact + (jnp.log(d_f / exact) / math.log(REL_MAX_DIST / exact)
                     * (REL_BUCKETS - exact)).astype(I32)
    large = jnp.minimum(large, REL_BUCKETS - 1)
    bucket = jnp.where(dist < exact, dist, large)
    out = jnp.zeros((BIAS_ROWS, LANES), F32)
    for b in range(REL_BUCKETS):
        out = jnp.where(bucket == b, rb_ref[b, h], out)
    o_ref[...] = (out - rb_ref[REL_BUCKETS - 1, h]) * LOG2E


def _bias_tiles(rel_bias):
    return pl.pallas_call(
        _bias_tile_kernel,
        out_shape=jax.ShapeDtypeStruct((ATT_HEADS, BIAS_ROWS, LANES), F32),
        grid=(ATT_HEADS,),
        in_specs=[pl.BlockSpec(memory_space=pltpu.SMEM)],
        out_specs=pl.BlockSpec((None, BIAS_ROWS, LANES), lambda h: (h, 0, 0)),
        name="rel_bias_tiles",
    )(rel_bias)


def _dsa_kernel(q_ref, k_ref, v_ref, qi_ref, wq_ref, kik_ref, qg_ref, kg_ref, bt_ref, seg_ref,
                segt_ref, o_ref, kn_ref, vt_ref, kib_ref, key_ref, planes_ref, mfar_ref, s_ref, *, tq, chunk):
    i = pl.program_id(1)
    s_len = k_ref.shape[0]
    n_chunk = (i + chunk) // chunk
    n_far = (i - 1 + chunk - 1) // chunk
    w0 = jnp.maximum(i - 1, 0)

    @pl.when(i == 0)
    def _prepare_keys():
        kg = kg_ref[...]
        for g in range(KV_HEADS):
            kn = _rms(k_ref[:, g * HEAD_DIM:(g + 1) * HEAD_DIM], kg).astype(BF16)
            kn_ref[g] = jnp.concatenate([kn, kn], axis=1)
        for j in range(s_len // LANES):
            for pair in range(KV_HEADS // 2):
                v_t = v_ref[j * LANES:(j + 1) * LANES, pair * LANES:(pair + 1) * LANES].T
                vt_ref[2 * pair, j, 0:HEAD_DIM] = v_t[0:HEAD_DIM].astype(BF16)
                vt_ref[2 * pair + 1, j, 0:HEAD_DIM] = v_t[HEAD_DIM:].astype(BF16)
        vt_ref[:, :, HEAD_DIM:V_ROWS] = jnp.ones((KV_HEADS, s_len // LANES, V_ROWS - HEAD_DIM, LANES),
                                                 BF16)
        ki = kik_ref[:, 0:HEAD_DIM].astype(BF16)
        kib_ref[...] = jnp.concatenate([ki, ki], axis=1)
        planes_ref[...] = jnp.zeros(planes_ref.shape, I32)

    ck = chunk * LANES
    krow = lax.broadcasted_iota(I32, (ck, tq), 0)
    tpos = i * tq + lax.broadcasted_iota(I32, (ck, tq), 1)
    low_half = lax.broadcasted_iota(I32, (tq, LANES), 1) < HEAD_DIM

    def split_pair(x):
        return [jnp.where(low_half, x, 0.0).astype(BF16), jnp.where(low_half, 0.0, x).astype(BF16)]

    w_t = wq_ref[...].T
    qi_pairs = [jnp.concatenate(split_pair(qi_ref[:, pair * LANES:(pair + 1) * LANES] * 0.125), axis=0)
                for pair in range(IDX_HEADS // 2)]
    w_heads = [w_t[HEAD_DIM + h:HEAD_DIM + h + 1, :] * (IDX_HEADS ** -0.5)
               for h in range(IDX_HEADS)]

    def bit_planes(c):
        for half in range(ck // GROUP_KEYS):
            words = [key_ref[c, pl.ds(half * GROUP_KEYS + 8 * r, 8), :] for r in range(32)]
            for shift, keep in ((16, 0x0000FFFF), (8, 0x00FF00FF), (4, 0x0F0F0F0F),
                                (2, 0x33333333), (1, 0x55555555)):
                amount = jnp.full((8, tq), shift, I32)
                for r in range(32):
                    if r & shift == 0:
                        a, b = words[r], words[r + shift]
                        t = (lax.shift_right_logical(a, amount) ^ b) & keep
                        words[r] = a ^ lax.shift_left(t, amount)
                        words[r + shift] = b ^ t
            words[31] = ~words[31]
            for b in range(32):
                planes_ref[c * (ck // GROUP_KEYS) + half, b] = words[b]

    def score_chunk(c):
        kt = kib_ref[pl.ds(pl.multiple_of(c * ck, ck), ck), :]
        dots = [lax.dot_general(kt, qi_pair, NT_DIMS, preferred_element_type=F32)
                for qi_pair in qi_pairs]
        sc = jnp.zeros((ck, tq), F32)
        for h in range(IDX_HEADS):
            d = dots[h // 2][:, (h % 2) * tq:(h % 2 + 1) * tq]
            sc = sc + w_heads[h] * jnp.maximum(d, 0.0)
        sc = jnp.where(sc == 0.0, 0.0, sc)
        sc = jnp.where(c * ck + krow <= tpos, sc, -jnp.inf)
        bits = pltpu.bitcast(sc, I32)
        key_ref[c] = bits ^ ((bits >> 31) & 0x7FFFFFFF)

    score_chunk(0)

    def scan(c, _):
        bit_planes(c - 1)
        score_chunk(c)
        return 0

    lax.fori_loop(1, n_chunk, scan, 0)

    def count_keys(preds):
        def body(c, cs):
            key = key_ref[c]
            return tuple(
                cnt + jnp.sum(pred(c, key).astype(I32).reshape(ck // 8, 8, tq), axis=0)
                for cnt, pred in zip(cs, preds))
        cs = lax.fori_loop(0, n_chunk, body,
                           tuple(jnp.zeros((8, tq), I32) for _ in preds))
        return [jnp.sum(cnt, axis=0, keepdims=True) for cnt in cs]

    n_groups = s_len // GROUP_KEYS

    def sliced_search():
        bit_planes(n_chunk - 1)
        live = n_chunk * (ck // GROUP_KEYS)

        def two_bits(step, state):
            eqs, n_gt, thr_u = state
            b0 = 30 - 2 * step
            s11, s10, s01, s00 = [], [], [], []
            for g in range(n_groups):
                with_hi = eqs[g] & planes_ref[g, b0 + 1]
                without_hi = eqs[g] ^ with_hi
                low = planes_ref[g, b0]
                s11.append(with_hi & low)
                s10.append(with_hi ^ s11[g])
                s01.append(without_hi & low)
                s00.append(without_hi ^ s01[g])

            def count(sets):
                total = lax.population_count(sets[0])
                for g in range(1, n_groups):
                    total = total + lax.population_count(sets[g])
                return jnp.sum(total, axis=0, keepdims=True)

            up11 = n_gt + count(s11)
            up10 = up11 + count(s10)
            up01 = up10 + count(s01)
            in11 = up11 >= TOPK
            in10 = jnp.logical_not(in11) & (up10 >= TOPK)
            in01 = jnp.logical_not(in11 | in10) & (up01 >= TOPK)
            n_gt = jnp.where(in11, n_gt, jnp.where(in10, up11, jnp.where(in01, up10, up01)))
            bits = jnp.where(in11, 3, jnp.where(in10, 2, jnp.where(in01, 1, 0)))
            thr_u = thr_u | jnp.left_shift(bits, b0)
            m11, m10, m01 = (jnp.broadcast_to(m, (8, tq)) for m in (in11, in10, in01))
            eqs = tuple(jnp.where(m11, s11[g], jnp.where(m10, s10[g], jnp.where(m01, s01[g], s00[g])))
                        for g in range(n_groups))
            return eqs, n_gt, thr_u

        eqs = tuple(jnp.full((8, tq), jnp.where(g < live, -1, 0), I32) for g in range(n_groups))
        zero = jnp.zeros((1, tq), I32)
        eqs, n_gt, thr_u = lax.fori_loop(0, 16, two_bits, (eqs, zero, zero))
        n_eq = lax.population_count(eqs[0])
        for g in range(1, n_groups):
            n_eq = n_eq + lax.population_count(eqs[g])
        return thr_u ^ INT_MIN, n_gt, jnp.sum(n_eq, axis=0, keepdims=True)

    thr, n_gt, n_eq = lax.cond(
        (i + 1) * tq <= TOPK,
        lambda: (jnp.full((1, tq), KEY_NEG_INF, I32), jnp.zeros((1, tq), I32), jnp.zeros((1, tq), I32)),
        sliced_search)
    thr_b = jnp.broadcast_to(thr, (ck, tq))
    need = TOPK - n_gt
    tied = (n_eq > need) & (thr != KEY_NEG_INF)
    any_tied = jnp.max(tied.astype(I32)) > 0

    def tie_bit(b, cut):
        cand = cut + lax.shift_left(jnp.int32(1), 10 - b)
        cand_b = jnp.broadcast_to(cand, (ck, tq))
        (cnt,) = count_keys([lambda c, key: (key == thr_b) & (c * ck + krow < cand_b)])
        return jnp.where(cnt < need, cand, cut)

    cut = lax.cond(any_tied,
                   lambda: lax.fori_loop(0, 11, tie_bit, jnp.zeros((1, tq), I32)),
                   lambda: jnp.full((1, tq), s_len, I32))

    def selection_mask(key, spos, tpos):
        thr_k = jnp.broadcast_to(thr, key.shape)
        cut_k = jnp.broadcast_to(cut, key.shape)
        sel = (key > thr_k) | ((key == thr_k) & (spos <= cut_k))
        return jnp.where(sel & (spos <= tpos), 0.0, NEG)

    def mask_chunk(c, _):
        spos = c * ck + krow
        madd = selection_mask(key_ref[c], spos, tpos)
        mfar_ref[c] = jnp.where(spos < (i - 1) * LANES, madd, NEG)
        return 0

    lax.fori_loop(0, n_far, mask_chunk, 0)

    def tile_mask(tile):
        c, u = tile // chunk, tile % chunk
        key = key_ref[c, pl.ds(pl.multiple_of(u * LANES, LANES), LANES), :]
        krow_t = lax.broadcasted_iota(I32, (LANES, tq), 0)
        tpos_t = i * tq + lax.broadcasted_iota(I32, (LANES, tq), 1)
        return selection_mask(key, tile * LANES + krow_t, tpos_t)

    q = q_ref[...]
    sq = q * q
    sq_hi = sq.astype(BF16)
    sq_lo = (sq - sq_hi.astype(F32)).astype(BF16)
    seg = seg_ref[...]
    ssum = (jnp.dot(sq_hi, seg, preferred_element_type=F32)
            + jnp.dot(sq_lo, seg, preferred_element_type=F32))
    rinv = lax.rsqrt(ssum * (1.0 / HEAD_DIM) + EPS)
    pieces, rest = [], rinv
    for _ in range(2):
        piece = rest.astype(BF16)
        pieces.append(piece)
        rest = rest - piece.astype(F32)
    segt = segt_ref[...]
    rinv_b = sum(jnp.dot(piece, segt, preferred_element_type=F32) for piece in pieces)
    qn = q * rinv_b * (qg_ref[...] * (HEAD_DIM ** -0.5 * LOG2E))
    qs = []
    for g in range(KV_HEADS):
        rows = []
        for pair in range(2 * g, 2 * g + 2):
            rows += split_pair(qn[:, pair * LANES:(pair + 1) * LANES])
        qs.append(jnp.concatenate(rows, axis=0))

    groups = range(KV_HEADS)

    def attend(sts, n_keys, first_tile, add_fn):
        key_start = pl.multiple_of(first_tile * LANES, LANES)
        for g in groups:
            kt = kn_ref[g, pl.ds(key_start, n_keys), :]
            add = jnp.concatenate([add_fn(g * GQA_REP + r) for r in range(GQA_REP)], axis=1)
            s_ref[g, 0:n_keys, :] = lax.dot_general(kt, qs[g], NT_DIMS,
                                                    preferred_element_type=F32) + add
        ps, stats = [], []
        for g in groups:
            m, _ = sts[g]
            m_new = jnp.maximum(m, jnp.max(s_ref[g, 0:n_keys, :], axis=0, keepdims=True))
            stats.append((m_new, jnp.exp2(m - m_new)))
            ps.append(jnp.exp2(s_ref[g, 0:n_keys, :] - m_new).astype(BF16))
        new = []
        for g in groups:
            v_t = jnp.concatenate([vt_ref[g, first_tile + u] for u in range(n_keys // LANES)],
                                  axis=1)
            new.append((stats[g][0], stats[g][1] * sts[g][1]
                        + jnp.dot(v_t, ps[g], preferred_element_type=F32)))
        return tuple(new)

    def far_chunk(c, sts):
        madd = mfar_ref[c]
        return attend(sts, ck, c * chunk, lambda head: madd)

    init = (jnp.full((1, GQA_REP * tq), NEG, F32), jnp.zeros((V_ROWS, GQA_REP * tq), F32))
    sts = lax.fori_loop(0, n_far, far_chunk, (init,) * KV_HEADS)

    win_mask = jnp.concatenate([tile_mask(w0), tile_mask(w0 + 1)], axis=0)
    bias_row = pl.multiple_of(jnp.where(i == 0, LANES, 0), LANES)
    sts = attend(sts, 2 * LANES, w0,
                 lambda head: bt_ref[head, pl.ds(bias_row, 2 * LANES), :] + win_mask)
    for g in range(KV_HEADS):
        acc = sts[g][1]
        out_t = acc[0:HEAD_DIM] / acc[HEAD_DIM:HEAD_DIM + 1]
        for pair in range(GQA_REP // 2):
            two = jnp.concatenate(
                [out_t[:, (2 * pair) * tq:(2 * pair + 1) * tq],
                 out_t[:, (2 * pair + 1) * tq:(2 * pair + 2) * tq]], axis=0)
            col = (g * GQA_REP + 2 * pair) * HEAD_DIM
            o_ref[:, col:col + LANES] = two.T.astype(o_ref.dtype)


def _dsa(proj, q_gain, k_gain, bias_tiles, *, tq=128, chunk=4):
    b, s, _ = proj.shape
    n_t = s // LANES
    att_width = ATT_HEADS * HEAD_DIM
    seg = (jnp.arange(att_width)[:, None] // HEAD_DIM == jnp.arange(LANES)[None, :]).astype(BF16)
    q_gain_all = jnp.tile(q_gain, (1, ATT_HEADS))
    return pl.pallas_call(
        functools.partial(_dsa_kernel, tq=tq, chunk=chunk),
        out_shape=jax.ShapeDtypeStruct((b, s, att_width), BF16),
        grid=(b, s // tq),
        in_specs=[
            pl.BlockSpec((None, tq, 1024), lambda bi, i: (bi, i, 0)),
            pl.BlockSpec((None, s, 256), lambda bi, i: (bi, 0, 4)),
            pl.BlockSpec((None, s, 256), lambda bi, i: (bi, 0, 5)),
            pl.BlockSpec((None, tq, 512), lambda bi, i: (bi, i, 3)),
            pl.BlockSpec((None, tq, LANES), lambda bi, i: (bi, i, 16)),
            pl.BlockSpec((None, s, LANES), lambda bi, i: (bi, 0, 16)),
            pl.BlockSpec((1, att_width), lambda bi, i: (0, 0)),
            pl.BlockSpec((1, HEAD_DIM), lambda bi, i: (0, 0)),
            pl.BlockSpec((ATT_HEADS, BIAS_ROWS, LANES), lambda bi, i: (0, 0, 0)),
            pl.BlockSpec((att_width, LANES), lambda bi, i: (0, 0)),
            pl.BlockSpec((LANES, att_width), lambda bi, i: (0, 0)),
        ],
        out_specs=pl.BlockSpec((None, tq, att_width), lambda bi, i: (bi, i, 0)),
        scratch_shapes=[
            pltpu.VMEM((KV_HEADS, s, LANES), BF16),
            pltpu.VMEM((KV_HEADS, n_t, V_ROWS, LANES), BF16),
            pltpu.VMEM((s, LANES), BF16),
            pltpu.VMEM((n_t // chunk, chunk * LANES, tq), I32),
            pltpu.VMEM((s // GROUP_KEYS, 32, 8, tq), I32),
            pltpu.VMEM((n_t // chunk, chunk * LANES, tq), F32),
            pltpu.VMEM((KV_HEADS, chunk * LANES, GQA_REP * tq), F32),
        ],
        compiler_params=pltpu.CompilerParams(
            dimension_semantics=("parallel", "arbitrary"), vmem_limit_bytes=VMEM_LIMIT),
        name="dsa_attention",
    )(proj, proj, proj, proj, proj, proj, q_gain_all, k_gain, bias_tiles, seg, seg.T)


def kernel(x, norm_mix, norm_ffn, ev_w_in, ev_conv_w, ev_w_out, od_w_in, od_q_gain,
           od_k_gain, od_w_out, rel_bias, ffn_w_gate, ffn_w_up, ffn_w_down):
    b, s, d = x.shape
    n_tok = b * s
    bf = lambda w: w.astype(BF16)

    proj0 = _norm_matmul(x.reshape(n_tok, d), norm_mix[0:1], bf(ev_w_in[0]))
    proj0 = proj0.reshape(b, s, -1)
    a_out = _stickbreak(proj0)
    x2 = _mix0_ffn(a_out, proj0, ev_conv_w[0], bf(ev_w_out[0]), x, norm_ffn[0:1],
                   bf(ffn_w_gate[0]), bf(ffn_w_up[0]), bf(ffn_w_down[0])).reshape(n_tok, d)

    w_in1 = jnp.pad(od_w_in[0], ((0, 0), (0, ODD_IN_PAD - od_w_in.shape[2])))
    proj1 = _norm_matmul(x2, norm_mix[1:2], bf(w_in1)).reshape(b, s, ODD_IN_PAD)
    att = _dsa(proj1, od_q_gain[0:1], od_k_gain[0:1], _bias_tiles(rel_bias))
    x4 = _proj_ffn(att.reshape(n_tok, -1), bf(od_w_out[0]), x2, norm_ffn[1:2],
                   bf(ffn_w_gate[1]), bf(ffn_w_up[1]), bf(ffn_w_down[1]))
    return x4.reshape(b, s, d)
```

```python
import functools
import math

import jax
import jax.numpy as jnp
from jax import lax
from jax.experimental import pallas as pl
from jax.experimental.pallas import tpu as pltpu

F32 = jnp.float32
BF16 = jnp.bfloat16
I32 = jnp.int32

D_MODEL = 1024
SEQ = 2048
HEAD_DIM = 64
SB_WIDTH = 512
CONV_CH = 512
ATT_HEADS = 16
KV_HEADS = 4
GQA_REP = 4
IDX_HEADS = 8
TOPK = 256
REL_BUCKETS = 32
REL_MAX_DIST = 128
FFN_HIDDEN = 2816
EPS = 1e-6
ODD_IN_PAD = 2176

LOG2E = 1.4426950408889634
LANES = 128
VMEM_LIMIT = 56 * 1024 * 1024
NEG = -1e30
INT_MIN = -(2 ** 31)
GROUP_KEYS = 32 * 8
V_ROWS = HEAD_DIM + 16
UNDERFLOW_MARGIN = 160.0
NORM_SLACK = 1.03
KEY_NEG_INF = -(2 ** 31) + 0x7FFFFF

NT_DIMS = (((1,), (1,)), ((), ()))


def _rms(x, g):
    ms = jnp.mean(x * x, axis=-1, keepdims=True)
    return x * lax.rsqrt(ms + EPS) * g


def _norm_matmul_kernel(x_ref, g_ref, w_ref, o_ref, *, tn):
    h = _rms(x_ref[...], g_ref[...]).astype(BF16)
    n = w_ref.shape[1]
    for start in range(0, n, tn):
        size = min(tn, n - start)
        o_ref[:, start:start + size] = jnp.dot(
            h, w_ref[:, start:start + size], preferred_element_type=F32)


def _norm_matmul(x, g, w, *, tm=1024, tn=512):
    m, d = x.shape
    n = w.shape[1]
    return pl.pallas_call(
        functools.partial(_norm_matmul_kernel, tn=tn),
        out_shape=jax.ShapeDtypeStruct((m, n), F32),
        grid=(m // tm,),
        in_specs=[
            pl.BlockSpec((tm, d), lambda i: (i, 0)),
            pl.BlockSpec((1, d), lambda i: (0, 0)),
            pl.BlockSpec((d, n), lambda i: (0, 0)),
        ],
        out_specs=pl.BlockSpec((tm, n), lambda i: (i, 0)),
        compiler_params=pltpu.CompilerParams(
            dimension_semantics=("parallel",), vmem_limit_bytes=VMEM_LIMIT),
        name="norm_matmul",
    )(x, g, w)


def _sb_kernel(q_ref, k_ref, v_ref, o_ref, vt_ref, kmax_ref, *, tq, pairs):
    qi = pl.program_id(2)
    n_sub = tq // LANES
    s_len = k_ref.shape[0]
    width = pairs * LANES
    head_of = (lax.broadcasted_iota(I32, (width, LANES), 0) // HEAD_DIM
               == lax.broadcasted_iota(I32, (width, LANES), 1)).astype(BF16)

    def max_sq_norm(x):
        sq = (x * x).astype(BF16)
        return jnp.max(jnp.dot(sq, head_of, preferred_element_type=F32), axis=0, keepdims=True)

    @pl.when(qi == 0)
    def _per_sequence_setup():
        for p in range(pairs):
            for j in range(s_len // LANES):
                vt_ref[p, j] = v_ref[j * LANES:(j + 1) * LANES,
                                     p * LANES:(p + 1) * LANES].T.astype(BF16)
        kmax = jnp.zeros((1, LANES), F32)
        for r in range(0, s_len, tq):
            kmax = jnp.maximum(kmax, max_sq_norm(k_ref[r:r + tq, :]))
        kmax_ref[...] = kmax

    lane = lax.broadcasted_iota(I32, (tq, LANES), 1)
    krow = lax.broadcasted_iota(I32, (tq, tq), 0)
    qcol = lax.broadcasted_iota(I32, (tq, tq), 1)
    from_here = (qcol >= krow).astype(BF16)
    causal = krow < qcol
    q_heads = []
    for p in range(pairs):
        q = q_ref[:, p * LANES:(p + 1) * LANES] * (HEAD_DIM ** -0.5 * LOG2E)
        for h in range(2):
            in_head = (lane >= HEAD_DIM * h) & (lane < HEAD_DIM * (h + 1))
            q_heads.append(jnp.where(in_head, q, 0.0).astype(BF16))

    def block(kb, st, masked):
        start = pl.multiple_of(kb * tq, tq)
        heads = range(2 * pairs)
        kblks = [k_ref[pl.ds(start, tq), p * LANES:(p + 1) * LANES].astype(BF16)
                 for p in range(pairs)]
        v_ts = [jnp.concatenate([vt_ref[p, kb * n_sub + u] for u in range(n_sub)], axis=1)
                for p in range(pairs)]
        zs = [lax.dot_general(kblks[idx // 2], q_heads[idx], NT_DIMS, preferred_element_type=F32)
              for idx in heads]
        softplus, his, los = [], [], []
        for z in zs:
            sp = jnp.maximum(z, 0.0) + jnp.log2(1.0 + jnp.exp2(-jnp.abs(z)))
            if masked:
                sp = jnp.where(causal, sp, 0.0)
            hi = sp.astype(BF16)
            softplus.append(sp)
            his.append(hi)
            los.append((sp - hi.astype(F32)).astype(BF16))
        since = [jnp.dot(from_here, his[idx], preferred_element_type=F32)
                 + jnp.dot(from_here, los[idx], preferred_element_type=F32) for idx in heads]
        ws = []
        for idx in heads:
            w = jnp.exp2(zs[idx] - since[idx] - st[2 * idx])
            if masked:
                w = jnp.where(causal, w, 0.0)
            ws.append(w.astype(BF16))
        new = []
        for idx in heads:
            h = idx % 2
            acc = st[2 * idx + 1] + jnp.dot(v_ts[idx // 2][h * HEAD_DIM:(h + 1) * HEAD_DIM, :],
                                            ws[idx], preferred_element_type=F32)
            carry = st[2 * idx] + jnp.sum(softplus[idx], axis=0, keepdims=True)
            new += [carry, acc]
        return tuple(new)

    z_bound = jnp.sqrt(max_sq_norm(q_ref[...]) * kmax_ref[...]) * (HEAD_DIM ** -0.5 * LOG2E * NORM_SLACK)
    z_bounds = [jnp.broadcast_to(z_bound[:, idx:idx + 1], (1, tq)) for idx in range(2 * pairs)]

    def all_underflow(st):
        slack = st[0] - z_bounds[0]
        for idx in range(1, 2 * pairs):
            slack = jnp.minimum(slack, st[2 * idx] - z_bounds[idx])
        return jnp.min(slack) > UNDERFLOW_MARGIN

    def walk(state):
        n, _, st = state
        st = block(qi - 1 - n, st, False)
        return n + 1, all_underflow(st), st

    st = (jnp.zeros((1, tq), F32), jnp.zeros((HEAD_DIM, tq), F32)) * (2 * pairs)
    st = block(qi, st, True)
    _, _, st = lax.while_loop(lambda state: (state[0] < qi) & jnp.logical_not(state[1]), walk,
                              (jnp.int32(0), all_underflow(st), st))
    for p in range(pairs):
        out_t = jnp.concatenate([st[4 * p + 1], st[4 * p + 3]], axis=0)
        for u in range(n_sub):
            o_ref[u * LANES:(u + 1) * LANES, p * LANES:(p + 1) * LANES] = (
                out_t[:, u * LANES:(u + 1) * LANES].T.astype(o_ref.dtype))


def _stickbreak(proj, *, tq=256, pairs=4):
    b, s, _ = proj.shape
    width = pairs * LANES
    n_steps = SB_WIDTH // width
    return pl.pallas_call(
        functools.partial(_sb_kernel, tq=tq, pairs=pairs),
        out_shape=jax.ShapeDtypeStruct((b, s, SB_WIDTH), BF16),
        grid=(b, n_steps, s // tq),
        in_specs=[
            pl.BlockSpec((None, tq, width), lambda bi, hp, qi: (bi, qi, hp)),
            pl.BlockSpec((None, s, width), lambda bi, hp, qi: (bi, 0, n_steps + hp)),
            pl.BlockSpec((None, s, width), lambda bi, hp, qi: (bi, 0, 2 * n_steps + hp)),
        ],
        out_specs=pl.BlockSpec((None, tq, width), lambda bi, hp, qi: (bi, qi, hp)),
        scratch_shapes=[pltpu.VMEM((pairs, s // LANES, LANES, LANES), BF16),
                        pltpu.VMEM((1, LANES), F32)],
        compiler_params=pltpu.CompilerParams(
            dimension_semantics=("parallel", "parallel", "arbitrary"),
            vmem_limit_bytes=VMEM_LIMIT),
        name="stickbreak",
    )(proj, proj, proj)


def _ffn(x, g_ref, wg_ref, wu_ref, wd_ref, act_ref, th):
    h = _rms(x, g_ref[...]).astype(BF16)
    for start in range(0, FFN_HIDDEN, th):
        stop = min(start + th, FFN_HIDDEN)
        hg = jnp.dot(h, wg_ref[:, start:stop], preferred_element_type=F32)
        hu = jnp.dot(h, wu_ref[:, start:stop], preferred_element_type=F32)
        sig = 1.0 / (1.0 + jnp.exp(-hg))
        act_ref[:, start:stop] = (hg * sig * hu).astype(BF16)
    return x + jnp.dot(act_ref[...], wd_ref[...], preferred_element_type=F32)


def _ffn_specs(d, index_map):
    resident = pl.Buffered(1)
    return [
        pl.BlockSpec((1, d), index_map),
        pl.BlockSpec((d, FFN_HIDDEN), index_map, pipeline_mode=resident),
        pl.BlockSpec((d, FFN_HIDDEN), index_map, pipeline_mode=resident),
        pl.BlockSpec((FFN_HIDDEN, d), index_map, pipeline_mode=resident),
    ]


def _mix0_ffn_kernel(a_ref, bg_ref, cg_ref, u_ref, cgh_ref, uh_ref, cw_ref, w_ref, x_ref,
                     g_ref, wg_ref, wu_ref, wd_ref, o_ref, act_ref, *, th):
    i = pl.program_id(1)
    g = cg_ref[...] * u_ref[...]
    gh = cgh_ref[...] * uh_ref[...]
    gh = jnp.where(i == 0, 0.0, gh)
    row = lax.broadcasted_iota(I32, g.shape, 0)
    g1 = jnp.where(row == 0, gh[7:8, :], pltpu.roll(g, 1, axis=0))
    g2 = pltpu.roll(g, 2, axis=0)
    g2 = jnp.where(row == 0, gh[6:7, :], jnp.where(row == 1, gh[7:8, :], g2))
    cw = cw_ref[...]
    y = bg_ref[...] * (cw[0:1, :] * g2 + cw[1:2, :] * g1 + cw[2:3, :] * g)
    acc = jnp.dot(a_ref[...], w_ref[0:SB_WIDTH, :], preferred_element_type=F32)
    acc = acc + jnp.dot(y.astype(BF16), w_ref[SB_WIDTH:, :], preferred_element_type=F32)
    o_ref[...] = _ffn(x_ref[...] + acc, g_ref, wg_ref, wu_ref, wd_ref, act_ref, th)


def _mix0_ffn(a_out, proj, conv_w, w_out, x, g, wg, wu, wd, *, tm=512, th=512):
    b, s, d = x.shape
    halo = lambda col: (lambda bi, i: (bi, jnp.maximum(i * (tm // 8) - 1, 0), col))
    return pl.pallas_call(
        functools.partial(_mix0_ffn_kernel, th=th),
        out_shape=jax.ShapeDtypeStruct((b, s, d), F32),
        grid=(b, s // tm),
        in_specs=[
            pl.BlockSpec((None, tm, SB_WIDTH), lambda bi, i: (bi, i, 0)),
            pl.BlockSpec((None, tm, CONV_CH), lambda bi, i: (bi, i, 3)),
            pl.BlockSpec((None, tm, CONV_CH), lambda bi, i: (bi, i, 4)),
            pl.BlockSpec((None, tm, CONV_CH), lambda bi, i: (bi, i, 5)),
            pl.BlockSpec((None, 8, CONV_CH), halo(4)),
            pl.BlockSpec((None, 8, CONV_CH), halo(5)),
            pl.BlockSpec((3, CONV_CH), lambda bi, i: (0, 0)),
            pl.BlockSpec((SB_WIDTH + CONV_CH, d), lambda bi, i: (0, 0)),
            pl.BlockSpec((None, tm, d), lambda bi, i: (bi, i, 0)),
        ] + _ffn_specs(d, lambda bi, i: (0, 0)),
        out_specs=pl.BlockSpec((None, tm, d), lambda bi, i: (bi, i, 0)),
        scratch_shapes=[pltpu.VMEM((tm, FFN_HIDDEN), BF16)],
        compiler_params=pltpu.CompilerParams(
            dimension_semantics=("parallel", "arbitrary"), vmem_limit_bytes=VMEM_LIMIT),
        name="mix0_ffn",
    )(a_out, proj, proj, proj, proj, proj, conv_w, w_out, x, g, wg, wu, wd)


def _proj_ffn_kernel(a_ref, w_ref, x_ref, g_ref, wg_ref, wu_ref, wd_ref, o_ref, act_ref, *, th):
    x1 = x_ref[...] + jnp.dot(a_ref[...], w_ref[...], preferred_element_type=F32)
    o_ref[...] = _ffn(x1, g_ref, wg_ref, wu_ref, wd_ref, act_ref, th)


def _proj_ffn(a, w, x, g, wg, wu, wd, *, tm=512, th=512):
    m, k = a.shape
    d = w.shape[1]
    return pl.pallas_call(
        functools.partial(_proj_ffn_kernel, th=th),
        out_shape=jax.ShapeDtypeStruct((m, d), F32),
        grid=(m // tm,),
        in_specs=[
            pl.BlockSpec((tm, k), lambda i: (i, 0)),
            pl.BlockSpec((k, d), lambda i: (0, 0)),
            pl.BlockSpec((tm, d), lambda i: (i, 0)),
        ] + _ffn_specs(d, lambda i: (0, 0)),
        out_specs=pl.BlockSpec((tm, d), lambda i: (i, 0)),
        scratch_shapes=[pltpu.VMEM((tm, FFN_HIDDEN), BF16)],
        compiler_params=pltpu.CompilerParams(
            dimension_semantics=("parallel",), vmem_limit_bytes=VMEM_LIMIT),
        name="proj_ffn",
    )(a, w, x, g, wg, wu, wd)


BIAS_ROWS = 3 * LANES


def _bias_tile_kernel(rb_ref, o_ref):
    c = lax.broadcasted_iota(I32, (BIAS_ROWS, LANES), 0)
    t = lax.broadcasted_iota(I32, (BIAS_ROWS, LANES), 1)
    dist = jnp.maximum(LANES + t - c, 0)
    exact = REL_BUCKETS // 2
    d_f = jnp.maximum(dist, 1).astype(F32)
    large = exact + (jnp.log(d_f / exact) / math.log(REL_MAX_DIST / exact)
                     * (REL_BUCKETS - exact)).astype(I32)
    large = jnp.minimum(large, REL_BUCKETS - 1)
    bucket = jnp.where(dist < exact, dist, large)

    def one_head(h, _):
        out = jnp.zeros((BIAS_ROWS, LANES), F32)
        for b in range(REL_BUCKETS):
            out = jnp.where(bucket == b, rb_ref[b, h], out)
        o_ref[h] = (out - rb_ref[REL_BUCKETS - 1, h]) * LOG2E
        return 0

    lax.fori_loop(0, ATT_HEADS, one_head, 0)


def _bias_tiles(rel_bias):
    return pl.pallas_call(
        _bias_tile_kernel,
        out_shape=jax.ShapeDtypeStruct((ATT_HEADS, BIAS_ROWS, LANES), F32),
        in_specs=[pl.BlockSpec(memory_space=pltpu.SMEM)],
        out_specs=pl.BlockSpec(memory_space=pltpu.VMEM),
        name="rel_bias_tiles",
    )(rel_bias)


def _dsa_kernel(q_ref, k_ref, v_ref, qi_ref, wq_ref, kik_ref, qg_ref, kg_ref, bt_ref, seg_ref,
                segt_ref, o_ref, kn_ref, vt_ref, kib_ref, key_ref, planes_ref, mfar_ref, s_ref, qs_ref, *, tq, chunk):
    i = pl.program_id(1)
    s_len = k_ref.shape[0]
    n_chunk = (i + chunk) // chunk
    n_far = (i - 1 + chunk - 1) // chunk
    w0 = jnp.maximum(i - 1, 0)

    @pl.when(i == 0)
    def _prepare_keys():
        kg = kg_ref[...]
        for g in range(KV_HEADS):
            kn = _rms(k_ref[:, g * HEAD_DIM:(g + 1) * HEAD_DIM], kg).astype(BF16)
            kn_ref[g] = jnp.concatenate([kn, kn], axis=1)
        for j in range(s_len // LANES):
            for pair in range(KV_HEADS // 2):
                v_t = v_ref[j * LANES:(j + 1) * LANES, pair * LANES:(pair + 1) * LANES].T
                vt_ref[2 * pair, j, 0:HEAD_DIM] = v_t[0:HEAD_DIM].astype(BF16)
                vt_ref[2 * pair + 1, j, 0:HEAD_DIM] = v_t[HEAD_DIM:].astype(BF16)
        vt_ref[:, :, HEAD_DIM:V_ROWS] = jnp.ones((KV_HEADS, s_len // LANES, V_ROWS - HEAD_DIM, LANES),
                                                 BF16)
        ki = kik_ref[:, 0:HEAD_DIM].astype(BF16)
        kib_ref[...] = jnp.concatenate([ki, ki], axis=1)
        planes_ref[...] = jnp.zeros(planes_ref.shape, I32)

    ck = chunk * LANES
    krow = lax.broadcasted_iota(I32, (ck, tq), 0)
    tpos = i * tq + lax.broadcasted_iota(I32, (ck, tq), 1)
    low_half = lax.broadcasted_iota(I32, (tq, LANES), 1) < HEAD_DIM

    def split_pair(x):
        return [jnp.where(low_half, x, 0.0).astype(BF16), jnp.where(low_half, 0.0, x).astype(BF16)]

    q = q_ref[...]
    sq = q * q
    sq_hi = sq.astype(BF16)
    sq_lo = (sq - sq_hi.astype(F32)).astype(BF16)
    seg = seg_ref[...]
    ssum = (jnp.dot(sq_hi, seg, preferred_element_type=F32)
            + jnp.dot(sq_lo, seg, preferred_element_type=F32))
    rinv = lax.rsqrt(ssum * (1.0 / HEAD_DIM) + EPS)
    pieces, rest = [], rinv
    for _ in range(2):
        piece = rest.astype(BF16)
        pieces.append(piece)
        rest = rest - piece.astype(F32)
    segt = segt_ref[...]
    rinv_b = sum(jnp.dot(piece, segt, preferred_element_type=F32) for piece in pieces)
    qn = q * rinv_b * (qg_ref[...] * (HEAD_DIM ** -0.5 * LOG2E))
    for g in range(KV_HEADS):
        rows = []
        for pair in range(2 * g, 2 * g + 2):
            rows += split_pair(qn[:, pair * LANES:(pair + 1) * LANES])
        qs_ref[g] = jnp.concatenate(rows, axis=0)

    w_t = wq_ref[...].T
    qi_pairs = [jnp.concatenate(split_pair(qi_ref[:, pair * LANES:(pair + 1) * LANES] * 0.125), axis=0)
                for pair in range(IDX_HEADS // 2)]
    w_heads = [w_t[HEAD_DIM + h:HEAD_DIM + h + 1, :] * (IDX_HEADS ** -0.5)
               for h in range(IDX_HEADS)]

    def bit_planes(c):
        for half in range(ck // GROUP_KEYS):
            words = [key_ref[c, pl.ds(half * GROUP_KEYS + 8 * r, 8), :] for r in range(32)]
            for shift, keep in ((16, 0x0000FFFF), (8, 0x00FF00FF), (4, 0x0F0F0F0F),
                                (2, 0x33333333), (1, 0x55555555)):
                amount = jnp.full((8, tq), shift, I32)
                for r in range(32):
                    if r & shift == 0:
                        a, b = words[r], words[r + shift]
                        t = (lax.shift_right_logical(a, amount) ^ b) & keep
                        words[r] = a ^ lax.shift_left(t, amount)
                        words[r + shift] = b ^ t
            words[31] = ~words[31]
            for b in range(32):
                planes_ref[c * (ck // GROUP_KEYS) + half, b] = words[b]

    def score_chunk(c):
        kt = kib_ref[pl.ds(pl.multiple_of(c * ck, ck), ck), :]
        dots = [lax.dot_general(kt, qi_pair, NT_DIMS, preferred_element_type=F32)
                for qi_pair in qi_pairs]
        sc = jnp.zeros((ck, tq), F32)
        for h in range(IDX_HEADS):
            d = dots[h // 2][:, (h % 2) * tq:(h % 2 + 1) * tq]
            sc = sc + w_heads[h] * jnp.maximum(d, 0.0)
        sc = jnp.where(sc == 0.0, 0.0, sc)
        sc = jnp.where(c * ck + krow <= tpos, sc, -jnp.inf)
        bits = pltpu.bitcast(sc, I32)
        key_ref[c] = bits ^ ((bits >> 31) & 0x7FFFFFFF)

    score_chunk(0)

    def scan(c, _):
        bit_planes(c - 1)
        score_chunk(c)
        return 0

    lax.fori_loop(1, n_chunk, scan, 0)

    def count_keys(preds):
        def body(c, cs):
            key = key_ref[c]
            return tuple(
                cnt + jnp.sum(pred(c, key).astype(I32).reshape(ck // 8, 8, tq), axis=0)
                for cnt, pred in zip(cs, preds))
        cs = lax.fori_loop(0, n_chunk, body,
                           tuple(jnp.zeros((8, tq), I32) for _ in preds))
        return [jnp.sum(cnt, axis=0, keepdims=True) for cnt in cs]

    n_groups = s_len // GROUP_KEYS

    def sliced_search():
        bit_planes(n_chunk - 1)
        live = n_chunk * (ck // GROUP_KEYS)

        def two_bits(step, state):
            eqs, n_gt, thr_u = state
            b0 = 30 - 2 * step
            s11, s10, s01, s00 = [], [], [], []
            for g in range(n_groups):
                with_hi = eqs[g] & planes_ref[g, b0 + 1]
                without_hi = eqs[g] ^ with_hi
                low = planes_ref[g, b0]
                s11.append(with_hi & low)
                s10.append(with_hi ^ s11[g])
                s01.append(without_hi & low)
                s00.append(without_hi ^ s01[g])

            def count(sets):
                total = lax.population_count(sets[0])
                for g in range(1, n_groups):
                    total = total + lax.population_count(sets[g])
                return jnp.sum(total, axis=0, keepdims=True)

            up11 = n_gt + count(s11)
            up10 = up11 + count(s10)
            up01 = up10 + count(s01)
            in11 = up11 >= TOPK
            in10 = jnp.logical_not(in11) & (up10 >= TOPK)
            in01 = jnp.logical_not(in11 | in10) & (up01 >= TOPK)
            n_gt = jnp.where(in11, n_gt, jnp.where(in10, up11, jnp.where(in01, up10, up01)))
            bits = jnp.where(in11, 3, jnp.where(in10, 2, jnp.where(in01, 1, 0)))
            thr_u = thr_u | jnp.left_shift(bits, b0)
            m11, m10, m01 = (jnp.broadcast_to(m, (8, tq)) for m in (in11, in10, in01))
            eqs = tuple(jnp.where(m11, s11[g], jnp.where(m10, s10[g], jnp.where(m01, s01[g], s00[g])))
                        for g in range(n_groups))
            return eqs, n_gt, thr_u

        eqs = tuple(jnp.full((8, tq), jnp.where(g < live, -1, 0), I32) for g in range(n_groups))
        zero = jnp.zeros((1, tq), I32)
        eqs, n_gt, thr_u = lax.fori_loop(0, 16, two_bits, (eqs, zero, zero))
        n_eq = lax.population_count(eqs[0])
        for g in range(1, n_groups):
            n_eq = n_eq + lax.population_count(eqs[g])
        return thr_u ^ INT_MIN, n_gt, jnp.sum(n_eq, axis=0, keepdims=True)

    thr, n_gt, n_eq = lax.cond(
        (i + 1) * tq <= TOPK,
        lambda: (jnp.full((1, tq), KEY_NEG_INF, I32), jnp.zeros((1, tq), I32), jnp.zeros((1, tq), I32)),
        sliced_search)
    thr_b = jnp.broadcast_to(thr, (ck, tq))
    need = TOPK - n_gt
    tied = (n_eq > need) & (thr != KEY_NEG_INF)
    any_tied = jnp.max(tied.astype(I32)) > 0

    def tie_bit(b, cut):
        cand = cut + lax.shift_left(jnp.int32(1), 10 - b)
        cand_b = jnp.broadcast_to(cand, (ck, tq))
        (cnt,) = count_keys([lambda c, key: (key == thr_b) & (c * ck + krow < cand_b)])
        return jnp.where(cnt < need, cand, cut)

    cut = lax.cond(any_tied,
                   lambda: lax.fori_loop(0, 11, tie_bit, jnp.zeros((1, tq), I32)),
                   lambda: jnp.full((1, tq), s_len, I32))

    def selection_mask(key, spos, tpos):
        thr_k = jnp.broadcast_to(thr, key.shape)
        cut_k = jnp.broadcast_to(cut, key.shape)
        sel = (key > thr_k) | ((key == thr_k) & (spos <= cut_k))
        return jnp.where(sel & (spos <= tpos), 0.0, NEG)

    def mask_chunk(c, _):
        spos = c * ck + krow
        madd = selection_mask(key_ref[c], spos, tpos)
        mfar_ref[c] = jnp.where(spos < (i - 1) * LANES, madd, NEG)
        return 0

    lax.fori_loop(0, n_far, mask_chunk, 0)

    def tile_mask(tile):
        c, u = tile // chunk, tile % chunk
        key = key_ref[c, pl.ds(pl.multiple_of(u * LANES, LANES), LANES), :]
        krow_t = lax.broadcasted_iota(I32, (LANES, tq), 0)
        tpos_t = i * tq + lax.broadcasted_iota(I32, (LANES, tq), 1)
        return selection_mask(key, tile * LANES + krow_t, tpos_t)

    qs = [qs_ref[g] for g in range(KV_HEADS)]

    groups = range(KV_HEADS)

    def attend(sts, n_keys, first_tile, add_fn):
        key_start = pl.multiple_of(first_tile * LANES, LANES)
        for g in groups:
            kt = kn_ref[g, pl.ds(key_start, n_keys), :]
            add = jnp.concatenate([add_fn(g * GQA_REP + r) for r in range(GQA_REP)], axis=1)
            s_ref[g, 0:n_keys, :] = lax.dot_general(kt, qs[g], NT_DIMS,
                                                    preferred_element_type=F32) + add
        ps, stats = [], []
        for g in groups:
            m, _ = sts[g]
            m_new = jnp.maximum(m, jnp.max(s_ref[g, 0:n_keys, :], axis=0, keepdims=True))
            stats.append((m_new, jnp.exp2(m - m_new)))
            ps.append(jnp.exp2(s_ref[g, 0:n_keys, :] - m_new).astype(BF16))
        new = []
        for g in groups:
            v_t = jnp.concatenate([vt_ref[g, first_tile + u] for u in range(n_keys // LANES)],
                                  axis=1)
            new.append((stats[g][0], stats[g][1] * sts[g][1]
                        + jnp.dot(v_t, ps[g], preferred_element_type=F32)))
        return tuple(new)

    def far_chunk(c, sts):
        madd = mfar_ref[c]
        return attend(sts, ck, c * chunk, lambda head: madd)

    init = (jnp.full((1, GQA_REP * tq), NEG, F32), jnp.zeros((V_ROWS, GQA_REP * tq), F32))
    sts = lax.fori_loop(0, n_far, far_chunk, (init,) * KV_HEADS)

    win_mask = jnp.concatenate([tile_mask(w0), tile_mask(w0 + 1)], axis=0)
    bias_row = pl.multiple_of(jnp.where(i == 0, LANES, 0), LANES)
    sts = attend(sts, 2 * LANES, w0,
                 lambda head: bt_ref[head, pl.ds(bias_row, 2 * LANES), :] + win_mask)
    for g in range(KV_HEADS):
        acc = sts[g][1]
        out_t = acc[0:HEAD_DIM] / acc[HEAD_DIM:HEAD_DIM + 1]
        for pair in range(GQA_REP // 2):
            two = jnp.concatenate(
                [out_t[:, (2 * pair) * tq:(2 * pair + 1) * tq],
                 out_t[:, (2 * pair + 1) * tq:(2 * pair + 2) * tq]], axis=0)
            col = (g * GQA_REP + 2 * pair) * HEAD_DIM
            o_ref[:, col:col + LANES] = two.T.astype(o_ref.dtype)


def _dsa(proj, q_gain, k_gain, bias_tiles, *, tq=128, chunk=4):
    b, s, _ = proj.shape
    n_t = s // LANES
    att_width = ATT_HEADS * HEAD_DIM
    seg = (jnp.arange(att_width)[:, None] // HEAD_DIM == jnp.arange(LANES)[None, :]).astype(BF16)
    q_gain_all = jnp.tile(q_gain, (1, ATT_HEADS))
    return pl.pallas_call(
        functools.partial(_dsa_kernel, tq=tq, chunk=chunk),
        out_shape=jax.ShapeDtypeStruct((b, s, att_width), BF16),
        grid=(b, s // tq),
        in_specs=[
            pl.BlockSpec((None, tq, 1024), lambda bi, i: (bi, i, 0)),
            pl.BlockSpec((None, s, 256), lambda bi, i: (bi, 0, 4)),
            pl.BlockSpec((None, s, 256), lambda bi, i: (bi, 0, 5)),
            pl.BlockSpec((None, tq, 512), lambda bi, i: (bi, i, 3)),
            pl.BlockSpec((None, tq, LANES), lambda bi, i: (bi, i, 16)),
            pl.BlockSpec((None, s, LANES), lambda bi, i: (bi, 0, 16)),
            pl.BlockSpec((1, att_width), lambda bi, i: (0, 0)),
            pl.BlockSpec((1, HEAD_DIM), lambda bi, i: (0, 0)),
            pl.BlockSpec((ATT_HEADS, BIAS_ROWS, LANES), lambda bi, i: (0, 0, 0)),
            pl.BlockSpec((att_width, LANES), lambda bi, i: (0, 0)),
            pl.BlockSpec((LANES, att_width), lambda bi, i: (0, 0)),
        ],
        out_specs=pl.BlockSpec((None, tq, att_width), lambda bi, i: (bi, i, 0)),
        scratch_shapes=[
            pltpu.VMEM((KV_HEADS, s, LANES), BF16),
            pltpu.VMEM((KV_HEADS, n_t, V_ROWS, LANES), BF16),
            pltpu.VMEM((s, LANES), BF16),
            pltpu.VMEM((n_t // chunk, chunk * LANES, tq), I32),
            pltpu.VMEM((s // GROUP_KEYS, 32, 8, tq), I32),
            pltpu.VMEM((n_t // chunk, chunk * LANES, tq), F32),
            pltpu.VMEM((KV_HEADS, chunk * LANES, GQA_REP * tq), F32),
            pltpu.VMEM((KV_HEADS, GQA_REP * tq, LANES), BF16),
        ],
        compiler_params=pltpu.CompilerParams(
            dimension_semantics=("parallel", "arbitrary"), vmem_limit_bytes=VMEM_LIMIT),
        name="dsa_attention",
    )(proj, proj, proj, proj, proj, proj, q_gain_all, k_gain, bias_tiles, seg, seg.T)


def kernel(x, norm_mix, norm_ffn, ev_w_in, ev_conv_w, ev_w_out, od_w_in, od_q_gain,
           od_k_gain, od_w_out, rel_bias, ffn_w_gate, ffn_w_up, ffn_w_down):
    b, s, d = x.shape
    n_tok = b * s
    bf = lambda w: w.astype(BF16)

    proj0 = _norm_matmul(x.reshape(n_tok, d), norm_mix[0:1], bf(ev_w_in[0]))
    proj0 = proj0.reshape(b, s, -1)
    a_out = _stickbreak(proj0)
    x2 = _mix0_ffn(a_out, proj0, ev_conv_w[0], bf(ev_w_out[0]), x, norm_ffn[0:1],
                   bf(ffn_w_gate[0]), bf(ffn_w_up[0]), bf(ffn_w_down[0])).reshape(n_tok, d)

    w_in1 = jnp.pad(od_w_in[0], ((0, 0), (0, ODD_IN_PAD - od_w_in.shape[2])))
    proj1 = _norm_matmul(x2, norm_mix[1:2], bf(w_in1)).reshape(b, s, ODD_IN_PAD)
    att = _dsa(proj1, od_q_gain[0:1], od_k_gain[0:1], _bias_tiles(rel_bias))
    x4 = _proj_ffn(att.reshape(n_tok, -1), bf(od_w_out[0]), x2, norm_ffn[1:2],
                   bf(ffn_w_gate[1]), bf(ffn_w_up[1]), bf(ffn_w_down[1]))
    return x4.reshape(b, s, d)
```

```python
import functools
import math

import jax
import jax.numpy as jnp
from jax import lax
from jax.experimental import pallas as pl
from jax.experimental.pallas import tpu as pltpu

F32 = jnp.float32
BF16 = jnp.bfloat16
I32 = jnp.int32

D_MODEL = 1024
SEQ = 2048
HEAD_DIM = 64
SB_WIDTH = 512
CONV_CH = 512
ATT_HEADS = 16
KV_HEADS = 4
GQA_REP = 4
IDX_HEADS = 8
TOPK = 256
REL_BUCKETS = 32
REL_MAX_DIST = 128
FFN_HIDDEN = 2816
EPS = 1e-6
ODD_IN_PAD = 2176

LOG2E = 1.4426950408889634
LANES = 128
VMEM_LIMIT = 56 * 1024 * 1024
NEG = -1e30
INT_MIN = -(2 ** 31)
GROUP_KEYS = 32 * 8
V_ROWS = HEAD_DIM + 16
UNDERFLOW_MARGIN = 160.0
NORM_SLACK = 1.03
KEY_NEG_INF = -(2 ** 31) + 0x7FFFFF

NT_DIMS = (((1,), (1,)), ((), ()))


def _rms(x, g):
    ms = jnp.mean(x * x, axis=-1, keepdims=True)
    return x * lax.rsqrt(ms + EPS) * g


def _norm_matmul_kernel(x_ref, g_ref, w_ref, o_ref, *, tn):
    h = _rms(x_ref[...], g_ref[...]).astype(BF16)
    n = w_ref.shape[1]
    for start in range(0, n, tn):
        size = min(tn, n - start)
        o_ref[:, start:start + size] = jnp.dot(
            h, w_ref[:, start:start + size], preferred_element_type=F32)


def _norm_matmul(x, g, w, *, tm=1024, tn=512):
    m, d = x.shape
    n = w.shape[1]
    return pl.pallas_call(
        functools.partial(_norm_matmul_kernel, tn=tn),
        out_shape=jax.ShapeDtypeStruct((m, n), F32),
        grid=(m // tm,),
        in_specs=[
            pl.BlockSpec((tm, d), lambda i: (i, 0)),
            pl.BlockSpec((1, d), lambda i: (0, 0)),
            pl.BlockSpec((d, n), lambda i: (0, 0)),
        ],
        out_specs=pl.BlockSpec((tm, n), lambda i: (i, 0)),
        compiler_params=pltpu.CompilerParams(
            dimension_semantics=("parallel",), vmem_limit_bytes=VMEM_LIMIT),
        name="norm_matmul",
    )(x, g, w)


def _sb_kernel(q_ref, k_ref, v_ref, o_ref, vt_ref, kmax_ref, *, tq, pairs):
    qi = pl.program_id(2)
    n_sub = tq // LANES
    s_len = k_ref.shape[0]
    width = pairs * LANES
    head_of = (lax.broadcasted_iota(I32, (width, LANES), 0) // HEAD_DIM
               == lax.broadcasted_iota(I32, (width, LANES), 1)).astype(BF16)

    def max_sq_norm(x):
        sq = (x * x).astype(BF16)
        return jnp.max(jnp.dot(sq, head_of, preferred_element_type=F32), axis=0, keepdims=True)

    @pl.when(qi == 0)
    def _per_sequence_setup():
        for p in range(pairs):
            for j in range(s_len // LANES):
                vt_ref[p, j] = v_ref[j * LANES:(j + 1) * LANES,
                                     p * LANES:(p + 1) * LANES].T.astype(BF16)
        kmax = jnp.zeros((1, LANES), F32)
        for r in range(0, s_len, tq):
            kmax = jnp.maximum(kmax, max_sq_norm(k_ref[r:r + tq, :]))
        kmax_ref[...] = kmax

    lane = lax.broadcasted_iota(I32, (tq, LANES), 1)
    krow = lax.broadcasted_iota(I32, (tq, tq), 0)
    qcol = lax.broadcasted_iota(I32, (tq, tq), 1)
    from_here = (qcol >= krow).astype(BF16)
    causal = krow < qcol
    q_heads = []
    for p in range(pairs):
        q = q_ref[:, p * LANES:(p + 1) * LANES] * (HEAD_DIM ** -0.5 * LOG2E)
        for h in range(2):
            in_head = (lane >= HEAD_DIM * h) & (lane < HEAD_DIM * (h + 1))
            q_heads.append(jnp.where(in_head, q, 0.0).astype(BF16))

    def block(kb, st, masked):
        start = pl.multiple_of(kb * tq, tq)
        heads = range(2 * pairs)
        kblks = [k_ref[pl.ds(start, tq), p * LANES:(p + 1) * LANES].astype(BF16)
                 for p in range(pairs)]
        v_ts = [jnp.concatenate([vt_ref[p, kb * n_sub + u] for u in range(n_sub)], axis=1)
                for p in range(pairs)]
        zs = [lax.dot_general(kblks[idx // 2], q_heads[idx], NT_DIMS, preferred_element_type=F32)
              for idx in heads]
        softplus, his, los = [], [], []
        for z in zs:
            sp = jnp.maximum(z, 0.0) + jnp.log2(1.0 + jnp.exp2(-jnp.abs(z)))
            if masked:
                sp = jnp.where(causal, sp, 0.0)
            hi = sp.astype(BF16)
            softplus.append(sp)
            his.append(hi)
            los.append((sp - hi.astype(F32)).astype(BF16))
        since = [jnp.dot(from_here, his[idx], preferred_element_type=F32)
                 + jnp.dot(from_here, los[idx], preferred_element_type=F32) for idx in heads]
        ws = []
        for idx in heads:
            w = jnp.exp2(zs[idx] - since[idx] - st[2 * idx])
            if masked:
                w = jnp.where(causal, w, 0.0)
            ws.append(w.astype(BF16))
        new = []
        for idx in heads:
            h = idx % 2
            acc = st[2 * idx + 1] + jnp.dot(v_ts[idx // 2][h * HEAD_DIM:(h + 1) * HEAD_DIM, :],
                                            ws[idx], preferred_element_type=F32)
            carry = st[2 * idx] + jnp.sum(softplus[idx], axis=0, keepdims=True)
            new += [carry, acc]
        return tuple(new)

    z_bound = jnp.sqrt(max_sq_norm(q_ref[...]) * kmax_ref[...]) * (HEAD_DIM ** -0.5 * LOG2E * NORM_SLACK)
    z_bounds = [jnp.broadcast_to(z_bound[:, idx:idx + 1], (1, tq)) for idx in range(2 * pairs)]

    def all_underflow(st):
        slack = st[0] - z_bounds[0]
        for idx in range(1, 2 * pairs):
            slack = jnp.minimum(slack, st[2 * idx] - z_bounds[idx])
        return jnp.min(slack) > UNDERFLOW_MARGIN

    def walk(state):
        n, _, st = state
        st = block(qi - 1 - n, st, False)
        return n + 1, all_underflow(st), st

    st = (jnp.zeros((1, tq), F32), jnp.zeros((HEAD_DIM, tq), F32)) * (2 * pairs)
    st = block(qi, st, True)
    _, _, st = lax.while_loop(lambda state: (state[0] < qi) & jnp.logical_not(state[1]), walk,
                              (jnp.int32(0), all_underflow(st), st))
    for p in range(pairs):
        out_t = jnp.concatenate([st[4 * p + 1], st[4 * p + 3]], axis=0)
        o_ref[p * LANES:(p + 1) * LANES, :] = out_t.astype(o_ref.dtype)


def _stickbreak(proj, *, tq=256, pairs=4):
    b, s, _ = proj.shape
    width = pairs * LANES
    n_steps = SB_WIDTH // width
    return pl.pallas_call(
        functools.partial(_sb_kernel, tq=tq, pairs=pairs),
        out_shape=jax.ShapeDtypeStruct((b, SB_WIDTH, s), BF16),
        grid=(b, n_steps, s // tq),
        in_specs=[
            pl.BlockSpec((None, tq, width), lambda bi, hp, qi: (bi, qi, hp)),
            pl.BlockSpec((None, s, width), lambda bi, hp, qi: (bi, 0, n_steps + hp)),
            pl.BlockSpec((None, s, width), lambda bi, hp, qi: (bi, 0, 2 * n_steps + hp)),
        ],
        out_specs=pl.BlockSpec((None, width, tq), lambda bi, hp, qi: (bi, hp, qi)),
        scratch_shapes=[pltpu.VMEM((pairs, s // LANES, LANES, LANES), BF16),
                        pltpu.VMEM((1, LANES), F32)],
        compiler_params=pltpu.CompilerParams(
            dimension_semantics=("parallel", "parallel", "arbitrary"),
            vmem_limit_bytes=VMEM_LIMIT),
        name="stickbreak",
    )(proj, proj, proj)


def _ffn(x, g_ref, wg_ref, wu_ref, wd_ref, act_ref, th):
    h = _rms(x, g_ref[...]).astype(BF16)
    for start in range(0, FFN_HIDDEN, th):
        stop = min(start + th, FFN_HIDDEN)
        hg = jnp.dot(h, wg_ref[:, start:stop], preferred_element_type=F32)
        hu = jnp.dot(h, wu_ref[:, start:stop], preferred_element_type=F32)
        sig = 1.0 / (1.0 + jnp.exp(-hg))
        act_ref[:, start:stop] = (hg * sig * hu).astype(BF16)
    return x + jnp.dot(act_ref[...], wd_ref[...], preferred_element_type=F32)


def _ffn_specs(d, index_map):
    resident = pl.Buffered(1)
    return [
        pl.BlockSpec((1, d), index_map),
        pl.BlockSpec((d, FFN_HIDDEN), index_map, pipeline_mode=resident),
        pl.BlockSpec((d, FFN_HIDDEN), index_map, pipeline_mode=resident),
        pl.BlockSpec((FFN_HIDDEN, d), index_map, pipeline_mode=resident),
    ]


def _mix0_ffn_kernel(a_ref, bg_ref, cg_ref, u_ref, cgh_ref, uh_ref, cw_ref, w_ref, x_ref,
                     g_ref, wg_ref, wu_ref, wd_ref, o_ref, act_ref, *, th):
    i = pl.program_id(1)
    g = cg_ref[...] * u_ref[...]
    gh = cgh_ref[...] * uh_ref[...]
    gh = jnp.where(i == 0, 0.0, gh)
    row = lax.broadcasted_iota(I32, g.shape, 0)
    g1 = jnp.where(row == 0, gh[7:8, :], pltpu.roll(g, 1, axis=0))
    g2 = pltpu.roll(g, 2, axis=0)
    g2 = jnp.where(row == 0, gh[6:7, :], jnp.where(row == 1, gh[7:8, :], g2))
    cw = cw_ref[...]
    y = bg_ref[...] * (cw[0:1, :] * g2 + cw[1:2, :] * g1 + cw[2:3, :] * g)
    acc = lax.dot_general(a_ref[...], w_ref[0:SB_WIDTH, :], (((0,), (0,)), ((), ())),
                          preferred_element_type=F32)
    acc = acc + jnp.dot(y.astype(BF16), w_ref[SB_WIDTH:, :], preferred_element_type=F32)
    o_ref[...] = _ffn(x_ref[...] + acc, g_ref, wg_ref, wu_ref, wd_ref, act_ref, th)


def _mix0_ffn(a_out, proj, conv_w, w_out, x, g, wg, wu, wd, *, tm=512, th=256):
    b, s, d = x.shape
    halo = lambda col: (lambda bi, i: (bi, jnp.maximum(i * (tm // 8) - 1, 0), col))
    return pl.pallas_call(
        functools.partial(_mix0_ffn_kernel, th=th),
        out_shape=jax.ShapeDtypeStruct((b, s, d), F32),
        grid=(b, s // tm),
        in_specs=[
            pl.BlockSpec((None, SB_WIDTH, tm), lambda bi, i: (bi, 0, i)),
            pl.BlockSpec((None, tm, CONV_CH), lambda bi, i: (bi, i, 3)),
            pl.BlockSpec((None, tm, CONV_CH), lambda bi, i: (bi, i, 4)),
            pl.BlockSpec((None, tm, CONV_CH), lambda bi, i: (bi, i, 5)),
            pl.BlockSpec((None, 8, CONV_CH), halo(4)),
            pl.BlockSpec((None, 8, CONV_CH), halo(5)),
            pl.BlockSpec((3, CONV_CH), lambda bi, i: (0, 0)),
            pl.BlockSpec((SB_WIDTH + CONV_CH, d), lambda bi, i: (0, 0)),
            pl.BlockSpec((None, tm, d), lambda bi, i: (bi, i, 0)),
        ] + _ffn_specs(d, lambda bi, i: (0, 0)),
        out_specs=pl.BlockSpec((None, tm, d), lambda bi, i: (bi, i, 0)),
        scratch_shapes=[pltpu.VMEM((tm, FFN_HIDDEN), BF16)],
        compiler_params=pltpu.CompilerParams(
            dimension_semantics=("parallel", "arbitrary"), vmem_limit_bytes=VMEM_LIMIT),
        name="mix0_ffn",
    )(a_out, proj, proj, proj, proj, proj, conv_w, w_out, x, g, wg, wu, wd)


def _proj_ffn_kernel(at_ref, w_ref, x_ref, g_ref, wg_ref, wu_ref, wd_ref, o_ref, act_ref, *, th):
    x1 = x_ref[...] + lax.dot_general(at_ref[...], w_ref[...], (((0,), (0,)), ((), ())),
                                      preferred_element_type=F32)
    o_ref[...] = _ffn(x1, g_ref, wg_ref, wu_ref, wd_ref, act_ref, th)


def _proj_ffn(a_t, w, x, g, wg, wu, wd, *, tm=512, th=256):
    b, k, s = a_t.shape
    d = w.shape[1]
    return pl.pallas_call(
        functools.partial(_proj_ffn_kernel, th=th),
        out_shape=jax.ShapeDtypeStruct((b, s, d), F32),
        grid=(b, s // tm),
        in_specs=[
            pl.BlockSpec((None, k, tm), lambda bi, i: (bi, 0, i)),
            pl.BlockSpec((k, d), lambda bi, i: (0, 0)),
            pl.BlockSpec((None, tm, d), lambda bi, i: (bi, i, 0)),
        ] + _ffn_specs(d, lambda bi, i: (0, 0)),
        out_specs=pl.BlockSpec((None, tm, d), lambda bi, i: (bi, i, 0)),
        scratch_shapes=[pltpu.VMEM((tm, FFN_HIDDEN), BF16)],
        compiler_params=pltpu.CompilerParams(
            dimension_semantics=("parallel", "parallel"), vmem_limit_bytes=VMEM_LIMIT),
        name="proj_ffn",
    )(a_t, w, x, g, wg, wu, wd)


BIAS_ROWS = 3 * LANES


def _bias_tile_kernel(rb_ref, o_ref):
    c = lax.broadcasted_iota(I32, (BIAS_ROWS, LANES), 0)
    t = lax.broadcasted_iota(I32, (BIAS_ROWS, LANES), 1)
    dist = jnp.maximum(LANES + t - c, 0)
    exact = REL_BUCKETS // 2
    d_f = jnp.maximum(dist, 1).astype(F32)
    large = exact + (jnp.log(d_f / exact) / math.log(REL_MAX_DIST / exact)
                     * (REL_BUCKETS - exact)).astype(I32)
    large = jnp.minimum(large, REL_BUCKETS - 1)
    bucket = jnp.where(dist < exact, dist, large)

    def one_head(h, _):
        out = jnp.zeros((BIAS_ROWS, LANES), F32)
        for b in range(REL_BUCKETS):
            out = jnp.where(bucket == b, rb_ref[b, h], out)
        o_ref[h] = (out - rb_ref[REL_BUCKETS - 1, h]) * LOG2E
        return 0

    lax.fori_loop(0, ATT_HEADS, one_head, 0)


def _bias_tiles(rel_bias):
    return pl.pallas_call(
        _bias_tile_kernel,
        out_shape=jax.ShapeDtypeStruct((ATT_HEADS, BIAS_ROWS, LANES), F32),
        in_specs=[pl.BlockSpec(memory_space=pltpu.SMEM)],
        out_specs=pl.BlockSpec(memory_space=pltpu.VMEM),
        name="rel_bias_tiles",
    )(rel_bias)


def _dsa_kernel(q_ref, k_ref, v_ref, qi_ref, wq_ref, kik_ref, qg_ref, kg_ref, bt_ref, seg_ref,
                segt_ref, o_ref, kn_ref, vt_ref, kib_ref, key_ref, planes_ref, mfar_ref, s_ref, qs_ref, *, tq, chunk):
    i = pl.program_id(1)
    s_len = k_ref.shape[0]
    n_chunk = (i + chunk) // chunk
    n_far = (i - 1 + chunk - 1) // chunk
    w0 = jnp.maximum(i - 1, 0)

    @pl.when(i == 0)
    def _prepare_keys():
        kg = kg_ref[...]
        for g in range(KV_HEADS):
            kn = _rms(k_ref[:, g * HEAD_DIM:(g + 1) * HEAD_DIM], kg).astype(BF16)
            kn_ref[g] = jnp.concatenate([kn, kn], axis=1)
        for j in range(s_len // LANES):
            for pair in range(KV_HEADS // 2):
                v_t = v_ref[j * LANES:(j + 1) * LANES, pair * LANES:(pair + 1) * LANES].T
                vt_ref[2 * pair, j, 0:HEAD_DIM] = v_t[0:HEAD_DIM].astype(BF16)
                vt_ref[2 * pair + 1, j, 0:HEAD_DIM] = v_t[HEAD_DIM:].astype(BF16)
        vt_ref[:, :, HEAD_DIM:V_ROWS] = jnp.ones((KV_HEADS, s_len // LANES, V_ROWS - HEAD_DIM, LANES),
                                                 BF16)
        ki = kik_ref[:, 0:HEAD_DIM].astype(BF16)
        kib_ref[...] = jnp.concatenate([ki, ki], axis=1)
        planes_ref[...] = jnp.zeros(planes_ref.shape, I32)

    ck = chunk * LANES
    krow = lax.broadcasted_iota(I32, (ck, tq), 0)
    tpos = i * tq + lax.broadcasted_iota(I32, (ck, tq), 1)
    low_half = lax.broadcasted_iota(I32, (tq, LANES), 1) < HEAD_DIM

    def split_pair(x):
        return [jnp.where(low_half, x, 0.0).astype(BF16), jnp.where(low_half, 0.0, x).astype(BF16)]

    q = q_ref[...]
    sq = q * q
    sq_hi = sq.astype(BF16)
    sq_lo = (sq - sq_hi.astype(F32)).astype(BF16)
    seg = seg_ref[...]
    ssum = (jnp.dot(sq_hi, seg, preferred_element_type=F32)
            + jnp.dot(sq_lo, seg, preferred_element_type=F32))
    rinv = lax.rsqrt(ssum * (1.0 / HEAD_DIM) + EPS)
    pieces, rest = [], rinv
    for _ in range(2):
        piece = rest.astype(BF16)
        pieces.append(piece)
        rest = rest - piece.astype(F32)
    segt = segt_ref[...]
    rinv_b = sum(jnp.dot(piece, segt, preferred_element_type=F32) for piece in pieces)
    qn = q * rinv_b * (qg_ref[...] * (HEAD_DIM ** -0.5 * LOG2E))
    for g in range(KV_HEADS):
        rows = []
        for pair in range(2 * g, 2 * g + 2):
            rows += split_pair(qn[:, pair * LANES:(pair + 1) * LANES])
        qs_ref[g] = jnp.concatenate(rows, axis=0)

    w_t = wq_ref[...].T
    qi_pairs = [jnp.concatenate(split_pair(qi_ref[:, pair * LANES:(pair + 1) * LANES] * 0.125), axis=0)
                for pair in range(IDX_HEADS // 2)]
    w_heads = [w_t[HEAD_DIM + h:HEAD_DIM + h + 1, :] * (IDX_HEADS ** -0.5)
               for h in range(IDX_HEADS)]

    def bit_planes(c):
        for half in range(ck // GROUP_KEYS):
            words = [key_ref[c, pl.ds(half * GROUP_KEYS + 8 * r, 8), :] for r in range(32)]
            for shift, keep in ((16, 0x0000FFFF), (8, 0x00FF00FF), (4, 0x0F0F0F0F),
                                (2, 0x33333333), (1, 0x55555555)):
                amount = jnp.full((8, tq), shift, I32)
                for r in range(32):
                    if r & shift == 0:
                        a, b = words[r], words[r + shift]
                        t = (lax.shift_right_logical(a, amount) ^ b) & keep
                        words[r] = a ^ lax.shift_left(t, amount)
                        words[r + shift] = b ^ t
            words[31] = ~words[31]
            for b in range(32):
                planes_ref[c * (ck // GROUP_KEYS) + half, b] = words[b]

    def score_chunk(c):
        kt = kib_ref[pl.ds(pl.multiple_of(c * ck, ck), ck), :]
        dots = [lax.dot_general(kt, qi_pair, NT_DIMS, preferred_element_type=F32)
                for qi_pair in qi_pairs]
        sc = jnp.zeros((ck, tq), F32)
        for h in range(IDX_HEADS):
            d = dots[h // 2][:, (h % 2) * tq:(h % 2 + 1) * tq]
            sc = sc + w_heads[h] * jnp.maximum(d, 0.0)
        sc = jnp.where(sc == 0.0, 0.0, sc)
        sc = jnp.where(c * ck + krow <= tpos, sc, -jnp.inf)
        bits = pltpu.bitcast(sc, I32)
        key_ref[c] = bits ^ ((bits >> 31) & 0x7FFFFFFF)

    score_chunk(0)

    def scan(c, _):
        bit_planes(c - 1)
        score_chunk(c)
        return 0

    lax.fori_loop(1, n_chunk, scan, 0)

    def count_keys(preds):
        def body(c, cs):
            key = key_ref[c]
            return tuple(
                cnt + jnp.sum(pred(c, key).astype(I32).reshape(ck // 8, 8, tq), axis=0)
                for cnt, pred in zip(cs, preds))
        cs = lax.fori_loop(0, n_chunk, body,
                           tuple(jnp.zeros((8, tq), I32) for _ in preds))
        return [jnp.sum(cnt, axis=0, keepdims=True) for cnt in cs]

    n_groups = s_len // GROUP_KEYS

    def sliced_search():
        bit_planes(n_chunk - 1)
        live = n_chunk * (ck // GROUP_KEYS)

        def two_bits(step, state):
            eqs, n_gt, thr_u = state
            b0 = 30 - 2 * step
            s11, s10, s01, s00 = [], [], [], []
            for g in range(n_groups):
                with_hi = eqs[g] & planes_ref[g, b0 + 1]
                without_hi = eqs[g] ^ with_hi
                low = planes_ref[g, b0]
                s11.append(with_hi & low)
                s10.append(with_hi ^ s11[g])
                s01.append(without_hi & low)
                s00.append(without_hi ^ s01[g])

            def count(sets):
                total = lax.population_count(sets[0])
                for g in range(1, n_groups):
                    total = total + lax.population_count(sets[g])
                return jnp.sum(total, axis=0, keepdims=True)

            up11 = n_gt + count(s11)
            up10 = up11 + count(s10)
            up01 = up10 + count(s01)
            in11 = up11 >= TOPK
            in10 = jnp.logical_not(in11) & (up10 >= TOPK)
            in01 = jnp.logical_not(in11 | in10) & (up01 >= TOPK)
            n_gt = jnp.where(in11, n_gt, jnp.where(in10, up11, jnp.where(in01, up10, up01)))
            bits = jnp.where(in11, 3, jnp.where(in10, 2, jnp.where(in01, 1, 0)))
            thr_u = thr_u | jnp.left_shift(bits, b0)
            m11, m10, m01 = (jnp.broadcast_to(m, (8, tq)) for m in (in11, in10, in01))
            eqs = tuple(jnp.where(m11, s11[g], jnp.where(m10, s10[g], jnp.where(m01, s01[g], s00[g])))
                        for g in range(n_groups))
            return eqs, n_gt, thr_u

        eqs = tuple(jnp.full((8, tq), jnp.where(g < live, -1, 0), I32) for g in range(n_groups))
        zero = jnp.zeros((1, tq), I32)
        eqs, n_gt, thr_u = lax.fori_loop(0, 16, two_bits, (eqs, zero, zero))
        n_eq = lax.population_count(eqs[0])
        for g in range(1, n_groups):
            n_eq = n_eq + lax.population_count(eqs[g])
        return thr_u ^ INT_MIN, n_gt, jnp.sum(n_eq, axis=0, keepdims=True)

    thr, n_gt, n_eq = lax.cond(
        (i + 1) * tq <= TOPK,
        lambda: (jnp.full((1, tq), KEY_NEG_INF, I32), jnp.zeros((1, tq), I32), jnp.zeros((1, tq), I32)),
        sliced_search)
    thr_b = jnp.broadcast_to(thr, (ck, tq))
    need = TOPK - n_gt
    tied = (n_eq > need) & (thr != KEY_NEG_INF)
    any_tied = jnp.max(tied.astype(I32)) > 0

    def tie_bit(b, cut):
        cand = cut + lax.shift_left(jnp.int32(1), 10 - b)
        cand_b = jnp.broadcast_to(cand, (ck, tq))
        (cnt,) = count_keys([lambda c, key: (key == thr_b) & (c * ck + krow < cand_b)])
        return jnp.where(cnt < need, cand, cut)

    cut = lax.cond(any_tied,
                   lambda: lax.fori_loop(0, 11, tie_bit, jnp.zeros((1, tq), I32)),
                   lambda: jnp.full((1, tq), s_len, I32))

    def selection_mask(key, spos, tpos):
        thr_k = jnp.broadcast_to(thr, key.shape)
        cut_k = jnp.broadcast_to(cut, key.shape)
        sel = (key > thr_k) | ((key == thr_k) & (spos <= cut_k))
        return jnp.where(sel & (spos <= tpos), 0.0, NEG)

    def mask_chunk(c, _):
        spos = c * ck + krow
        madd = selection_mask(key_ref[c], spos, tpos)
        mfar_ref[c] = jnp.where(spos < (i - 1) * LANES, madd, NEG)
        return 0

    lax.fori_loop(0, n_far, mask_chunk, 0)

    def tile_mask(tile):
        c, u = tile // chunk, tile % chunk
        key = key_ref[c, pl.ds(pl.multiple_of(u * LANES, LANES), LANES), :]
        krow_t = lax.broadcasted_iota(I32, (LANES, tq), 0)
        tpos_t = i * tq + lax.broadcasted_iota(I32, (LANES, tq), 1)
        return selection_mask(key, tile * LANES + krow_t, tpos_t)

    qs = [qs_ref[g] for g in range(KV_HEADS)]

    groups = range(KV_HEADS)

    def attend(sts, n_keys, first_tile, add_fn):
        key_start = pl.multiple_of(first_tile * LANES, LANES)
        block_max = []
        for g in groups:
            kt = kn_ref[g, pl.ds(key_start, n_keys), :]
            add = jnp.concatenate([add_fn(g * GQA_REP + r) for r in range(GQA_REP)], axis=1)
            s = lax.dot_general(kt, qs[g], NT_DIMS, preferred_element_type=F32) + add
            s_ref[g, 0:n_keys, :] = s
            block_max.append(jnp.max(s, axis=0, keepdims=True))
        ps, stats = [], []
        for g in groups:
            m, _ = sts[g]
            m_new = jnp.maximum(m, block_max[g])
            stats.append((m_new, jnp.exp2(m - m_new)))
            ps.append(jnp.exp2(s_ref[g, 0:n_keys, :] - m_new).astype(BF16))
        new = []
        for g in groups:
            v_t = jnp.concatenate([vt_ref[g, first_tile + u] for u in range(n_keys // LANES)],
                                  axis=1)
            new.append((stats[g][0], stats[g][1] * sts[g][1]
                        + jnp.dot(v_t, ps[g], preferred_element_type=F32)))
        return tuple(new)

    def far_chunk(c, sts):
        madd = mfar_ref[c]
        return attend(sts, ck, c * chunk, lambda head: madd)

    init = (jnp.full((1, GQA_REP * tq), NEG, F32), jnp.zeros((V_ROWS, GQA_REP * tq), F32))
    sts = lax.fori_loop(0, n_far, far_chunk, (init,) * KV_HEADS)

    win_mask = jnp.concatenate([tile_mask(w0), tile_mask(w0 + 1)], axis=0)
    bias_row = pl.multiple_of(jnp.where(i == 0, LANES, 0), LANES)
    sts = attend(sts, 2 * LANES, w0,
                 lambda head: bt_ref[head, pl.ds(bias_row, 2 * LANES), :] + win_mask)
    for g in range(KV_HEADS):
        acc = sts[g][1]
        out_t = acc[0:HEAD_DIM] / acc[HEAD_DIM:HEAD_DIM + 1]
        for pair in range(GQA_REP // 2):
            two = jnp.concatenate(
                [out_t[:, (2 * pair) * tq:(2 * pair + 1) * tq],
                 out_t[:, (2 * pair + 1) * tq:(2 * pair + 2) * tq]], axis=0)
            col = (g * GQA_REP + 2 * pair) * HEAD_DIM
            o_ref[col:col + LANES, :] = two.astype(o_ref.dtype)


def _dsa(proj, q_gain, k_gain, bias_tiles, *, tq=128, chunk=4):
    b, s, _ = proj.shape
    n_t = s // LANES
    att_width = ATT_HEADS * HEAD_DIM
    seg = (jnp.arange(att_width)[:, None] // HEAD_DIM == jnp.arange(LANES)[None, :]).astype(BF16)
    q_gain_all = jnp.tile(q_gain, (1, ATT_HEADS))
    return pl.pallas_call(
        functools.partial(_dsa_kernel, tq=tq, chunk=chunk),
        out_shape=jax.ShapeDtypeStruct((b, att_width, s), BF16),
        grid=(b, s // tq),
        in_specs=[
            pl.BlockSpec((None, tq, 1024), lambda bi, i: (bi, i, 0)),
            pl.BlockSpec((None, s, 256), lambda bi, i: (bi, 0, 4)),
            pl.BlockSpec((None, s, 256), lambda bi, i: (bi, 0, 5)),
            pl.BlockSpec((None, tq, 512), lambda bi, i: (bi, i, 3)),
            pl.BlockSpec((None, tq, LANES), lambda bi, i: (bi, i, 16)),
            pl.BlockSpec((None, s, LANES), lambda bi, i: (bi, 0, 16)),
            pl.BlockSpec((1, att_width), lambda bi, i: (0, 0)),
            pl.BlockSpec((1, HEAD_DIM), lambda bi, i: (0, 0)),
            pl.BlockSpec((ATT_HEADS, BIAS_ROWS, LANES), lambda bi, i: (0, 0, 0)),
            pl.BlockSpec((att_width, LANES), lambda bi, i: (0, 0)),
            pl.BlockSpec((LANES, att_width), lambda bi, i: (0, 0)),
        ],
        out_specs=pl.BlockSpec((None, att_width, tq), lambda bi, i: (bi, 0, i)),
        scratch_shapes=[
            pltpu.VMEM((KV_HEADS, s, LANES), BF16),
            pltpu.VMEM((KV_HEADS, n_t, V_ROWS, LANES), BF16),
            pltpu.VMEM((s, LANES), BF16),
            pltpu.VMEM((n_t // chunk, chunk * LANES, tq), I32),
            pltpu.VMEM((s // GROUP_KEYS, 32, 8, tq), I32),
            pltpu.VMEM((n_t // chunk, chunk * LANES, tq), F32),
            pltpu.VMEM((KV_HEADS, chunk * LANES, GQA_REP * tq), F32),
            pltpu.VMEM((KV_HEADS, GQA_REP * tq, LANES), BF16),
        ],
        compiler_params=pltpu.CompilerParams(
            dimension_semantics=("parallel", "arbitrary"), vmem_limit_bytes=VMEM_LIMIT),
        name="dsa_attention",
    )(proj, proj, proj, proj, proj, proj, q_gain_all, k_gain, bias_tiles, seg, seg.T)


def kernel(x, norm_mix, norm_ffn, ev_w_in, ev_conv_w, ev_w_out, od_w_in, od_q_gain,
           od_k_gain, od_w_out, rel_bias, ffn_w_gate, ffn_w_up, ffn_w_down):
    b, s, d = x.shape
    n_tok = b * s
    bf = lambda w: w.astype(BF16)

    proj0 = _norm_matmul(x.reshape(n_tok, d), norm_mix[0:1], bf(ev_w_in[0]))
    proj0 = proj0.reshape(b, s, -1)
    a_out = _stickbreak(proj0)
    x2 = _mix0_ffn(a_out, proj0, ev_conv_w[0], bf(ev_w_out[0]), x, norm_ffn[0:1],
                   bf(ffn_w_gate[0]), bf(ffn_w_up[0]), bf(ffn_w_down[0])).reshape(n_tok, d)

    w_in1 = jnp.pad(od_w_in[0], ((0, 0), (0, ODD_IN_PAD - od_w_in.shape[2])))
    proj1 = _norm_matmul(x2, norm_mix[1:2], bf(w_in1)).reshape(b, s, ODD_IN_PAD)
    att = _dsa(proj1, od_q_gain[0:1], od_k_gain[0:1], _bias_tiles(rel_bias))
    return _proj_ffn(att, bf(od_w_out[0]), x2.reshape(b, s, d), norm_ffn[1:2],
                     bf(ffn_w_gate[1]), bf(ffn_w_up[1]), bf(ffn_w_down[1]))
```

```python
import functools
import math

import jax
import jax.numpy as jnp
from jax import lax
from jax.experimental import pallas as pl
from jax.experimental.pallas import tpu as pltpu

F32 = jnp.float32
BF16 = jnp.bfloat16
I32 = jnp.int32

D_MODEL = 1024
SEQ = 2048
HEAD_DIM = 64
SB_WIDTH = 512
CONV_CH = 512
ATT_HEADS = 16
KV_HEADS = 4
GQA_REP = 4
IDX_HEADS = 8
TOPK = 256
REL_BUCKETS = 32
REL_MAX_DIST = 128
FFN_HIDDEN = 2816
EPS = 1e-6
ODD_IN_PAD = 2176

LOG2E = 1.4426950408889634
LANES = 128
VMEM_LIMIT = 56 * 1024 * 1024
NEG = -1e30
INT_MIN = -(2 ** 31)
GROUP_KEYS = 32 * 8
V_ROWS = HEAD_DIM + 16
UNDERFLOW_MARGIN = 160.0
NORM_SLACK = 1.03
KEY_NEG_INF = -(2 ** 31) + 0x7FFFFF

NT_DIMS = (((1,), (1,)), ((), ()))


def _rms(x, g):
    ms = jnp.mean(x * x, axis=-1, keepdims=True)
    return x * lax.rsqrt(ms + EPS) * g


def _norm_matmul_kernel(x_ref, g_ref, w_ref, o_ref, *low_ref, tn):
    h = _rms(x_ref[...], g_ref[...]).astype(BF16)
    n_f32 = o_ref.shape[1]
    n = w_ref.shape[1]
    for start in range(0, n, tn):
        stop = min(start + tn, n if start >= n_f32 else n_f32)
        y = jnp.dot(h, w_ref[:, start:stop], preferred_element_type=F32)
        if start < n_f32:
            o_ref[:, start:stop] = y
        else:
            low_ref[0][:, start - n_f32:stop - n_f32] = y.astype(BF16)


def _norm_matmul(x, g, w, *, n_f32=None, tm=1024, tn=512):
    m, d = x.shape
    n = w.shape[1]
    n_f32 = n if n_f32 is None else n_f32
    assert n_f32 % tn == 0 or n_f32 == n
    out_shape = [jax.ShapeDtypeStruct((m, n_f32), F32)]
    out_specs = [pl.BlockSpec((tm, n_f32), lambda i: (i, 0))]
    if n_f32 < n:
        out_shape.append(jax.ShapeDtypeStruct((m, n - n_f32), BF16))
        out_specs.append(pl.BlockSpec((tm, n - n_f32), lambda i: (i, 0)))
    return pl.pallas_call(
        functools.partial(_norm_matmul_kernel, tn=tn),
        out_shape=out_shape,
        grid=(m // tm,),
        in_specs=[
            pl.BlockSpec((tm, d), lambda i: (i, 0)),
            pl.BlockSpec((1, d), lambda i: (0, 0)),
            pl.BlockSpec((d, n), lambda i: (0, 0)),
        ],
        out_specs=out_specs,
        compiler_params=pltpu.CompilerParams(
            dimension_semantics=("parallel",), vmem_limit_bytes=VMEM_LIMIT),
        name="norm_matmul",
    )(x, g, w)


def _sb_kernel(q_ref, k_ref, v_ref, o_ref, vt_ref, kmax_ref, *, tq, pairs):
    qi = pl.program_id(2)
    n_sub = tq // LANES
    s_len = k_ref.shape[0]
    width = pairs * LANES
    head_of = (lax.broadcasted_iota(I32, (width, LANES), 0) // HEAD_DIM
               == lax.broadcasted_iota(I32, (width, LANES), 1)).astype(BF16)

    def max_sq_norm(x):
        x = x.astype(F32)
        sq = (x * x).astype(BF16)
        return jnp.max(jnp.dot(sq, head_of, preferred_element_type=F32), axis=0, keepdims=True)

    @pl.when(qi == 0)
    def _per_sequence_setup():
        for p in range(pairs):
            for j in range(s_len // LANES):
                vt_ref[p, j] = v_ref[j * LANES:(j + 1) * LANES,
                                     p * LANES:(p + 1) * LANES].astype(F32).T.astype(BF16)
        kmax = jnp.zeros((1, LANES), F32)
        for r in range(0, s_len, tq):
            kmax = jnp.maximum(kmax, max_sq_norm(k_ref[r:r + tq, :]))
        kmax_ref[...] = kmax

    lane = lax.broadcasted_iota(I32, (tq, LANES), 1)
    krow = lax.broadcasted_iota(I32, (tq, tq), 0)
    qcol = lax.broadcasted_iota(I32, (tq, tq), 1)
    from_here = (qcol >= krow).astype(BF16)
    causal = krow < qcol
    q_heads = []
    for p in range(pairs):
        q = q_ref[:, p * LANES:(p + 1) * LANES] * (HEAD_DIM ** -0.5 * LOG2E)
        for h in range(2):
            in_head = (lane >= HEAD_DIM * h) & (lane < HEAD_DIM * (h + 1))
            q_heads.append(jnp.where(in_head, q, 0.0).astype(BF16))

    def block(kb, st, masked):
        start = pl.multiple_of(kb * tq, tq)
        heads = range(2 * pairs)
        kblks = [k_ref[pl.ds(start, tq), p * LANES:(p + 1) * LANES].astype(BF16)
                 for p in range(pairs)]
        v_ts = [jnp.concatenate([vt_ref[p, kb * n_sub + u] for u in range(n_sub)], axis=1)
                for p in range(pairs)]
        zs = [lax.dot_general(kblks[idx // 2], q_heads[idx], NT_DIMS, preferred_element_type=F32)
              for idx in heads]
        softplus, his, los = [], [], []
        for z in zs:
            sp = jnp.maximum(z, 0.0) + jnp.log2(1.0 + jnp.exp2(-jnp.abs(z)))
            if masked:
                sp = jnp.where(causal, sp, 0.0)
            hi = sp.astype(BF16)
            softplus.append(sp)
            his.append(hi)
            los.append((sp - hi.astype(F32)).astype(BF16))
        since = [jnp.dot(from_here, his[idx], preferred_element_type=F32)
                 + jnp.dot(from_here, los[idx], preferred_element_type=F32) for idx in heads]
        ws = []
        for idx in heads:
            w = jnp.exp2(zs[idx] - since[idx] - st[2 * idx])
            if masked:
                w = jnp.where(causal, w, 0.0)
            ws.append(w.astype(BF16))
        new = []
        for idx in heads:
            h = idx % 2
            acc = st[2 * idx + 1] + jnp.dot(v_ts[idx // 2][h * HEAD_DIM:(h + 1) * HEAD_DIM, :],
                                            ws[idx], preferred_element_type=F32)
            carry = st[2 * idx] + jnp.sum(softplus[idx], axis=0, keepdims=True)
            new += [carry, acc]
        return tuple(new)

    z_bound = jnp.sqrt(max_sq_norm(q_ref[...]) * kmax_ref[...]) * (HEAD_DIM ** -0.5 * LOG2E * NORM_SLACK)
    z_bounds = [jnp.broadcast_to(z_bound[:, idx:idx + 1], (1, tq)) for idx in range(2 * pairs)]

    def all_underflow(st):
        slack = st[0] - z_bounds[0]
        for idx in range(1, 2 * pairs):
            slack = jnp.minimum(slack, st[2 * idx] - z_bounds[idx])
        return jnp.min(slack) > UNDERFLOW_MARGIN

    def walk(state):
        n, _, st = state
        st = block(qi - 1 - n, st, False)
        return n + 1, all_underflow(st), st

    st = (jnp.zeros((1, tq), F32), jnp.zeros((HEAD_DIM, tq), F32)) * (2 * pairs)
    st = block(qi, st, True)
    _, _, st = lax.while_loop(lambda state: (state[0] < qi) & jnp.logical_not(state[1]), walk,
                              (jnp.int32(0), all_underflow(st), st))
    for p in range(pairs):
        out_t = jnp.concatenate([st[4 * p + 1], st[4 * p + 3]], axis=0)
        o_ref[p * LANES:(p + 1) * LANES, :] = out_t.astype(o_ref.dtype)


def _stickbreak(proj, kv, *, tq=256, pairs=4):
    b, s, _ = proj.shape
    width = pairs * LANES
    n_steps = SB_WIDTH // width
    return pl.pallas_call(
        functools.partial(_sb_kernel, tq=tq, pairs=pairs),
        out_shape=jax.ShapeDtypeStruct((b, SB_WIDTH, s), BF16),
        grid=(b, n_steps, s // tq),
        in_specs=[
            pl.BlockSpec((None, tq, width), lambda bi, hp, qi: (bi, qi, hp)),
            pl.BlockSpec((None, s, width), lambda bi, hp, qi: (bi, 0, hp)),
            pl.BlockSpec((None, s, width), lambda bi, hp, qi: (bi, 0, n_steps + hp)),
        ],
        out_specs=pl.BlockSpec((None, width, tq), lambda bi, hp, qi: (bi, hp, qi)),
        scratch_shapes=[pltpu.VMEM((pairs, s // LANES, LANES, LANES), BF16),
                        pltpu.VMEM((1, LANES), F32)],
        compiler_params=pltpu.CompilerParams(
            dimension_semantics=("parallel", "parallel", "arbitrary"),
            vmem_limit_bytes=VMEM_LIMIT),
        name="stickbreak",
    )(proj, kv, kv)


def _ffn(x, g_ref, wg_ref, wu_ref, wd_ref, act_ref, th):
    h = _rms(x, g_ref[...]).astype(BF16)
    for start in range(0, FFN_HIDDEN, th):
        stop = min(start + th, FFN_HIDDEN)
        hg = jnp.dot(h, wg_ref[:, start:stop], preferred_element_type=F32)
        hu = jnp.dot(h, wu_ref[:, start:stop], preferred_element_type=F32)
        sig = 1.0 / (1.0 + jnp.exp(-hg))
        act_ref[:, start:stop] = (hg * sig * hu).astype(BF16)
    return x + jnp.dot(act_ref[...], wd_ref[...], preferred_element_type=F32)


def _ffn_specs(d, index_map):
    resident = pl.Buffered(1)
    return [
        pl.BlockSpec((1, d), index_map),
        pl.BlockSpec((d, FFN_HIDDEN), index_map, pipeline_mode=resident),
        pl.BlockSpec((d, FFN_HIDDEN), index_map, pipeline_mode=resident),
        pl.BlockSpec((FFN_HIDDEN, d), index_map, pipeline_mode=resident),
    ]


def _mix0_ffn_kernel(a_ref, bg_ref, cg_ref, u_ref, cgh_ref, uh_ref, cw_ref, w_ref, x_ref,
                     g_ref, wg_ref, wu_ref, wd_ref, o_ref, act_ref, *, th):
    i = pl.program_id(1)
    g = cg_ref[...] * u_ref[...]
    gh = cgh_ref[...] * uh_ref[...]
    gh = jnp.where(i == 0, 0.0, gh)
    row = lax.broadcasted_iota(I32, g.shape, 0)
    g1 = jnp.where(row == 0, gh[7:8, :], pltpu.roll(g, 1, axis=0))
    g2 = pltpu.roll(g, 2, axis=0)
    g2 = jnp.where(row == 0, gh[6:7, :], jnp.where(row == 1, gh[7:8, :], g2))
    cw = cw_ref[...]
    y = bg_ref[...] * (cw[0:1, :] * g2 + cw[1:2, :] * g1 + cw[2:3, :] * g)
    acc = lax.dot_general(a_ref[...], w_ref[0:SB_WIDTH, :], (((0,), (0,)), ((), ())),
                          preferred_element_type=F32)
    acc = acc + jnp.dot(y.astype(BF16), w_ref[SB_WIDTH:, :], preferred_element_type=F32)
    o_ref[...] = _ffn(x_ref[...] + acc, g_ref, wg_ref, wu_ref, wd_ref, act_ref, th)


def _mix0_ffn(a_out, proj, conv_w, w_out, x, g, wg, wu, wd, *, tm=512, th=256):
    b, s, d = x.shape
    halo = lambda col: (lambda bi, i: (bi, jnp.maximum(i * (tm // 8) - 1, 0), col))
    return pl.pallas_call(
        functools.partial(_mix0_ffn_kernel, th=th),
        out_shape=jax.ShapeDtypeStruct((b, s, d), F32),
        grid=(b, s // tm),
        in_specs=[
            pl.BlockSpec((None, SB_WIDTH, tm), lambda bi, i: (bi, 0, i)),
            pl.BlockSpec((None, tm, CONV_CH), lambda bi, i: (bi, i, 1)),
            pl.BlockSpec((None, tm, CONV_CH), lambda bi, i: (bi, i, 2)),
            pl.BlockSpec((None, tm, CONV_CH), lambda bi, i: (bi, i, 3)),
            pl.BlockSpec((None, 8, CONV_CH), halo(2)),
            pl.BlockSpec((None, 8, CONV_CH), halo(3)),
            pl.BlockSpec((3, CONV_CH), lambda bi, i: (0, 0)),
            pl.BlockSpec((SB_WIDTH + CONV_CH, d), lambda bi, i: (0, 0)),
            pl.BlockSpec((None, tm, d), lambda bi, i: (bi, i, 0)),
        ] + _ffn_specs(d, lambda bi, i: (0, 0)),
        out_specs=pl.BlockSpec((None, tm, d), lambda bi, i: (bi, i, 0)),
        scratch_shapes=[pltpu.VMEM((tm, FFN_HIDDEN), BF16)],
        compiler_params=pltpu.CompilerParams(
            dimension_semantics=("parallel", "arbitrary"), vmem_limit_bytes=VMEM_LIMIT),
        name="mix0_ffn",
    )(a_out, proj, proj, proj, proj, proj, conv_w, w_out, x, g, wg, wu, wd)


def _proj_ffn_kernel(at_ref, w_ref, x_ref, g_ref, wg_ref, wu_ref, wd_ref, o_ref, act_ref, *, th):
    x1 = x_ref[...] + lax.dot_general(at_ref[...], w_ref[...], (((0,), (0,)), ((), ())),
                                      preferred_element_type=F32)
    o_ref[...] = _ffn(x1, g_ref, wg_ref, wu_ref, wd_ref, act_ref, th)


def _proj_ffn(a_t, w, x, g, wg, wu, wd, *, tm=512, th=256):
    b, k, s = a_t.shape
    d = w.shape[1]
    return pl.pallas_call(
        functools.partial(_proj_ffn_kernel, th=th),
        out_shape=jax.ShapeDtypeStruct((b, s, d), F32),
        grid=(b, s // tm),
        in_specs=[
            pl.BlockSpec((None, k, tm), lambda bi, i: (bi, 0, i)),
            pl.BlockSpec((k, d), lambda bi, i: (0, 0)),
            pl.BlockSpec((None, tm, d), lambda bi, i: (bi, i, 0)),
        ] + _ffn_specs(d, lambda bi, i: (0, 0)),
        out_specs=pl.BlockSpec((None, tm, d), lambda bi, i: (bi, i, 0)),
        scratch_shapes=[pltpu.VMEM((tm, FFN_HIDDEN), BF16)],
        compiler_params=pltpu.CompilerParams(
            dimension_semantics=("parallel", "parallel"), vmem_limit_bytes=VMEM_LIMIT),
        name="proj_ffn",
    )(a_t, w, x, g, wg, wu, wd)


BIAS_ROWS = 3 * LANES


def _bias_tile_kernel(rb_ref, o_ref):
    c = lax.broadcasted_iota(I32, (BIAS_ROWS, LANES), 0)
    t = lax.broadcasted_iota(I32, (BIAS_ROWS, LANES), 1)
    dist = jnp.maximum(LANES + t - c, 0)
    exact = REL_BUCKETS // 2
    d_f = jnp.maximum(dist, 1).astype(F32)
    large = exact + (jnp.log(d_f / exact) / math.log(REL_MAX_DIST / exact)
                     * (REL_BUCKETS - exact)).astype(I32)
    large = jnp.minimum(large, REL_BUCKETS - 1)
    bucket = jnp.where(dist < exact, dist, large)

    def one_head(h, _):
        out = jnp.zeros((BIAS_ROWS, LANES), F32)
        for b in range(REL_BUCKETS):
            out = jnp.where(bucket == b, rb_ref[b, h], out)
        o_ref[h] = (out - rb_ref[REL_BUCKETS - 1, h]) * LOG2E
        return 0

    lax.fori_loop(0, ATT_HEADS, one_head, 0)


def _bias_tiles(rel_bias):
    return pl.pallas_call(
        _bias_tile_kernel,
        out_shape=jax.ShapeDtypeStruct((ATT_HEADS, BIAS_ROWS, LANES), F32),
        in_specs=[pl.BlockSpec(memory_space=pltpu.SMEM)],
        out_specs=pl.BlockSpec(memory_space=pltpu.VMEM),
        name="rel_bias_tiles",
    )(rel_bias)


def _dsa_kernel(q_ref, k_ref, v_ref, qi_ref, wq_ref, kik_ref, qg_ref, kg_ref, bt_ref, seg_ref,
                segt_ref, o_ref, kn_ref, vt_ref, kib_ref, key_ref, planes_ref, mfar_ref, s_ref, qs_ref, *, tq, chunk):
    i = pl.program_id(1)
    s_len = k_ref.shape[0]
    n_chunk = (i + chunk) // chunk
    n_far = (i - 1 + chunk - 1) // chunk
    w0 = jnp.maximum(i - 1, 0)

    @pl.when(i == 0)
    def _prepare_keys():
        kg = kg_ref[...]
        for g in range(KV_HEADS):
            kn = _rms(k_ref[:, g * HEAD_DIM:(g + 1) * HEAD_DIM], kg).astype(BF16)
            kn_ref[g] = jnp.concatenate([kn, kn], axis=1)
        for j in range(s_len // LANES):
            for pair in range(KV_HEADS // 2):
                v_t = v_ref[j * LANES:(j + 1) * LANES, pair * LANES:(pair + 1) * LANES].T
                vt_ref[2 * pair, j, 0:HEAD_DIM] = v_t[0:HEAD_DIM].astype(BF16)
                vt_ref[2 * pair + 1, j, 0:HEAD_DIM] = v_t[HEAD_DIM:].astype(BF16)
        vt_ref[:, :, HEAD_DIM:V_ROWS] = jnp.ones((KV_HEADS, s_len // LANES, V_ROWS - HEAD_DIM, LANES),
                                                 BF16)
        ki = kik_ref[:, 0:HEAD_DIM].astype(BF16)
        kib_ref[...] = jnp.concatenate([ki, ki], axis=1)
        planes_ref[...] = jnp.zeros(planes_ref.shape, I32)

    ck = chunk * LANES
    krow = lax.broadcasted_iota(I32, (ck, tq), 0)
    tpos = i * tq + lax.broadcasted_iota(I32, (ck, tq), 1)
    low_half = lax.broadcasted_iota(I32, (tq, LANES), 1) < HEAD_DIM

    def split_pair(x):
        return [jnp.where(low_half, x, 0.0).astype(BF16), jnp.where(low_half, 0.0, x).astype(BF16)]

    q = q_ref[...]
    sq = q * q
    sq_hi = sq.astype(BF16)
    sq_lo = (sq - sq_hi.astype(F32)).astype(BF16)
    seg = seg_ref[...]
    ssum = (jnp.dot(sq_hi, seg, preferred_element_type=F32)
            + jnp.dot(sq_lo, seg, preferred_element_type=F32))
    rinv = lax.rsqrt(ssum * (1.0 / HEAD_DIM) + EPS)
    pieces, rest = [], rinv
    for _ in range(2):
        piece = rest.astype(BF16)
        pieces.append(piece)
        rest = rest - piece.astype(F32)
    segt = segt_ref[...]
    rinv_b = sum(jnp.dot(piece, segt, preferred_element_type=F32) for piece in pieces)
    qn = q * rinv_b * (qg_ref[...] * (HEAD_DIM ** -0.5 * LOG2E))
    for g in range(KV_HEADS):
        rows = []
        for pair in range(2 * g, 2 * g + 2):
            rows += split_pair(qn[:, pair * LANES:(pair + 1) * LANES])
        qs_ref[g] = jnp.concatenate(rows, axis=0)

    w_t = wq_ref[...].T
    qi_pairs = [jnp.concatenate(split_pair(qi_ref[:, pair * LANES:(pair + 1) * LANES] * 0.125), axis=0)
                for pair in range(IDX_HEADS // 2)]
    w_heads = [w_t[HEAD_DIM + h:HEAD_DIM + h + 1, :] * (IDX_HEADS ** -0.5)
               for h in range(IDX_HEADS)]

    def bit_planes(c):
        for half in range(ck // GROUP_KEYS):
            words = [key_ref[c, pl.ds(half * GROUP_KEYS + 8 * r, 8), :] for r in range(32)]
            for shift, keep in ((16, 0x0000FFFF), (8, 0x00FF00FF), (4, 0x0F0F0F0F),
                                (2, 0x33333333), (1, 0x55555555)):
                amount = jnp.full((8, tq), shift, I32)
                for r in range(32):
                    if r & shift == 0:
                        a, b = words[r], words[r + shift]
                        t = (lax.shift_right_logical(a, amount) ^ b) & keep
                        words[r] = a ^ lax.shift_left(t, amount)
                        words[r + shift] = b ^ t
            words[31] = ~words[31]
            for b in range(32):
                planes_ref[c * (ck // GROUP_KEYS) + half, b] = words[b]

    def score_chunk(c):
        kt = kib_ref[pl.ds(pl.multiple_of(c * ck, ck), ck), :]
        dots = [lax.dot_general(kt, qi_pair, NT_DIMS, preferred_element_type=F32)
                for qi_pair in qi_pairs]
        sc = jnp.zeros((ck, tq), F32)
        for h in range(IDX_HEADS):
            d = dots[h // 2][:, (h % 2) * tq:(h % 2 + 1) * tq]
            sc = sc + w_heads[h] * jnp.maximum(d, 0.0)
        sc = jnp.where(sc == 0.0, 0.0, sc)
        sc = jnp.where(c * ck + krow <= tpos, sc, -jnp.inf)
        bits = pltpu.bitcast(sc, I32)
        key_ref[c] = bits ^ ((bits >> 31) & 0x7FFFFFFF)

    score_chunk(0)

    def scan(c, _):
        bit_planes(c - 1)
        score_chunk(c)
        return 0

    lax.fori_loop(1, n_chunk, scan, 0)

    def count_keys(preds):
        def body(c, cs):
            key = key_ref[c]
            return tuple(
                cnt + jnp.sum(pred(c, key).astype(I32).reshape(ck // 8, 8, tq), axis=0)
                for cnt, pred in zip(cs, preds))
        cs = lax.fori_loop(0, n_chunk, body,
                           tuple(jnp.zeros((8, tq), I32) for _ in preds))
        return [jnp.sum(cnt, axis=0, keepdims=True) for cnt in cs]

    n_groups = s_len // GROUP_KEYS

    def sliced_search():
        bit_planes(n_chunk - 1)
        live = n_chunk * (ck // GROUP_KEYS)

        def two_bits(step, state):
            eqs, n_gt, thr_u = state
            b0 = 30 - 2 * step
            s11, s10, s01, s00 = [], [], [], []
            for g in range(n_groups):
                with_hi = eqs[g] & planes_ref[g, b0 + 1]
                without_hi = eqs[g] ^ with_hi
                low = planes_ref[g, b0]
                s11.append(with_hi & low)
                s10.append(with_hi ^ s11[g])
                s01.append(without_hi & low)
                s00.append(without_hi ^ s01[g])

            def count(sets):
                total = lax.population_count(sets[0])
                for g in range(1, n_groups):
                    total = total + lax.population_count(sets[g])
                return jnp.sum(total, axis=0, keepdims=True)

            up11 = n_gt + count(s11)
            up10 = up11 + count(s10)
            up01 = up10 + count(s01)
            in11 = up11 >= TOPK
            in10 = jnp.logical_not(in11) & (up10 >= TOPK)
            in01 = jnp.logical_not(in11 | in10) & (up01 >= TOPK)
            n_gt = jnp.where(in11, n_gt, jnp.where(in10, up11, jnp.where(in01, up10, up01)))
            bits = jnp.where(in11, 3, jnp.where(in10, 2, jnp.where(in01, 1, 0)))
            thr_u = thr_u | jnp.left_shift(bits, b0)
            m11, m10, m01 = (jnp.broadcast_to(m, (8, tq)) for m in (in11, in10, in01))
            eqs = tuple(jnp.where(m11, s11[g], jnp.where(m10, s10[g], jnp.where(m01, s01[g], s00[g])))
                        for g in range(n_groups))
            return eqs, n_gt, thr_u

        eqs = tuple(jnp.full((8, tq), jnp.where(g < live, -1, 0), I32) for g in range(n_groups))
        zero = jnp.zeros((1, tq), I32)
        eqs, n_gt, thr_u = lax.fori_loop(0, 16, two_bits, (eqs, zero, zero))
        n_eq = lax.population_count(eqs[0])
        for g in range(1, n_groups):
            n_eq = n_eq + lax.population_count(eqs[g])
        return thr_u ^ INT_MIN, n_gt, jnp.sum(n_eq, axis=0, keepdims=True)

    thr, n_gt, n_eq = lax.cond(
        (i + 1) * tq <= TOPK,
        lambda: (jnp.full((1, tq), KEY_NEG_INF, I32), jnp.zeros((1, tq), I32), jnp.zeros((1, tq), I32)),
        sliced_search)
    thr_b = jnp.broadcast_to(thr, (ck, tq))
    need = TOPK - n_gt
    tied = (n_eq > need) & (thr != KEY_NEG_INF)
    any_tied = jnp.max(tied.astype(I32)) > 0

    def tie_bit(b, cut):
        cand = cut + lax.shift_left(jnp.int32(1), 10 - b)
        cand_b = jnp.broadcast_to(cand, (ck, tq))
        (cnt,) = count_keys([lambda c, key: (key == thr_b) & (c * ck + krow < cand_b)])
        return jnp.where(cnt < need, cand, cut)

    cut = lax.cond(any_tied,
                   lambda: lax.fori_loop(0, 11, tie_bit, jnp.zeros((1, tq), I32)),
                   lambda: jnp.full((1, tq), s_len, I32))

    def selection_mask(key, spos, tpos):
        thr_k = jnp.broadcast_to(thr, key.shape)
        cut_k = jnp.broadcast_to(cut, key.shape)
        sel = (key > thr_k) | ((key == thr_k) & (spos <= cut_k))
        return jnp.where(sel & (spos <= tpos), 0.0, NEG)

    def mask_chunk(c, _):
        spos = c * ck + krow
        madd = selection_mask(key_ref[c], spos, tpos)
        mfar_ref[c] = jnp.where(spos < (i - 1) * LANES, madd, NEG)
        return 0

    lax.fori_loop(0, n_far, mask_chunk, 0)

    def tile_mask(tile):
        c, u = tile // chunk, tile % chunk
        key = key_ref[c, pl.ds(pl.multiple_of(u * LANES, LANES), LANES), :]
        krow_t = lax.broadcasted_iota(I32, (LANES, tq), 0)
        tpos_t = i * tq + lax.broadcasted_iota(I32, (LANES, tq), 1)
        return selection_mask(key, tile * LANES + krow_t, tpos_t)

    qs = [qs_ref[g] for g in range(KV_HEADS)]

    groups = range(KV_HEADS)

    def attend(sts, n_keys, first_tile, add_fn):
        key_start = pl.multiple_of(first_tile * LANES, LANES)
        block_max = []
        for g in groups:
            kt = kn_ref[g, pl.ds(key_start, n_keys), :]
            add = jnp.concatenate([add_fn(g * GQA_REP + r) for r in range(GQA_REP)], axis=1)
            s = lax.dot_general(kt, qs[g], NT_DIMS, preferred_element_type=F32) + add
            s_ref[g, 0:n_keys, :] = s
            block_max.append(jnp.max(s, axis=0, keepdims=True))
        ps, stats = [], []
        for g in groups:
            m, _ = sts[g]
            m_new = jnp.maximum(m, block_max[g])
            stats.append((m_new, jnp.exp2(m - m_new)))
            ps.append(jnp.exp2(s_ref[g, 0:n_keys, :] - m_new).astype(BF16))
        new = []
        for g in groups:
            v_t = jnp.concatenate([vt_ref[g, first_tile + u] for u in range(n_keys // LANES)],
                                  axis=1)
            new.append((stats[g][0], stats[g][1] * sts[g][1]
                        + jnp.dot(v_t, ps[g], preferred_element_type=F32)))
        return tuple(new)

    def far_chunk(c, sts):
        madd = mfar_ref[c]
        return attend(sts, ck, c * chunk, lambda head: madd)

    init = (jnp.full((1, GQA_REP * tq), NEG, F32), jnp.zeros((V_ROWS, GQA_REP * tq), F32))
    sts = lax.fori_loop(0, n_far, far_chunk, (init,) * KV_HEADS)

    win_mask = jnp.concatenate([tile_mask(w0), tile_mask(w0 + 1)], axis=0)
    bias_row = pl.multiple_of(jnp.where(i == 0, LANES, 0), LANES)
    sts = attend(sts, 2 * LANES, w0,
                 lambda head: bt_ref[head, pl.ds(bias_row, 2 * LANES), :] + win_mask)
    for g in range(KV_HEADS):
        acc = sts[g][1]
        out_t = acc[0:HEAD_DIM] / acc[HEAD_DIM:HEAD_DIM + 1]
        for pair in range(GQA_REP // 2):
            two = jnp.concatenate(
                [out_t[:, (2 * pair) * tq:(2 * pair + 1) * tq],
                 out_t[:, (2 * pair + 1) * tq:(2 * pair + 2) * tq]], axis=0)
            col = (g * GQA_REP + 2 * pair) * HEAD_DIM
            o_ref[col:col + LANES, :] = two.astype(o_ref.dtype)


def _dsa(proj, q_gain, k_gain, bias_tiles, *, tq=128, chunk=4):
    b, s, _ = proj.shape
    n_t = s // LANES
    att_width = ATT_HEADS * HEAD_DIM
    seg = (jnp.arange(att_width)[:, None] // HEAD_DIM == jnp.arange(LANES)[None, :]).astype(BF16)
    q_gain_all = jnp.tile(q_gain, (1, ATT_HEADS))
    return pl.pallas_call(
        functools.partial(_dsa_kernel, tq=tq, chunk=chunk),
        out_shape=jax.ShapeDtypeStruct((b, att_width, s), BF16),
        grid=(b, s // tq),
        in_specs=[
            pl.BlockSpec((None, tq, 1024), lambda bi, i: (bi, i, 0)),
            pl.BlockSpec((None, s, 256), lambda bi, i: (bi, 0, 4)),
            pl.BlockSpec((None, s, 256), lambda bi, i: (bi, 0, 5)),
            pl.BlockSpec((None, tq, 512), lambda bi, i: (bi, i, 3)),
            pl.BlockSpec((None, tq, LANES), lambda bi, i: (bi, i, 16)),
            pl.BlockSpec((None, s, LANES), lambda bi, i: (bi, 0, 16)),
            pl.BlockSpec((1, att_width), lambda bi, i: (0, 0)),
            pl.BlockSpec((1, HEAD_DIM), lambda bi, i: (0, 0)),
            pl.BlockSpec((ATT_HEADS, BIAS_ROWS, LANES), lambda bi, i: (0, 0, 0)),
            pl.BlockSpec((att_width, LANES), lambda bi, i: (0, 0)),
            pl.BlockSpec((LANES, att_width), lambda bi, i: (0, 0)),
        ],
        out_specs=pl.BlockSpec((None, att_width, tq), lambda bi, i: (bi, 0, i)),
        scratch_shapes=[
            pltpu.VMEM((KV_HEADS, s, LANES), BF16),
            pltpu.VMEM((KV_HEADS, n_t, V_ROWS, LANES), BF16),
            pltpu.VMEM((s, LANES), BF16),
            pltpu.VMEM((n_t // chunk, chunk * LANES, tq), I32),
            pltpu.VMEM((s // GROUP_KEYS, 32, 8, tq), I32),
            pltpu.VMEM((n_t // chunk, chunk * LANES, tq), F32),
            pltpu.VMEM((KV_HEADS, chunk * LANES, GQA_REP * tq), F32),
            pltpu.VMEM((KV_HEADS, GQA_REP * tq, LANES), BF16),
        ],
        compiler_params=pltpu.CompilerParams(
            dimension_semantics=("parallel", "arbitrary"), vmem_limit_bytes=VMEM_LIMIT),
        name="dsa_attention",
    )(proj, proj, proj, proj, proj, proj, q_gain_all, k_gain, bias_tiles, seg, seg.T)


def kernel(x, norm_mix, norm_ffn, ev_w_in, ev_conv_w, ev_w_out, od_w_in, od_q_gain,
           od_k_gain, od_w_out, rel_bias, ffn_w_gate, ffn_w_up, ffn_w_down):
    b, s, d = x.shape
    n_tok = b * s
    bf = lambda w: w.astype(BF16)

    w_in0 = ev_w_in[0]
    kv_cols = slice(SB_WIDTH, 3 * SB_WIDTH)
    w_in0 = jnp.concatenate([w_in0[:, :SB_WIDTH], w_in0[:, 3 * SB_WIDTH:], w_in0[:, kv_cols]], axis=1)
    proj0, kv0 = _norm_matmul(x.reshape(n_tok, d), norm_mix[0:1], bf(w_in0),
                              n_f32=SB_WIDTH + 3 * CONV_CH)
    proj0 = proj0.reshape(b, s, -1)
    a_out = _stickbreak(proj0, kv0.reshape(b, s, -1))
    x2 = _mix0_ffn(a_out, proj0, ev_conv_w[0], bf(ev_w_out[0]), x, norm_ffn[0:1],
                   bf(ffn_w_gate[0]), bf(ffn_w_up[0]), bf(ffn_w_down[0])).reshape(n_tok, d)

    w_in1 = jnp.pad(od_w_in[0], ((0, 0), (0, ODD_IN_PAD - od_w_in.shape[2])))
    (proj1,) = _norm_matmul(x2, norm_mix[1:2], bf(w_in1))
    proj1 = proj1.reshape(b, s, ODD_IN_PAD)
    att = _dsa(proj1, od_q_gain[0:1], od_k_gain[0:1], _bias_tiles(rel_bias))
    return _proj_ffn(att, bf(od_w_out[0]), x2.reshape(b, s, d), norm_ffn[1:2],
                     bf(ffn_w_gate[1]), bf(ffn_w_up[1]), bf(ffn_w_down[1]))
```

```python
import functools
import math

import jax
import jax.numpy as jnp
from jax import lax
from jax.experimental import pallas as pl
from jax.experimental.pallas import tpu as pltpu

F32 = jnp.float32
BF16 = jnp.bfloat16
I32 = jnp.int32

D_MODEL = 1024
SEQ = 2048
HEAD_DIM = 64
SB_WIDTH = 512
CONV_CH = 512
ATT_HEADS = 16
KV_HEADS = 4
GQA_REP = 4
IDX_HEADS = 8
TOPK = 256
REL_BUCKETS = 32
REL_MAX_DIST = 128
FFN_HIDDEN = 2816
EPS = 1e-6
ODD_IN_PAD = 2176

LOG2E = 1.4426950408889634
LANES = 128
VMEM_LIMIT = 56 * 1024 * 1024
NEG = -1e30
INT_MIN = -(2 ** 31)
GROUP_KEYS = 32 * 8
V_ROWS = HEAD_DIM + 16
UNDERFLOW_MARGIN = 160.0
NORM_SLACK = 1.03
KEY_NEG_INF = -(2 ** 31) + 0x7FFFFF

NT_DIMS = (((1,), (1,)), ((), ()))


def _rms(x, g):
    ms = jnp.mean(x * x, axis=-1, keepdims=True)
    return x * lax.rsqrt(ms + EPS) * g


def _norm_matmul_kernel(x_ref, g_ref, w_ref, o_ref, *, tn):
    h = _rms(x_ref[...], g_ref[...]).astype(BF16)
    n = w_ref.shape[1]
    for start in range(0, n, tn):
        size = min(tn, n - start)
        o_ref[:, start:start + size] = jnp.dot(
            h, w_ref[:, start:start + size], preferred_element_type=F32)


def _norm_matmul(x, g, w, *, tm=1024, tn=512):
    m, d = x.shape
    n = w.shape[1]
    return pl.pallas_call(
        functools.partial(_norm_matmul_kernel, tn=tn),
        out_shape=jax.ShapeDtypeStruct((m, n), F32),
        grid=(m // tm,),
        in_specs=[
            pl.BlockSpec((tm, d), lambda i: (i, 0)),
            pl.BlockSpec((1, d), lambda i: (0, 0)),
            pl.BlockSpec((d, n), lambda i: (0, 0)),
        ],
        out_specs=pl.BlockSpec((tm, n), lambda i: (i, 0)),
        compiler_params=pltpu.CompilerParams(
            dimension_semantics=("parallel",), vmem_limit_bytes=VMEM_LIMIT),
        name="norm_matmul",
    )(x, g, w)


def _sb_kernel(q_ref, k_ref, v_ref, o_ref, vt_ref, kmax_ref, *, tq, pairs):
    qi = pl.program_id(2)
    n_sub = tq // LANES
    s_len = k_ref.shape[0]
    width = pairs * LANES
    head_of = (lax.broadcasted_iota(I32, (width, LANES), 0) // HEAD_DIM
               == lax.broadcasted_iota(I32, (width, LANES), 1)).astype(BF16)

    def max_sq_norm(x):
        sq = (x * x).astype(BF16)
        return jnp.max(jnp.dot(sq, head_of, preferred_element_type=F32), axis=0, keepdims=True)

    @pl.when(qi == 0)
    def _per_sequence_setup():
        for p in range(pairs):
            for j in range(s_len // LANES):
                vt_ref[p, j] = v_ref[j * LANES:(j + 1) * LANES,
                                     p * LANES:(p + 1) * LANES].T.astype(BF16)
        kmax = jnp.zeros((1, LANES), F32)
        for r in range(0, s_len, tq):
            kmax = jnp.maximum(kmax, max_sq_norm(k_ref[r:r + tq, :]))
        kmax_ref[...] = kmax

    lane = lax.broadcasted_iota(I32, (tq, LANES), 1)
    krow = lax.broadcasted_iota(I32, (tq, tq), 0)
    qcol = lax.broadcasted_iota(I32, (tq, tq), 1)
    from_here = (qcol >= krow).astype(BF16)
    causal = krow < qcol
    q_heads = []
    for p in range(pairs):
        q = q_ref[:, p * LANES:(p + 1) * LANES] * (HEAD_DIM ** -0.5 * LOG2E)
        for h in range(2):
            in_head = (lane >= HEAD_DIM * h) & (lane < HEAD_DIM * (h + 1))
            q_heads.append(jnp.where(in_head, q, 0.0).astype(BF16))

    def block(kb, st, masked):
        start = pl.multiple_of(kb * tq, tq)
        heads = range(2 * pairs)
        kblks = [k_ref[pl.ds(start, tq), p * LANES:(p + 1) * LANES].astype(BF16)
                 for p in range(pairs)]
        v_ts = [jnp.concatenate([vt_ref[p, kb * n_sub + u] for u in range(n_sub)], axis=1)
                for p in range(pairs)]
        zs = [lax.dot_general(kblks[idx // 2], q_heads[idx], NT_DIMS, preferred_element_type=F32)
              for idx in heads]
        softplus, his, los = [], [], []
        for z in zs:
            sp = jnp.maximum(z, 0.0) + jnp.log2(1.0 + jnp.exp2(-jnp.abs(z)))
            if masked:
                sp = jnp.where(causal, sp, 0.0)
            hi = sp.astype(BF16)
            softplus.append(sp)
            his.append(hi)
            los.append((sp - hi.astype(F32)).astype(BF16))
        since = [jnp.dot(from_here, his[idx], preferred_element_type=F32)
                 + jnp.dot(from_here, los[idx], preferred_element_type=F32) for idx in heads]
        ws = []
        for idx in heads:
            w = jnp.exp2(zs[idx] - since[idx] - st[2 * idx])
            if masked:
                w = jnp.where(causal, w, 0.0)
            ws.append(w.astype(BF16))
        new = []
        for idx in heads:
            h = idx % 2
            acc = st[2 * idx + 1] + jnp.dot(v_ts[idx // 2][h * HEAD_DIM:(h + 1) * HEAD_DIM, :],
                                            ws[idx], preferred_element_type=F32)
            carry = st[2 * idx] + jnp.sum(softplus[idx], axis=0, keepdims=True)
            new += [carry, acc]
        return tuple(new)

    z_bound = jnp.sqrt(max_sq_norm(q_ref[...]) * kmax_ref[...]) * (HEAD_DIM ** -0.5 * LOG2E * NORM_SLACK)
    z_bounds = [jnp.broadcast_to(z_bound[:, idx:idx + 1], (1, tq)) for idx in range(2 * pairs)]

    def all_underflow(st):
        slack = st[0] - z_bounds[0]
        for idx in range(1, 2 * pairs):
            slack = jnp.minimum(slack, st[2 * idx] - z_bounds[idx])
        return jnp.min(slack) > UNDERFLOW_MARGIN

    def walk(state):
        n, _, st = state
        st = block(qi - 1 - n, st, False)
        return n + 1, all_underflow(st), st

    st = (jnp.zeros((1, tq), F32), jnp.zeros((HEAD_DIM, tq), F32)) * (2 * pairs)
    st = block(qi, st, True)
    _, _, st = lax.while_loop(lambda state: (state[0] < qi) & jnp.logical_not(state[1]), walk,
                              (jnp.int32(0), all_underflow(st), st))
    for p in range(pairs):
        out_t = jnp.concatenate([st[4 * p + 1], st[4 * p + 3]], axis=0)
        o_ref[p * LANES:(p + 1) * LANES, :] = out_t.astype(o_ref.dtype)


def _stickbreak(proj, *, tq=256, pairs=4):
    b, s, _ = proj.shape
    width = pairs * LANES
    n_steps = SB_WIDTH // width
    return pl.pallas_call(
        functools.partial(_sb_kernel, tq=tq, pairs=pairs),
        out_shape=jax.ShapeDtypeStruct((b, SB_WIDTH, s), BF16),
        grid=(b, n_steps, s // tq),
        in_specs=[
            pl.BlockSpec((None, tq, width), lambda bi, hp, qi: (bi, qi, hp)),
            pl.BlockSpec((None, s, width), lambda bi, hp, qi: (bi, 0, n_steps + hp)),
            pl.BlockSpec((None, s, width), lambda bi, hp, qi: (bi, 0, 2 * n_steps + hp)),
        ],
        out_specs=pl.BlockSpec((None, width, tq), lambda bi, hp, qi: (bi, hp, qi)),
        scratch_shapes=[pltpu.VMEM((pairs, s // LANES, LANES, LANES), BF16),
                        pltpu.VMEM((1, LANES), F32)],
        compiler_params=pltpu.CompilerParams(
            dimension_semantics=("parallel", "parallel", "arbitrary"),
            vmem_limit_bytes=VMEM_LIMIT),
        name="stickbreak",
    )(proj, proj, proj)


def _ffn(x, g_ref, wg_ref, wu_ref, wd_ref, act_ref, th):
    h = _rms(x, g_ref[...]).astype(BF16)
    for start in range(0, FFN_HIDDEN, th):
        stop = min(start + th, FFN_HIDDEN)
        hg = jnp.dot(h, wg_ref[:, start:stop], preferred_element_type=F32)
        hu = jnp.dot(h, wu_ref[:, start:stop], preferred_element_type=F32)
        sig = 1.0 / (1.0 + jnp.exp(-hg))
        act_ref[:, start:stop] = (hg * sig * hu).astype(BF16)
    return x + jnp.dot(act_ref[...], wd_ref[...], preferred_element_type=F32)


def _ffn_specs(d, index_map):
    resident = pl.Buffered(1)
    return [
        pl.BlockSpec((1, d), index_map),
        pl.BlockSpec((d, FFN_HIDDEN), index_map, pipeline_mode=resident),
        pl.BlockSpec((d, FFN_HIDDEN), index_map, pipeline_mode=resident),
        pl.BlockSpec((FFN_HIDDEN, d), index_map, pipeline_mode=resident),
    ]


def _mix0_ffn_kernel(a_ref, bg_ref, cg_ref, u_ref, cgh_ref, uh_ref, cw_ref, w_ref, x_ref,
                     g_ref, wg_ref, wu_ref, wd_ref, o_ref, act_ref, *, th):
    i = pl.program_id(1)
    g = cg_ref[...] * u_ref[...]
    gh = cgh_ref[...] * uh_ref[...]
    gh = jnp.where(i == 0, 0.0, gh)
    row = lax.broadcasted_iota(I32, g.shape, 0)
    g1 = jnp.where(row == 0, gh[7:8, :], pltpu.roll(g, 1, axis=0))
    g2 = pltpu.roll(g, 2, axis=0)
    g2 = jnp.where(row == 0, gh[6:7, :], jnp.where(row == 1, gh[7:8, :], g2))
    cw = cw_ref[...]
    y = bg_ref[...] * (cw[0:1, :] * g2 + cw[1:2, :] * g1 + cw[2:3, :] * g)
    acc = lax.dot_general(a_ref[...], w_ref[0:SB_WIDTH, :], (((0,), (0,)), ((), ())),
                          preferred_element_type=F32)
    acc = acc + jnp.dot(y.astype(BF16), w_ref[SB_WIDTH:, :], preferred_element_type=F32)
    o_ref[...] = _ffn(x_ref[...] + acc, g_ref, wg_ref, wu_ref, wd_ref, act_ref, th)


def _mix0_ffn(a_out, proj, conv_w, w_out, x, g, wg, wu, wd, *, tm=512, th=256):
    b, s, d = x.shape
    halo = lambda col: (lambda bi, i: (bi, jnp.maximum(i * (tm // 8) - 1, 0), col))
    return pl.pallas_call(
        functools.partial(_mix0_ffn_kernel, th=th),
        out_shape=jax.ShapeDtypeStruct((b, s, d), F32),
        grid=(b, s // tm),
        in_specs=[
            pl.BlockSpec((None, SB_WIDTH, tm), lambda bi, i: (bi, 0, i)),
            pl.BlockSpec((None, tm, CONV_CH), lambda bi, i: (bi, i, 3)),
            pl.BlockSpec((None, tm, CONV_CH), lambda bi, i: (bi, i, 4)),
            pl.BlockSpec((None, tm, CONV_CH), lambda bi, i: (bi, i, 5)),
            pl.BlockSpec((None, 8, CONV_CH), halo(4)),
            pl.BlockSpec((None, 8, CONV_CH), halo(5)),
            pl.BlockSpec((3, CONV_CH), lambda bi, i: (0, 0)),
            pl.BlockSpec((SB_WIDTH + CONV_CH, d), lambda bi, i: (0, 0)),
            pl.BlockSpec((None, tm, d), lambda bi, i: (bi, i, 0)),
        ] + _ffn_specs(d, lambda bi, i: (0, 0)),
        out_specs=pl.BlockSpec((None, tm, d), lambda bi, i: (bi, i, 0)),
        scratch_shapes=[pltpu.VMEM((tm, FFN_HIDDEN), BF16)],
        compiler_params=pltpu.CompilerParams(
            dimension_semantics=("parallel", "arbitrary"), vmem_limit_bytes=VMEM_LIMIT),
        name="mix0_ffn",
    )(a_out, proj, proj, proj, proj, proj, conv_w, w_out, x, g, wg, wu, wd)


def _proj_ffn_kernel(at_ref, w_ref, x_ref, g_ref, wg_ref, wu_ref, wd_ref, o_ref, act_ref, *, th):
    x1 = x_ref[...] + lax.dot_general(at_ref[...], w_ref[...], (((0,), (0,)), ((), ())),
                                      preferred_element_type=F32)
    o_ref[...] = _ffn(x1, g_ref, wg_ref, wu_ref, wd_ref, act_ref, th)


def _proj_ffn(a_t, w, x, g, wg, wu, wd, *, tm=1024, th=256):
    b, k, s = a_t.shape
    d = w.shape[1]
    return pl.pallas_call(
        functools.partial(_proj_ffn_kernel, th=th),
        out_shape=jax.ShapeDtypeStruct((b, s, d), F32),
        grid=(b, s // tm),
        in_specs=[
            pl.BlockSpec((None, k, tm), lambda bi, i: (bi, 0, i)),
            pl.BlockSpec((k, d), lambda bi, i: (0, 0)),
            pl.BlockSpec((None, tm, d), lambda bi, i: (bi, i, 0)),
        ] + _ffn_specs(d, lambda bi, i: (0, 0)),
        out_specs=pl.BlockSpec((None, tm, d), lambda bi, i: (bi, i, 0)),
        scratch_shapes=[pltpu.VMEM((tm, FFN_HIDDEN), BF16)],
        compiler_params=pltpu.CompilerParams(
            dimension_semantics=("parallel", "parallel"), vmem_limit_bytes=VMEM_LIMIT),
        name="proj_ffn",
    )(a_t, w, x, g, wg, wu, wd)


BIAS_ROWS = 3 * LANES


def _bias_tile_kernel(rb_ref, o_ref):
    c = lax.broadcasted_iota(I32, (BIAS_ROWS, LANES), 0)
    t = lax.broadcasted_iota(I32, (BIAS_ROWS, LANES), 1)
    dist = jnp.maximum(LANES + t - c, 0)
    exact = REL_BUCKETS // 2
    d_f = jnp.maximum(dist, 1).astype(F32)
    large = exact + (jnp.log(d_f / exact) / math.log(REL_MAX_DIST / exact)
                     * (REL_BUCKETS - exact)).astype(I32)
    large = jnp.minimum(large, REL_BUCKETS - 1)
    bucket = jnp.where(dist < exact, dist, large)

    def one_head(h, _):
        out = jnp.zeros((BIAS_ROWS, LANES), F32)
        for b in range(REL_BUCKETS):
            out = jnp.where(bucket == b, rb_ref[b, h], out)
        o_ref[h] = (out - rb_ref[REL_BUCKETS - 1, h]) * LOG2E
        return 0

    lax.fori_loop(0, ATT_HEADS, one_head, 0)


def _bias_tiles(rel_bias):
    return pl.pallas_call(
        _bias_tile_kernel,
        out_shape=jax.ShapeDtypeStruct((ATT_HEADS, BIAS_ROWS, LANES), F32),
        in_specs=[pl.BlockSpec(memory_space=pltpu.SMEM)],
        out_specs=pl.BlockSpec(memory_space=pltpu.VMEM),
        name="rel_bias_tiles",
    )(rel_bias)


def _dsa_kernel(q_ref, k_ref, v_ref, qi_ref, wq_ref, kik_ref, qg_ref, kg_ref, bt_ref, seg_ref,
                segt_ref, o_ref, kn_ref, vt_ref, kib_ref, key_ref, planes_ref, mfar_ref, s_ref, qs_ref, *, tq, chunk):
    i = pl.program_id(1)
    s_len = k_ref.shape[0]
    n_chunk = (i + chunk) // chunk
    n_far = (i - 1 + chunk - 1) // chunk
    w0 = jnp.maximum(i - 1, 0)

    @pl.when(i == 0)
    def _prepare_keys():
        kg = kg_ref[...]
        for g in range(KV_HEADS):
            kn = _rms(k_ref[:, g * HEAD_DIM:(g + 1) * HEAD_DIM], kg).astype(BF16)
            kn_ref[g] = jnp.concatenate([kn, kn], axis=1)
        for j in range(s_len // LANES):
            for pair in range(KV_HEADS // 2):
                v_t = v_ref[j * LANES:(j + 1) * LANES, pair * LANES:(pair + 1) * LANES].T
                vt_ref[2 * pair, j, 0:HEAD_DIM] = v_t[0:HEAD_DIM].astype(BF16)
                vt_ref[2 * pair + 1, j, 0:HEAD_DIM] = v_t[HEAD_DIM:].astype(BF16)
        vt_ref[:, :, HEAD_DIM:V_ROWS] = jnp.ones((KV_HEADS, s_len // LANES, V_ROWS - HEAD_DIM, LANES),
                                                 BF16)
        ki = kik_ref[:, 0:HEAD_DIM].astype(BF16)
        kib_ref[...] = jnp.concatenate([ki, ki], axis=1)
        planes_ref[...] = jnp.zeros(planes_ref.shape, I32)

    ck = chunk * LANES
    krow = lax.broadcasted_iota(I32, (ck, tq), 0)
    tpos = i * tq + lax.broadcasted_iota(I32, (ck, tq), 1)
    low_half = lax.broadcasted_iota(I32, (tq, LANES), 1) < HEAD_DIM

    def split_pair(x):
        return [jnp.where(low_half, x, 0.0).astype(BF16), jnp.where(low_half, 0.0, x).astype(BF16)]

    q = q_ref[...]
    sq = q * q
    sq_hi = sq.astype(BF16)
    sq_lo = (sq - sq_hi.astype(F32)).astype(BF16)
    seg = seg_ref[...]
    ssum = (jnp.dot(sq_hi, seg, preferred_element_type=F32)
            + jnp.dot(sq_lo, seg, preferred_element_type=F32))
    rinv = lax.rsqrt(ssum * (1.0 / HEAD_DIM) + EPS)
    pieces, rest = [], rinv
    for _ in range(2):
        piece = rest.astype(BF16)
        pieces.append(piece)
        rest = rest - piece.astype(F32)
    segt = segt_ref[...]
    rinv_b = sum(jnp.dot(piece, segt, preferred_element_type=F32) for piece in pieces)
    qn = q * rinv_b * (qg_ref[...] * (HEAD_DIM ** -0.5 * LOG2E))
    for g in range(KV_HEADS):
        rows = []
        for pair in range(2 * g, 2 * g + 2):
            rows += split_pair(qn[:, pair * LANES:(pair + 1) * LANES])
        qs_ref[g] = jnp.concatenate(rows, axis=0)

    w_t = wq_ref[...].T
    qi_pairs = [jnp.concatenate(split_pair(qi_ref[:, pair * LANES:(pair + 1) * LANES] * 0.125), axis=0)
                for pair in range(IDX_HEADS // 2)]
    w_heads = [w_t[HEAD_DIM + h:HEAD_DIM + h + 1, :] * (IDX_HEADS ** -0.5)
               for h in range(IDX_HEADS)]

    def bit_planes(c):
        for half in range(ck // GROUP_KEYS):
            words = [key_ref[c, pl.ds(half * GROUP_KEYS + 8 * r, 8), :] for r in range(32)]
            for shift, keep in ((16, 0x0000FFFF), (8, 0x00FF00FF), (4, 0x0F0F0F0F),
                                (2, 0x33333333), (1, 0x55555555)):
                amount = jnp.full((8, tq), shift, I32)
                for r in range(32):
                    if r & shift == 0:
                        a, b = words[r], words[r + shift]
                        t = (lax.shift_right_logical(a, amount) ^ b) & keep
                        words[r] = a ^ lax.shift_left(t, amount)
                        words[r + shift] = b ^ t
            words[31] = ~words[31]
            for b in range(32):
                planes_ref[c * (ck // GROUP_KEYS) + half, b] = words[b]

    def score_chunk(c):
        kt = kib_ref[pl.ds(pl.multiple_of(c * ck, ck), ck), :]
        dots = [lax.dot_general(kt, qi_pair, NT_DIMS, preferred_element_type=F32)
                for qi_pair in qi_pairs]
        sc = jnp.zeros((ck, tq), F32)
        for h in range(IDX_HEADS):
            d = dots[h // 2][:, (h % 2) * tq:(h % 2 + 1) * tq]
            sc = sc + w_heads[h] * jnp.maximum(d, 0.0)
        sc = jnp.where(sc == 0.0, 0.0, sc)
        sc = jnp.where(c * ck + krow <= tpos, sc, -jnp.inf)
        bits = pltpu.bitcast(sc, I32)
        key_ref[c] = bits ^ ((bits >> 31) & 0x7FFFFFFF)

    score_chunk(0)

    def scan(c, _):
        bit_planes(c - 1)
        score_chunk(c)
        return 0

    lax.fori_loop(1, n_chunk, scan, 0)

    def count_keys(preds):
        def body(c, cs):
            key = key_ref[c]
            return tuple(
                cnt + jnp.sum(pred(c, key).astype(I32).reshape(ck // 8, 8, tq), axis=0)
                for cnt, pred in zip(cs, preds))
        cs = lax.fori_loop(0, n_chunk, body,
                           tuple(jnp.zeros((8, tq), I32) for _ in preds))
        return [jnp.sum(cnt, axis=0, keepdims=True) for cnt in cs]

    n_groups = s_len // GROUP_KEYS

    def sliced_search():
        bit_planes(n_chunk - 1)
        live = n_chunk * (ck // GROUP_KEYS)

        def two_bits(step, state):
            eqs, n_gt, thr_u = state
            b0 = 30 - 2 * step
            s11, s10, s01, s00 = [], [], [], []
            for g in range(n_groups):
                with_hi = eqs[g] & planes_ref[g, b0 + 1]
                without_hi = eqs[g] ^ with_hi
                low = planes_ref[g, b0]
                s11.append(with_hi & low)
                s10.append(with_hi ^ s11[g])
                s01.append(without_hi & low)
                s00.append(without_hi ^ s01[g])

            def count(sets):
                total = lax.population_count(sets[0])
                for g in range(1, n_groups):
                    total = total + lax.population_count(sets[g])
                return jnp.sum(total, axis=0, keepdims=True)

            up11 = n_gt + count(s11)
            up10 = up11 + count(s10)
            up01 = up10 + count(s01)
            in11 = up11 >= TOPK
            in10 = jnp.logical_not(in11) & (up10 >= TOPK)
            in01 = jnp.logical_not(in11 | in10) & (up01 >= TOPK)
            n_gt = jnp.where(in11, n_gt, jnp.where(in10, up11, jnp.where(in01, up10, up01)))
            bits = jnp.where(in11, 3, jnp.where(in10, 2, jnp.where(in01, 1, 0)))
            thr_u = thr_u | jnp.left_shift(bits, b0)
            m11, m10, m01 = (jnp.broadcast_to(m, (8, tq)) for m in (in11, in10, in01))
            eqs = tuple(jnp.where(m11, s11[g], jnp.where(m10, s10[g], jnp.where(m01, s01[g], s00[g])))
                        for g in range(n_groups))
            return eqs, n_gt, thr_u

        eqs = tuple(jnp.full((8, tq), jnp.where(g < live, -1, 0), I32) for g in range(n_groups))
        zero = jnp.zeros((1, tq), I32)
        eqs, n_gt, thr_u = lax.fori_loop(0, 16, two_bits, (eqs, zero, zero))
        n_eq = lax.population_count(eqs[0])
        for g in range(1, n_groups):
            n_eq = n_eq + lax.population_count(eqs[g])
        return thr_u ^ INT_MIN, n_gt, jnp.sum(n_eq, axis=0, keepdims=True)

    thr, n_gt, n_eq = lax.cond(
        (i + 1) * tq <= TOPK,
        lambda: (jnp.full((1, tq), KEY_NEG_INF, I32), jnp.zeros((1, tq), I32), jnp.zeros((1, tq), I32)),
        sliced_search)
    thr_b = jnp.broadcast_to(thr, (ck, tq))
    need = TOPK - n_gt
    tied = (n_eq > need) & (thr != KEY_NEG_INF)
    any_tied = jnp.max(tied.astype(I32)) > 0

    def tie_bit(b, cut):
        cand = cut + lax.shift_left(jnp.int32(1), 10 - b)
        cand_b = jnp.broadcast_to(cand, (ck, tq))
        (cnt,) = count_keys([lambda c, key: (key == thr_b) & (c * ck + krow < cand_b)])
        return jnp.where(cnt < need, cand, cut)

    cut = lax.cond(any_tied,
                   lambda: lax.fori_loop(0, 11, tie_bit, jnp.zeros((1, tq), I32)),
                   lambda: jnp.full((1, tq), s_len, I32))

    def selection_mask(key, spos, tpos):
        thr_k = jnp.broadcast_to(thr, key.shape)
        cut_k = jnp.broadcast_to(cut, key.shape)
        sel = (key > thr_k) | ((key == thr_k) & (spos <= cut_k))
        return jnp.where(sel & (spos <= tpos), 0.0, NEG)

    def mask_chunk(c, _):
        spos = c * ck + krow
        madd = selection_mask(key_ref[c], spos, tpos)
        mfar_ref[c] = jnp.where(spos < (i - 1) * LANES, madd, NEG)
        return 0

    lax.fori_loop(0, n_far, mask_chunk, 0)

    def tile_mask(tile):
        c, u = tile // chunk, tile % chunk
        key = key_ref[c, pl.ds(pl.multiple_of(u * LANES, LANES), LANES), :]
        krow_t = lax.broadcasted_iota(I32, (LANES, tq), 0)
        tpos_t = i * tq + lax.broadcasted_iota(I32, (LANES, tq), 1)
        return selection_mask(key, tile * LANES + krow_t, tpos_t)

    qs = [qs_ref[g] for g in range(KV_HEADS)]

    groups = range(KV_HEADS)

    def attend(sts, n_keys, first_tile, add_fn):
        key_start = pl.multiple_of(first_tile * LANES, LANES)
        block_max = []
        for g in groups:
            kt = kn_ref[g, pl.ds(key_start, n_keys), :]
            add = jnp.concatenate([add_fn(g * GQA_REP + r) for r in range(GQA_REP)], axis=1)
            s = lax.dot_general(kt, qs[g], NT_DIMS, preferred_element_type=F32) + add
            s_ref[g, 0:n_keys, :] = s
            block_max.append(jnp.max(s, axis=0, keepdims=True))
        new = []
        for g in groups:
            m, acc = sts[g]
            m_new = jnp.maximum(m, block_max[g])
            p = jnp.exp2(s_ref[g, 0:n_keys, :] - m_new).astype(BF16)
            v_t = jnp.concatenate([vt_ref[g, first_tile + u] for u in range(n_keys // LANES)],
                                  axis=1)
            new.append((m_new, jnp.exp2(m - m_new) * acc
                        + jnp.dot(v_t, p, preferred_element_type=F32)))
        return tuple(new)

    def far_chunk(c, sts):
        madd = mfar_ref[c]
        return attend(sts, ck, c * chunk, lambda head: madd)

    init = (jnp.full((1, GQA_REP * tq), NEG, F32), jnp.zeros((V_ROWS, GQA_REP * tq), F32))
    sts = lax.fori_loop(0, n_far, far_chunk, (init,) * KV_HEADS)

    win_mask = jnp.concatenate([tile_mask(w0), tile_mask(w0 + 1)], axis=0)
    bias_row = pl.multiple_of(jnp.where(i == 0, LANES, 0), LANES)
    sts = attend(sts, 2 * LANES, w0,
                 lambda head: bt_ref[head, pl.ds(bias_row, 2 * LANES), :] + win_mask)
    for g in range(KV_HEADS):
        acc = sts[g][1]
        out_t = acc[0:HEAD_DIM] / acc[HEAD_DIM:HEAD_DIM + 1]
        for pair in range(GQA_REP // 2):
            two = jnp.concatenate(
                [out_t[:, (2 * pair) * tq:(2 * pair + 1) * tq],
                 out_t[:, (2 * pair + 1) * tq:(2 * pair + 2) * tq]], axis=0)
            col = (g * GQA_REP + 2 * pair) * HEAD_DIM
            o_ref[col:col + LANES, :] = two.astype(o_ref.dtype)


def _dsa(proj, q_gain, k_gain, bias_tiles, *, tq=128, chunk=4):
    b, s, _ = proj.shape
    n_t = s // LANES
    att_width = ATT_HEADS * HEAD_DIM
    seg = (jnp.arange(att_width)[:, None] // HEAD_DIM == jnp.arange(LANES)[None, :]).astype(BF16)
    q_gain_all = jnp.tile(q_gain, (1, ATT_HEADS))
    return pl.pallas_call(
        functools.partial(_dsa_kernel, tq=tq, chunk=chunk),
        out_shape=jax.ShapeDtypeStruct((b, att_width, s), BF16),
        grid=(b, s // tq),
        in_specs=[
            pl.BlockSpec((None, tq, 1024), lambda bi, i: (bi, i, 0)),
            pl.BlockSpec((None, s, 256), lambda bi, i: (bi, 0, 4)),
            pl.BlockSpec((None, s, 256), lambda bi, i: (bi, 0, 5)),
            pl.BlockSpec((None, tq, 512), lambda bi, i: (bi, i, 3)),
            pl.BlockSpec((None, tq, LANES), lambda bi, i: (bi, i, 16)),
            pl.BlockSpec((None, s, LANES), lambda bi, i: (bi, 0, 16)),
            pl.BlockSpec((1, att_width), lambda bi, i: (0, 0)),
            pl.BlockSpec((1, HEAD_DIM), lambda bi, i: (0, 0)),
            pl.BlockSpec((ATT_HEADS, BIAS_ROWS, LANES), lambda bi, i: (0, 0, 0)),
            pl.BlockSpec((att_width, LANES), lambda bi, i: (0, 0)),
            pl.BlockSpec((LANES, att_width), lambda bi, i: (0, 0)),
        ],
        out_specs=pl.BlockSpec((None, att_width, tq), lambda bi, i: (bi, 0, i)),
        scratch_shapes=[
            pltpu.VMEM((KV_HEADS, s, LANES), BF16),
            pltpu.VMEM((KV_HEADS, n_t, V_ROWS, LANES), BF16),
            pltpu.VMEM((s, LANES), BF16),
            pltpu.VMEM((n_t // chunk, chunk * LANES, tq), I32),
            pltpu.VMEM((s // GROUP_KEYS, 32, 8, tq), I32),
            pltpu.VMEM((n_t // chunk, chunk * LANES, tq), F32),
            pltpu.VMEM((KV_HEADS, chunk * LANES, GQA_REP * tq), F32),
            pltpu.VMEM((KV_HEADS, GQA_REP * tq, LANES), BF16),
        ],
        compiler_params=pltpu.CompilerParams(
            dimension_semantics=("parallel", "arbitrary"), vmem_limit_bytes=VMEM_LIMIT),
        name="dsa_attention",
    )(proj, proj, proj, proj, proj, proj, q_gain_all, k_gain, bias_tiles, seg, seg.T)


def kernel(x, norm_mix, norm_ffn, ev_w_in, ev_conv_w, ev_w_out, od_w_in, od_q_gain,
           od_k_gain, od_w_out, rel_bias, ffn_w_gate, ffn_w_up, ffn_w_down):
    b, s, d = x.shape
    n_tok = b * s
    bf = lambda w: w.astype(BF16)

    proj0 = _norm_matmul(x.reshape(n_tok, d), norm_mix[0:1], bf(ev_w_in[0]))
    proj0 = proj0.reshape(b, s, -1)
    a_out = _stickbreak(proj0)
    x2 = _mix0_ffn(a_out, proj0, ev_conv_w[0], bf(ev_w_out[0]), x, norm_ffn[0:1],
                   bf(ffn_w_gate[0]), bf(ffn_w_up[0]), bf(ffn_w_down[0])).reshape(n_tok, d)

    w_in1 = jnp.pad(od_w_in[0], ((0, 0), (0, ODD_IN_PAD - od_w_in.shape[2])))
    proj1 = _norm_matmul(x2, norm_mix[1:2], bf(w_in1)).reshape(b, s, ODD_IN_PAD)
    att = _dsa(proj1, od_q_gain[0:1], od_k_gain[0:1], _bias_tiles(rel_bias))
    return _proj_ffn(att, bf(od_w_out[0]), x2.reshape(b, s, d), norm_ffn[1:2],
                     bf(ffn_w_gate[1]), bf(ffn_w_up[1]), bf(ffn_w_down[1]))
```

```python
import functools
import math

import jax
import jax.numpy as jnp
from jax import lax
from jax.experimental import pallas as pl
from jax.experimental.pallas import tpu as pltpu

F32 = jnp.float32
BF16 = jnp.bfloat16
I32 = jnp.int32

D_MODEL = 1024
SEQ = 2048
HEAD_DIM = 64
SB_WIDTH = 512
CONV_CH = 512
ATT_HEADS = 16
KV_HEADS = 4
GQA_REP = 4
IDX_HEADS = 8
TOPK = 256
REL_BUCKETS = 32
REL_MAX_DIST = 128
FFN_HIDDEN = 2816
EPS = 1e-6
ODD_IN_PAD = 2176

LOG2E = 1.4426950408889634
LANES = 128
VMEM_LIMIT = 56 * 1024 * 1024
VMEM_LIMIT_LARGE = 60 * 1024 * 1024
NEG = -1e30
INT_MIN = -(2 ** 31)
GROUP_KEYS = 32 * 8
V_ROWS = HEAD_DIM + 16
UNDERFLOW_MARGIN = 160.0
NORM_SLACK = 1.03
KEY_NEG_INF = -(2 ** 31) + 0x7FFFFF

NT_DIMS = (((1,), (1,)), ((), ()))


def _rms(x, g):
    ms = jnp.mean(x * x, axis=-1, keepdims=True)
    return x * lax.rsqrt(ms + EPS) * g


def _norm_matmul_kernel(x_ref, g_ref, w_ref, o_ref, *, tn):
    h = _rms(x_ref[...], g_ref[...]).astype(BF16)
    n = w_ref.shape[1]
    for start in range(0, n, tn):
        size = min(tn, n - start)
        o_ref[:, start:start + size] = jnp.dot(
            h, w_ref[:, start:start + size], preferred_element_type=F32)


def _norm_matmul(x, g, w, *, tm=1024, tn=512):
    m, d = x.shape
    n = w.shape[1]
    return pl.pallas_call(
        functools.partial(_norm_matmul_kernel, tn=tn),
        out_shape=jax.ShapeDtypeStruct((m, n), F32),
        grid=(m // tm,),
        in_specs=[
            pl.BlockSpec((tm, d), lambda i: (i, 0)),
            pl.BlockSpec((1, d), lambda i: (0, 0)),
            pl.BlockSpec((d, n), lambda i: (0, 0)),
        ],
        out_specs=pl.BlockSpec((tm, n), lambda i: (i, 0)),
        compiler_params=pltpu.CompilerParams(
            dimension_semantics=("parallel",), vmem_limit_bytes=VMEM_LIMIT),
        name="norm_matmul",
    )(x, g, w)


def _sb_kernel(q_ref, k_ref, v_ref, o_ref, vt_ref, kmax_ref, *, tq, pairs):
    qi = pl.program_id(2)
    n_sub = tq // LANES
    s_len = k_ref.shape[0]
    width = pairs * LANES
    head_of = (lax.broadcasted_iota(I32, (width, LANES), 0) // HEAD_DIM
               == lax.broadcasted_iota(I32, (width, LANES), 1)).astype(BF16)

    def max_sq_norm(x):
        sq = (x * x).astype(BF16)
        return jnp.max(jnp.dot(sq, head_of, preferred_element_type=F32), axis=0, keepdims=True)

    @pl.when(qi == 0)
    def _per_sequence_setup():
        for p in range(pairs):
            for j in range(s_len // LANES):
                vt_ref[p, j] = v_ref[j * LANES:(j + 1) * LANES,
                                     p * LANES:(p + 1) * LANES].T.astype(BF16)
        kmax = jnp.zeros((1, LANES), F32)
        for r in range(0, s_len, tq):
            kmax = jnp.maximum(kmax, max_sq_norm(k_ref[r:r + tq, :]))
        kmax_ref[...] = kmax

    lane = lax.broadcasted_iota(I32, (tq, LANES), 1)
    krow = lax.broadcasted_iota(I32, (tq, tq), 0)
    qcol = lax.broadcasted_iota(I32, (tq, tq), 1)
    from_here = (qcol >= krow).astype(BF16)
    causal = krow < qcol
    q_heads = []
    for p in range(pairs):
        q = q_ref[:, p * LANES:(p + 1) * LANES] * (HEAD_DIM ** -0.5 * LOG2E)
        for h in range(2):
            in_head = (lane >= HEAD_DIM * h) & (lane < HEAD_DIM * (h + 1))
            q_heads.append(jnp.where(in_head, q, 0.0).astype(BF16))

    def block(kb, st, masked):
        start = pl.multiple_of(kb * tq, tq)
        heads = range(2 * pairs)
        kblks = [k_ref[pl.ds(start, tq), p * LANES:(p + 1) * LANES].astype(BF16)
                 for p in range(pairs)]
        v_ts = [jnp.concatenate([vt_ref[p, kb * n_sub + u] for u in range(n_sub)], axis=1)
                for p in range(pairs)]
        zs = [lax.dot_general(kblks[idx // 2], q_heads[idx], NT_DIMS, preferred_element_type=F32)
              for idx in heads]
        softplus, his, los = [], [], []
        for z in zs:
            sp = jnp.maximum(z, 0.0) + jnp.log2(1.0 + jnp.exp2(-jnp.abs(z)))
            if masked:
                sp = jnp.where(causal, sp, 0.0)
            hi = sp.astype(BF16)
            softplus.append(sp)
            his.append(hi)
            los.append((sp - hi.astype(F32)).astype(BF16))
        since = [jnp.dot(from_here, his[idx], preferred_element_type=F32)
                 + jnp.dot(from_here, los[idx], preferred_element_type=F32) for idx in heads]
        ws = []
        for idx in heads:
            w = jnp.exp2(zs[idx] - since[idx] - st[2 * idx])
            if masked:
                w = jnp.where(causal, w, 0.0)
            ws.append(w.astype(BF16))
        new = []
        for idx in heads:
            h = idx % 2
            acc = st[2 * idx + 1] + jnp.dot(v_ts[idx // 2][h * HEAD_DIM:(h + 1) * HEAD_DIM, :],
                                            ws[idx], preferred_element_type=F32)
            carry = st[2 * idx] + jnp.sum(softplus[idx], axis=0, keepdims=True)
            new += [carry, acc]
        return tuple(new)

    z_bound = jnp.sqrt(max_sq_norm(q_ref[...]) * kmax_ref[...]) * (HEAD_DIM ** -0.5 * LOG2E * NORM_SLACK)
    z_bounds = [jnp.broadcast_to(z_bound[:, idx:idx + 1], (1, tq)) for idx in range(2 * pairs)]

    def all_underflow(st):
        slack = st[0] - z_bounds[0]
        for idx in range(1, 2 * pairs):
            slack = jnp.minimum(slack, st[2 * idx] - z_bounds[idx])
        return jnp.min(slack) > UNDERFLOW_MARGIN

    def walk(state):
        n, _, st = state
        st = block(qi - 1 - n, st, False)
        return n + 1, all_underflow(st), st

    st = (jnp.zeros((1, tq), F32), jnp.zeros((HEAD_DIM, tq), F32)) * (2 * pairs)
    st = block(qi, st, True)
    _, _, st = lax.while_loop(lambda state: (state[0] < qi) & jnp.logical_not(state[1]), walk,
                              (jnp.int32(0), all_underflow(st), st))
    for p in range(pairs):
        out_t = jnp.concatenate([st[4 * p + 1], st[4 * p + 3]], axis=0)
        o_ref[p * LANES:(p + 1) * LANES, :] = out_t.astype(o_ref.dtype)


def _stickbreak(proj, *, tq=256, pairs=4):
    b, s, _ = proj.shape
    width = pairs * LANES
    n_steps = SB_WIDTH // width
    return pl.pallas_call(
        functools.partial(_sb_kernel, tq=tq, pairs=pairs),
        out_shape=jax.ShapeDtypeStruct((b, SB_WIDTH, s), BF16),
        grid=(b, n_steps, s // tq),
        in_specs=[
            pl.BlockSpec((None, tq, width), lambda bi, hp, qi: (bi, qi, hp)),
            pl.BlockSpec((None, s, width), lambda bi, hp, qi: (bi, 0, n_steps + hp)),
            pl.BlockSpec((None, s, width), lambda bi, hp, qi: (bi, 0, 2 * n_steps + hp)),
        ],
        out_specs=pl.BlockSpec((None, width, tq), lambda bi, hp, qi: (bi, hp, qi)),
        scratch_shapes=[pltpu.VMEM((pairs, s // LANES, LANES, LANES), BF16),
                        pltpu.VMEM((1, LANES), F32)],
        compiler_params=pltpu.CompilerParams(
            dimension_semantics=("parallel", "parallel", "arbitrary"),
            vmem_limit_bytes=VMEM_LIMIT),
        name="stickbreak",
    )(proj, proj, proj)


def _ffn(x, g_ref, wg_ref, wu_ref, wd_ref, act_ref, th):
    h = _rms(x, g_ref[...]).astype(BF16)
    for start in range(0, FFN_HIDDEN, th):
        stop = min(start + th, FFN_HIDDEN)
        hg = jnp.dot(h, wg_ref[:, start:stop], preferred_element_type=F32)
        hu = jnp.dot(h, wu_ref[:, start:stop], preferred_element_type=F32)
        sig = 1.0 / (1.0 + jnp.exp(-hg))
        act_ref[:, start:stop] = (hg * sig * hu).astype(BF16)
    return x + jnp.dot(act_ref[...], wd_ref[...], preferred_element_type=F32)


def _ffn_specs(d, index_map):
    resident = pl.Buffered(1)
    return [
        pl.BlockSpec((1, d), index_map),
        pl.BlockSpec((d, FFN_HIDDEN), index_map, pipeline_mode=resident),
        pl.BlockSpec((d, FFN_HIDDEN), index_map, pipeline_mode=resident),
        pl.BlockSpec((FFN_HIDDEN, d), index_map, pipeline_mode=resident),
    ]


def _mix0_ffn_kernel(a_ref, bg_ref, cg_ref, u_ref, cgh_ref, uh_ref, cw_ref, w_ref, x_ref,
                     g_ref, wg_ref, wu_ref, wd_ref, o_ref, act_ref, *, th):
    i = pl.program_id(1)
    g = cg_ref[...] * u_ref[...]
    gh = cgh_ref[...] * uh_ref[...]
    gh = jnp.where(i == 0, 0.0, gh)
    row = lax.broadcasted_iota(I32, g.shape, 0)
    g1 = jnp.where(row == 0, gh[7:8, :], pltpu.roll(g, 1, axis=0))
    g2 = pltpu.roll(g, 2, axis=0)
    g2 = jnp.where(row == 0, gh[6:7, :], jnp.where(row == 1, gh[7:8, :], g2))
    cw = cw_ref[...]
    y = bg_ref[...] * (cw[0:1, :] * g2 + cw[1:2, :] * g1 + cw[2:3, :] * g)
    acc = lax.dot_general(a_ref[...], w_ref[0:SB_WIDTH, :], (((0,), (0,)), ((), ())),
                          preferred_element_type=F32)
    acc = acc + jnp.dot(y.astype(BF16), w_ref[SB_WIDTH:, :], preferred_element_type=F32)
    o_ref[...] = _ffn(x_ref[...] + acc, g_ref, wg_ref, wu_ref, wd_ref, act_ref, th)


def _mix0_ffn(a_out, proj, conv_w, w_out, x, g, wg, wu, wd, *, tm=1024, th=256):
    b, s, d = x.shape
    halo = lambda col: (lambda bi, i: (bi, jnp.maximum(i * (tm // 8) - 1, 0), col))
    return pl.pallas_call(
        functools.partial(_mix0_ffn_kernel, th=th),
        out_shape=jax.ShapeDtypeStruct((b, s, d), F32),
        grid=(b, s // tm),
        in_specs=[
            pl.BlockSpec((None, SB_WIDTH, tm), lambda bi, i: (bi, 0, i)),
            pl.BlockSpec((None, tm, CONV_CH), lambda bi, i: (bi, i, 3)),
            pl.BlockSpec((None, tm, CONV_CH), lambda bi, i: (bi, i, 4)),
            pl.BlockSpec((None, tm, CONV_CH), lambda bi, i: (bi, i, 5)),
            pl.BlockSpec((None, 8, CONV_CH), halo(4)),
            pl.BlockSpec((None, 8, CONV_CH), halo(5)),
            pl.BlockSpec((3, CONV_CH), lambda bi, i: (0, 0)),
            pl.BlockSpec((SB_WIDTH + CONV_CH, d), lambda bi, i: (0, 0), pipeline_mode=pl.Buffered(1)),
            pl.BlockSpec((None, tm, d), lambda bi, i: (bi, i, 0)),
        ] + _ffn_specs(d, lambda bi, i: (0, 0)),
        out_specs=pl.BlockSpec((None, tm, d), lambda bi, i: (bi, i, 0)),
        scratch_shapes=[pltpu.VMEM((tm, FFN_HIDDEN), BF16)],
        compiler_params=pltpu.CompilerParams(
            dimension_semantics=("parallel", "arbitrary"), vmem_limit_bytes=VMEM_LIMIT_LARGE),
        name="mix0_ffn",
    )(a_out, proj, proj, proj, proj, proj, conv_w, w_out, x, g, wg, wu, wd)


def _proj_ffn_kernel(at_ref, w_ref, x_ref, g_ref, wg_ref, wu_ref, wd_ref, o_ref, act_ref, *, th):
    x1 = x_ref[...] + lax.dot_general(at_ref[...], w_ref[...], (((0,), (0,)), ((), ())),
                                      preferred_element_type=F32)
    o_ref[...] = _ffn(x1, g_ref, wg_ref, wu_ref, wd_ref, act_ref, th)


def _proj_ffn(a_t, w, x, g, wg, wu, wd, *, tm=1024, th=256):
    b, k, s = a_t.shape
    d = w.shape[1]
    return pl.pallas_call(
        functools.partial(_proj_ffn_kernel, th=th),
        out_shape=jax.ShapeDtypeStruct((b, s, d), F32),
        grid=(b, s // tm),
        in_specs=[
            pl.BlockSpec((None, k, tm), lambda bi, i: (bi, 0, i)),
            pl.BlockSpec((k, d), lambda bi, i: (0, 0)),
            pl.BlockSpec((None, tm, d), lambda bi, i: (bi, i, 0)),
        ] + _ffn_specs(d, lambda bi, i: (0, 0)),
        out_specs=pl.BlockSpec((None, tm, d), lambda bi, i: (bi, i, 0)),
        scratch_shapes=[pltpu.VMEM((tm, FFN_HIDDEN), BF16)],
        compiler_params=pltpu.CompilerParams(
            dimension_semantics=("parallel", "parallel"), vmem_limit_bytes=VMEM_LIMIT),
        name="proj_ffn",
    )(a_t, w, x, g, wg, wu, wd)


BIAS_ROWS = 3 * LANES


def _bias_tile_kernel(rb_ref, o_ref):
    c = lax.broadcasted_iota(I32, (BIAS_ROWS, LANES), 0)
    t = lax.broadcasted_iota(I32, (BIAS_ROWS, LANES), 1)
    dist = jnp.maximum(LANES + t - c, 0)
    exact = REL_BUCKETS // 2
    d_f = jnp.maximum(dist, 1).astype(F32)
    large = exact + (jnp.log(d_f / exact) / math.log(REL_MAX_DIST / exact)
                     * (REL_BUCKETS - exact)).astype(I32)
    large = jnp.minimum(large, REL_BUCKETS - 1)
    bucket = jnp.where(dist < exact, dist, large)

    def one_head(h, _):
        out = jnp.zeros((BIAS_ROWS, LANES), F32)
        for b in range(REL_BUCKETS):
            out = jnp.where(bucket == b, rb_ref[b, h], out)
        o_ref[h] = (out - rb_ref[REL_BUCKETS - 1, h]) * LOG2E
        return 0

    lax.fori_loop(0, ATT_HEADS, one_head, 0)


def _bias_tiles(rel_bias):
    return pl.pallas_call(
        _bias_tile_kernel,
        out_shape=jax.ShapeDtypeStruct((ATT_HEADS, BIAS_ROWS, LANES), F32),
        in_specs=[pl.BlockSpec(memory_space=pltpu.SMEM)],
        out_specs=pl.BlockSpec(memory_space=pltpu.VMEM),
        name="rel_bias_tiles",
    )(rel_bias)


def _dsa_kernel(q_ref, k_ref, v_ref, qi_ref, wq_ref, kik_ref, qg_ref, kg_ref, bt_ref, seg_ref,
                segt_ref, o_ref, kn_ref, vt_ref, kib_ref, key_ref, planes_ref, mfar_ref, s_ref, qs_ref, *, tq, chunk):
    i = pl.program_id(1)
    s_len = k_ref.shape[0]
    n_chunk = (i + chunk) // chunk
    n_far = (i - 1 + chunk - 1) // chunk
    w0 = jnp.maximum(i - 1, 0)

    @pl.when(i == 0)
    def _prepare_keys():
        kg = kg_ref[...]
        for g in range(KV_HEADS):
            kn = _rms(k_ref[:, g * HEAD_DIM:(g + 1) * HEAD_DIM], kg).astype(BF16)
            kn_ref[g] = jnp.concatenate([kn, kn], axis=1)
        for j in range(s_len // LANES):
            for pair in range(KV_HEADS // 2):
                v_t = v_ref[j * LANES:(j + 1) * LANES, pair * LANES:(pair + 1) * LANES].T
                vt_ref[2 * pair, j, 0:HEAD_DIM] = v_t[0:HEAD_DIM].astype(BF16)
                vt_ref[2 * pair + 1, j, 0:HEAD_DIM] = v_t[HEAD_DIM:].astype(BF16)
        vt_ref[:, :, HEAD_DIM:V_ROWS] = jnp.ones((KV_HEADS, s_len // LANES, V_ROWS - HEAD_DIM, LANES),
                                                 BF16)
        ki = kik_ref[:, 0:HEAD_DIM].astype(BF16)
        kib_ref[...] = jnp.concatenate([ki, ki], axis=1)
        planes_ref[...] = jnp.zeros(planes_ref.shape, I32)

    ck = chunk * LANES
    krow = lax.broadcasted_iota(I32, (ck, tq), 0)
    tpos = i * tq + lax.broadcasted_iota(I32, (ck, tq), 1)
    low_half = lax.broadcasted_iota(I32, (tq, LANES), 1) < HEAD_DIM

    def split_pair(x):
        return [jnp.where(low_half, x, 0.0).astype(BF16), jnp.where(low_half, 0.0, x).astype(BF16)]

    q = q_ref[...]
    sq = q * q
    sq_hi = sq.astype(BF16)
    sq_lo = (sq - sq_hi.astype(F32)).astype(BF16)
    seg = seg_ref[...]
    ssum = (jnp.dot(sq_hi, seg, preferred_element_type=F32)
            + jnp.dot(sq_lo, seg, preferred_element_type=F32))
    rinv = lax.rsqrt(ssum * (1.0 / HEAD_DIM) + EPS)
    pieces, rest = [], rinv
    for _ in range(2):
        piece = rest.astype(BF16)
        pieces.append(piece)
        rest = rest - piece.astype(F32)
    segt = segt_ref[...]
    rinv_b = sum(jnp.dot(piece, segt, preferred_element_type=F32) for piece in pieces)
    qn = q * rinv_b * (qg_ref[...] * (HEAD_DIM ** -0.5 * LOG2E))
    for g in range(KV_HEADS):
        rows = []
        for pair in range(2 * g, 2 * g + 2):
            rows += split_pair(qn[:, pair * LANES:(pair + 1) * LANES])
        qs_ref[g] = jnp.concatenate(rows, axis=0)

    w_t = wq_ref[...].T
    qi_pairs = [jnp.concatenate(split_pair(qi_ref[:, pair * LANES:(pair + 1) * LANES] * 0.125), axis=0)
                for pair in range(IDX_HEADS // 2)]
    w_heads = [w_t[HEAD_DIM + h:HEAD_DIM + h + 1, :] * (IDX_HEADS ** -0.5)
               for h in range(IDX_HEADS)]

    def bit_planes(c):
        for half in range(ck // GROUP_KEYS):
            words = [key_ref[c, pl.ds(half * GROUP_KEYS + 8 * r, 8), :] for r in range(32)]
            for shift, keep in ((16, 0x0000FFFF), (8, 0x00FF00FF), (4, 0x0F0F0F0F),
                                (2, 0x33333333), (1, 0x55555555)):
                amount = jnp.full((8, tq), shift, I32)
                for r in range(32):
                    if r & shift == 0:
                        a, b = words[r], words[r + shift]
                        t = (lax.shift_right_logical(a, amount) ^ b) & keep
                        words[r] = a ^ lax.shift_left(t, amount)
                        words[r + shift] = b ^ t
            words[31] = ~words[31]
            for b in range(32):
                planes_ref[c * (ck // GROUP_KEYS) + half, b] = words[b]

    def score_chunk(c):
        kt = kib_ref[pl.ds(pl.multiple_of(c * ck, ck), ck), :]
        dots = [lax.dot_general(kt, qi_pair, NT_DIMS, preferred_element_type=F32)
                for qi_pair in qi_pairs]
        sc = jnp.zeros((ck, tq), F32)
        for h in range(IDX_HEADS):
            d = dots[h // 2][:, (h % 2) * tq:(h % 2 + 1) * tq]
            sc = sc + w_heads[h] * jnp.maximum(d, 0.0)
        sc = jnp.where(sc == 0.0, 0.0, sc)
        sc = jnp.where(c * ck + krow <= tpos, sc, -jnp.inf)
        bits = pltpu.bitcast(sc, I32)
        key_ref[c] = bits ^ ((bits >> 31) & 0x7FFFFFFF)

    score_chunk(0)

    def scan(c, _):
        bit_planes(c - 1)
        score_chunk(c)
        return 0

    lax.fori_loop(1, n_chunk, scan, 0)

    def count_keys(preds):
        def body(c, cs):
            key = key_ref[c]
            return tuple(
                cnt + jnp.sum(pred(c, key).astype(I32).reshape(ck // 8, 8, tq), axis=0)
                for cnt, pred in zip(cs, preds))
        cs = lax.fori_loop(0, n_chunk, body,
                           tuple(jnp.zeros((8, tq), I32) for _ in preds))
        return [jnp.sum(cnt, axis=0, keepdims=True) for cnt in cs]

    n_groups = s_len // GROUP_KEYS

    def sliced_search():
        bit_planes(n_chunk - 1)
        live = n_chunk * (ck // GROUP_KEYS)

        def two_bits(step, state):
            eqs, n_gt, thr_u = state
            b0 = 30 - 2 * step
            s11, s10, s01, s00 = [], [], [], []
            for g in range(n_groups):
                with_hi = eqs[g] & planes_ref[g, b0 + 1]
                without_hi = eqs[g] ^ with_hi
                low = planes_ref[g, b0]
                s11.append(with_hi & low)
                s10.append(with_hi ^ s11[g])
                s01.append(without_hi & low)
                s00.append(without_hi ^ s01[g])

            def count(sets):
                total = lax.population_count(sets[0])
                for g in range(1, n_groups):
                    total = total + lax.population_count(sets[g])
                return jnp.sum(total, axis=0, keepdims=True)

            up11 = n_gt + count(s11)
            up10 = up11 + count(s10)
            up01 = up10 + count(s01)
            in11 = up11 >= TOPK
            in10 = jnp.logical_not(in11) & (up10 >= TOPK)
            in01 = jnp.logical_not(in11 | in10) & (up01 >= TOPK)
            n_gt = jnp.where(in11, n_gt, jnp.where(in10, up11, jnp.where(in01, up10, up01)))
            bits = jnp.where(in11, 3, jnp.where(in10, 2, jnp.where(in01, 1, 0)))
            thr_u = thr_u | jnp.left_shift(bits, b0)
            m11, m10, m01 = (jnp.broadcast_to(m, (8, tq)) for m in (in11, in10, in01))
            eqs = tuple(jnp.where(m11, s11[g], jnp.where(m10, s10[g], jnp.where(m01, s01[g], s00[g])))
                        for g in range(n_groups))
            return eqs, n_gt, thr_u

        eqs = tuple(jnp.full((8, tq), jnp.where(g < live, -1, 0), I32) for g in range(n_groups))
        zero = jnp.zeros((1, tq), I32)
        eqs, n_gt, thr_u = lax.fori_loop(0, 16, two_bits, (eqs, zero, zero))
        n_eq = lax.population_count(eqs[0])
        for g in range(1, n_groups):
            n_eq = n_eq + lax.population_count(eqs[g])
        return thr_u ^ INT_MIN, n_gt, jnp.sum(n_eq, axis=0, keepdims=True)

    thr, n_gt, n_eq = lax.cond(
        (i + 1) * tq <= TOPK,
        lambda: (jnp.full((1, tq), KEY_NEG_INF, I32), jnp.zeros((1, tq), I32), jnp.zeros((1, tq), I32)),
        sliced_search)
    thr_b = jnp.broadcast_to(thr, (ck, tq))
    need = TOPK - n_gt
    tied = (n_eq > need) & (thr != KEY_NEG_INF)
    any_tied = jnp.max(tied.astype(I32)) > 0

    def tie_bit(b, cut):
        cand = cut + lax.shift_left(jnp.int32(1), 10 - b)
        cand_b = jnp.broadcast_to(cand, (ck, tq))
        (cnt,) = count_keys([lambda c, key: (key == thr_b) & (c * ck + krow < cand_b)])
        return jnp.where(cnt < need, cand, cut)

    cut = lax.cond(any_tied,
                   lambda: lax.fori_loop(0, 11, tie_bit, jnp.zeros((1, tq), I32)),
                   lambda: jnp.full((1, tq), s_len, I32))

    def selection_mask(key, spos, tpos):
        thr_k = jnp.broadcast_to(thr, key.shape)
        cut_k = jnp.broadcast_to(cut, key.shape)
        sel = (key > thr_k) | ((key == thr_k) & (spos <= cut_k))
        return jnp.where(sel & (spos <= tpos), 0.0, NEG)

    def mask_chunk(c, _):
        spos = c * ck + krow
        madd = selection_mask(key_ref[c], spos, tpos)
        mfar_ref[c] = jnp.where(spos < (i - 1) * LANES, madd, NEG)
        return 0

    lax.fori_loop(0, n_far, mask_chunk, 0)

    def tile_mask(tile):
        c, u = tile // chunk, tile % chunk
        key = key_ref[c, pl.ds(pl.multiple_of(u * LANES, LANES), LANES), :]
        krow_t = lax.broadcasted_iota(I32, (LANES, tq), 0)
        tpos_t = i * tq + lax.broadcasted_iota(I32, (LANES, tq), 1)
        return selection_mask(key, tile * LANES + krow_t, tpos_t)

    qs = [qs_ref[g] for g in range(KV_HEADS)]

    groups = range(KV_HEADS)

    def attend(sts, n_keys, first_tile, add_fn):
        key_start = pl.multiple_of(first_tile * LANES, LANES)
        block_max = []
        for g in groups:
            kt = kn_ref[g, pl.ds(key_start, n_keys), :]
            add = jnp.concatenate([add_fn(g * GQA_REP + r) for r in range(GQA_REP)], axis=1)
            s = lax.dot_general(kt, qs[g], NT_DIMS, preferred_element_type=F32) + add
            s_ref[g, 0:n_keys, :] = s
            block_max.append(jnp.max(s, axis=0, keepdims=True))
        new = []
        for g in groups:
            m, acc = sts[g]
            m_new = jnp.maximum(m, block_max[g])
            p = jnp.exp2(s_ref[g, 0:n_keys, :] - m_new).astype(BF16)
            v_t = jnp.concatenate([vt_ref[g, first_tile + u] for u in range(n_keys // LANES)],
                                  axis=1)
            new.append((m_new, jnp.exp2(m - m_new) * acc
                        + jnp.dot(v_t, p, preferred_element_type=F32)))
        return tuple(new)

    def far_chunk(c, sts):
        madd = mfar_ref[c]
        return attend(sts, ck, c * chunk, lambda head: madd)

    init = (jnp.full((1, GQA_REP * tq), NEG, F32), jnp.zeros((V_ROWS, GQA_REP * tq), F32))
    sts = lax.fori_loop(0, n_far, far_chunk, (init,) * KV_HEADS)

    win_mask = jnp.concatenate([tile_mask(w0), tile_mask(w0 + 1)], axis=0)
    bias_row = pl.multiple_of(jnp.where(i == 0, LANES, 0), LANES)
    sts = attend(sts, 2 * LANES, w0,
                 lambda head: bt_ref[head, pl.ds(bias_row, 2 * LANES), :] + win_mask)
    for g in range(KV_HEADS):
        acc = sts[g][1]
        out_t = acc[0:HEAD_DIM] / acc[HEAD_DIM:HEAD_DIM + 1]
        for pair in range(GQA_REP // 2):
            two = jnp.concatenate(
                [out_t[:, (2 * pair) * tq:(2 * pair + 1) * tq],
                 out_t[:, (2 * pair + 1) * tq:(2 * pair + 2) * tq]], axis=0)
            col = (g * GQA_REP + 2 * pair) * HEAD_DIM
            o_ref[col:col + LANES, :] = two.astype(o_ref.dtype)


def _dsa(proj, q_gain, k_gain, bias_tiles, *, tq=128, chunk=4):
    b, s, _ = proj.shape
    n_t = s // LANES
    att_width = ATT_HEADS * HEAD_DIM
    seg = (jnp.arange(att_width)[:, None] // HEAD_DIM == jnp.arange(LANES)[None, :]).astype(BF16)
    q_gain_all = jnp.tile(q_gain, (1, ATT_HEADS))
    return pl.pallas_call(
        functools.partial(_dsa_kernel, tq=tq, chunk=chunk),
        out_shape=jax.ShapeDtypeStruct((b, att_width, s), BF16),
        grid=(b, s // tq),
        in_specs=[
            pl.BlockSpec((None, tq, 1024), lambda bi, i: (bi, i, 0)),
            pl.BlockSpec((None, s, 256), lambda bi, i: (bi, 0, 4)),
            pl.BlockSpec((None, s, 256), lambda bi, i: (bi, 0, 5)),
            pl.BlockSpec((None, tq, 512), lambda bi, i: (bi, i, 3)),
            pl.BlockSpec((None, tq, LANES), lambda bi, i: (bi, i, 16)),
            pl.BlockSpec((None, s, LANES), lambda bi, i: (bi, 0, 16)),
            pl.BlockSpec((1, att_width), lambda bi, i: (0, 0)),
            pl.BlockSpec((1, HEAD_DIM), lambda bi, i: (0, 0)),
            pl.BlockSpec((ATT_HEADS, BIAS_ROWS, LANES), lambda bi, i: (0, 0, 0)),
            pl.BlockSpec((att_width, LANES), lambda bi, i: (0, 0)),
            pl.BlockSpec((LANES, att_width), lambda bi, i: (0, 0)),
        ],
        out_specs=pl.BlockSpec((None, att_width, tq), lambda bi, i: (bi, 0, i)),
        scratch_shapes=[
            pltpu.VMEM((KV_HEADS, s, LANES), BF16),
            pltpu.VMEM((KV_HEADS, n_t, V_ROWS, LANES), BF16),
            pltpu.VMEM((s, LANES), BF16),
            pltpu.VMEM((n_t // chunk, chunk * LANES, tq), I32),
            pltpu.VMEM((s // GROUP_KEYS, 32, 8, tq), I32),
            pltpu.VMEM((n_t // chunk, chunk * LANES, tq), F32),
            pltpu.VMEM((KV_HEADS, chunk * LANES, GQA_REP * tq), F32),
            pltpu.VMEM((KV_HEADS, GQA_REP * tq, LANES), BF16),
        ],
        compiler_params=pltpu.CompilerParams(
            dimension_semantics=("parallel", "arbitrary"), vmem_limit_bytes=VMEM_LIMIT),
        name="dsa_attention",
    )(proj, proj, proj, proj, proj, proj, q_gain_all, k_gain, bias_tiles, seg, seg.T)


def kernel(x, norm_mix, norm_ffn, ev_w_in, ev_conv_w, ev_w_out, od_w_in, od_q_gain,
           od_k_gain, od_w_out, rel_bias, ffn_w_gate, ffn_w_up, ffn_w_down):
    b, s, d = x.shape
    n_tok = b * s
    bf = lambda w: w.astype(BF16)

    proj0 = _norm_matmul(x.reshape(n_tok, d), norm_mix[0:1], bf(ev_w_in[0]))
    proj0 = proj0.reshape(b, s, -1)
    a_out = _stickbreak(proj0)
    x2 = _mix0_ffn(a_out, proj0, ev_conv_w[0], bf(ev_w_out[0]), x, norm_ffn[0:1],
                   bf(ffn_w_gate[0]), bf(ffn_w_up[0]), bf(ffn_w_down[0])).reshape(n_tok, d)

    w_in1 = jnp.pad(od_w_in[0], ((0, 0), (0, ODD_IN_PAD - od_w_in.shape[2])))
    proj1 = _norm_matmul(x2, norm_mix[1:2], bf(w_in1)).reshape(b, s, ODD_IN_PAD)
    att = _dsa(proj1, od_q_gain[0:1], od_k_gain[0:1], _bias_tiles(rel_bias))
    return _proj_ffn(att, bf(od_w_out[0]), x2.reshape(b, s, d), norm_ffn[1:2],
                     bf(ffn_w_gate[1]), bf(ffn_w_up[1]), bf(ffn_w_down[1]))
```

```python
import functools
import math

import jax
import jax.numpy as jnp
from jax import lax
from jax.experimental import pallas as pl
from jax.experimental.pallas import tpu as pltpu

F32 = jnp.float32
BF16 = jnp.bfloat16
I32 = jnp.int32

D_MODEL = 1024
SEQ = 2048
HEAD_DIM = 64
SB_WIDTH = 512
CONV_CH = 512
ATT_HEADS = 16
KV_HEADS = 4
GQA_REP = 4
IDX_HEADS = 8
TOPK = 256
REL_BUCKETS = 32
REL_MAX_DIST = 128
FFN_HIDDEN = 2816
EPS = 1e-6
ODD_IN_PAD = 2176

LOG2E = 1.4426950408889634
LANES = 128
VMEM_LIMIT = 56 * 1024 * 1024
VMEM_LIMIT_LARGE = 60 * 1024 * 1024
NEG = -1e30
INT_MIN = -(2 ** 31)
GROUP_KEYS = 32 * 8
V_ROWS = HEAD_DIM + 16
UNDERFLOW_MARGIN = 160.0
NORM_SLACK = 1.03
KEY_NEG_INF = -(2 ** 31) + 0x7FFFFF

NT_DIMS = (((1,), (1,)), ((), ()))


def _rms(x, g):
    ms = jnp.mean(x * x, axis=-1, keepdims=True)
    return x * lax.rsqrt(ms + EPS) * g


def _norm_matmul_kernel(x_ref, g_ref, w_ref, o_ref, *, tn):
    h = _rms(x_ref[...], g_ref[...]).astype(BF16)
    n = w_ref.shape[1]
    for start in range(0, n, tn):
        size = min(tn, n - start)
        o_ref[:, start:start + size] = jnp.dot(
            h, w_ref[:, start:start + size], preferred_element_type=F32)


def _norm_matmul(x, g, w, *, tm=1024, tn=512):
    m, d = x.shape
    n = w.shape[1]
    return pl.pallas_call(
        functools.partial(_norm_matmul_kernel, tn=tn),
        out_shape=jax.ShapeDtypeStruct((m, n), F32),
        grid=(m // tm,),
        in_specs=[
            pl.BlockSpec((tm, d), lambda i: (i, 0)),
            pl.BlockSpec((1, d), lambda i: (0, 0)),
            pl.BlockSpec((d, n), lambda i: (0, 0)),
        ],
        out_specs=pl.BlockSpec((tm, n), lambda i: (i, 0)),
        compiler_params=pltpu.CompilerParams(
            dimension_semantics=("parallel",), vmem_limit_bytes=VMEM_LIMIT),
        name="norm_matmul",
    )(x, g, w)


def _sb_kernel(q_ref, k_ref, v_ref, o_ref, vt_ref, kmax_ref, *, tq, pairs):
    qi = pl.program_id(2)
    n_sub = tq // LANES
    s_len = k_ref.shape[0]
    width = pairs * LANES
    head_of = (lax.broadcasted_iota(I32, (width, LANES), 0) // HEAD_DIM
               == lax.broadcasted_iota(I32, (width, LANES), 1)).astype(BF16)

    def max_sq_norm(x):
        sq = (x * x).astype(BF16)
        return jnp.max(jnp.dot(sq, head_of, preferred_element_type=F32), axis=0, keepdims=True)

    @pl.when(qi == 0)
    def _per_sequence_setup():
        for p in range(pairs):
            for j in range(s_len // LANES):
                vt_ref[p, j] = v_ref[j * LANES:(j + 1) * LANES,
                                     p * LANES:(p + 1) * LANES].T.astype(BF16)
        kmax = jnp.zeros((1, LANES), F32)
        for r in range(0, s_len, tq):
            kmax = jnp.maximum(kmax, max_sq_norm(k_ref[r:r + tq, :]))
        kmax_ref[...] = kmax

    lane = lax.broadcasted_iota(I32, (tq, LANES), 1)
    krow = lax.broadcasted_iota(I32, (tq, tq), 0)
    qcol = lax.broadcasted_iota(I32, (tq, tq), 1)
    from_here = (qcol >= krow).astype(BF16)
    causal = krow < qcol
    q_heads = []
    for p in range(pairs):
        q = q_ref[:, p * LANES:(p + 1) * LANES] * (HEAD_DIM ** -0.5 * LOG2E)
        for h in range(2):
            in_head = (lane >= HEAD_DIM * h) & (lane < HEAD_DIM * (h + 1))
            q_heads.append(jnp.where(in_head, q, 0.0).astype(BF16))

    def block(kb, st, masked):
        start = pl.multiple_of(kb * tq, tq)
        heads = range(2 * pairs)
        kblks = [k_ref[pl.ds(start, tq), p * LANES:(p + 1) * LANES].astype(BF16)
                 for p in range(pairs)]
        v_ts = [jnp.concatenate([vt_ref[p, kb * n_sub + u] for u in range(n_sub)], axis=1)
                for p in range(pairs)]
        zs = [lax.dot_general(kblks[idx // 2], q_heads[idx], NT_DIMS, preferred_element_type=F32)
              for idx in heads]
        softplus, his, los = [], [], []
        for z in zs:
            sp = jnp.maximum(z, 0.0) + jnp.log2(1.0 + jnp.exp2(-jnp.abs(z)))
            if masked:
                sp = jnp.where(causal, sp, 0.0)
            hi = sp.astype(BF16)
            softplus.append(sp)
            his.append(hi)
            los.append((sp - hi.astype(F32)).astype(BF16))
        since = [jnp.dot(from_here, his[idx], preferred_element_type=F32)
                 + jnp.dot(from_here, los[idx], preferred_element_type=F32) for idx in heads]
        ws = []
        for idx in heads:
            w = jnp.exp2(zs[idx] - since[idx] - st[2 * idx])
            if masked:
                w = jnp.where(causal, w, 0.0)
            ws.append(w.astype(BF16))
        new = []
        for idx in heads:
            h = idx % 2
            acc = st[2 * idx + 1] + jnp.dot(v_ts[idx // 2][h * HEAD_DIM:(h + 1) * HEAD_DIM, :],
                                            ws[idx], preferred_element_type=F32)
            carry = st[2 * idx] + jnp.sum(softplus[idx], axis=0, keepdims=True)
            new += [carry, acc]
        return tuple(new)

    z_bound = jnp.sqrt(max_sq_norm(q_ref[...]) * kmax_ref[...]) * (HEAD_DIM ** -0.5 * LOG2E * NORM_SLACK)
    z_bounds = [jnp.broadcast_to(z_bound[:, idx:idx + 1], (1, tq)) for idx in range(2 * pairs)]

    def all_underflow(st):
        slack = st[0] - z_bounds[0]
        for idx in range(1, 2 * pairs):
            slack = jnp.minimum(slack, st[2 * idx] - z_bounds[idx])
        return jnp.min(slack) > UNDERFLOW_MARGIN

    def walk(state):
        n, _, st = state
        st = block(qi - 1 - n, st, False)
        return n + 1, all_underflow(st), st

    st = (jnp.zeros((1, tq), F32), jnp.zeros((HEAD_DIM, tq), F32)) * (2 * pairs)
    st = block(qi, st, True)
    _, _, st = lax.while_loop(lambda state: (state[0] < qi) & jnp.logical_not(state[1]), walk,
                              (jnp.int32(0), all_underflow(st), st))
    for p in range(pairs):
        out_t = jnp.concatenate([st[4 * p + 1], st[4 * p + 3]], axis=0)
        o_ref[p * LANES:(p + 1) * LANES, :] = out_t.astype(o_ref.dtype)


def _stickbreak(proj, *, tq=256, pairs=4):
    b, s, _ = proj.shape
    width = pairs * LANES
    n_steps = SB_WIDTH // width
    return pl.pallas_call(
        functools.partial(_sb_kernel, tq=tq, pairs=pairs),
        out_shape=jax.ShapeDtypeStruct((b, SB_WIDTH, s), BF16),
        grid=(b, n_steps, s // tq),
        in_specs=[
            pl.BlockSpec((None, tq, width), lambda bi, hp, qi: (bi, qi, hp)),
            pl.BlockSpec((None, s, width), lambda bi, hp, qi: (bi, 0, n_steps + hp)),
            pl.BlockSpec((None, s, width), lambda bi, hp, qi: (bi, 0, 2 * n_steps + hp)),
        ],
        out_specs=pl.BlockSpec((None, width, tq), lambda bi, hp, qi: (bi, hp, qi)),
        scratch_shapes=[pltpu.VMEM((pairs, s // LANES, LANES, LANES), BF16),
                        pltpu.VMEM((1, LANES), F32)],
        compiler_params=pltpu.CompilerParams(
            dimension_semantics=("parallel", "parallel", "arbitrary"),
            vmem_limit_bytes=VMEM_LIMIT),
        name="stickbreak",
    )(proj, proj, proj)


def _ffn(x, g_ref, wg_ref, wu_ref, wd_ref, act_ref, th):
    h = _rms(x, g_ref[...]).astype(BF16)
    for start in range(0, FFN_HIDDEN, th):
        stop = min(start + th, FFN_HIDDEN)
        hg = jnp.dot(h, wg_ref[:, start:stop], preferred_element_type=F32)
        hu = jnp.dot(h, wu_ref[:, start:stop], preferred_element_type=F32)
        sig = 1.0 / (1.0 + jnp.exp(-hg))
        act_ref[:, start:stop] = (hg * sig * hu).astype(BF16)
    return x + jnp.dot(act_ref[...], wd_ref[...], preferred_element_type=F32)


def _ffn_specs(d, index_map):
    resident = pl.Buffered(1)
    return [
        pl.BlockSpec((1, d), index_map),
        pl.BlockSpec((d, FFN_HIDDEN), index_map, pipeline_mode=resident),
        pl.BlockSpec((d, FFN_HIDDEN), index_map, pipeline_mode=resident),
        pl.BlockSpec((FFN_HIDDEN, d), index_map, pipeline_mode=resident),
    ]


def _mix0_ffn_kernel(a_ref, bg_ref, cg_ref, u_ref, cgh_ref, uh_ref, cw_ref, w_ref, x_ref,
                     g_ref, wg_ref, wu_ref, wd_ref, o_ref, act_ref, *, th):
    i = pl.program_id(1)
    g = cg_ref[...] * u_ref[...]
    gh = cgh_ref[...] * uh_ref[...]
    gh = jnp.where(i == 0, 0.0, gh)
    row = lax.broadcasted_iota(I32, g.shape, 0)
    g1 = jnp.where(row == 0, gh[7:8, :], pltpu.roll(g, 1, axis=0))
    g2 = pltpu.roll(g, 2, axis=0)
    g2 = jnp.where(row == 0, gh[6:7, :], jnp.where(row == 1, gh[7:8, :], g2))
    cw = cw_ref[...]
    y = bg_ref[...] * (cw[0:1, :] * g2 + cw[1:2, :] * g1 + cw[2:3, :] * g)
    acc = lax.dot_general(a_ref[...], w_ref[0:SB_WIDTH, :], (((0,), (0,)), ((), ())),
                          preferred_element_type=F32)
    acc = acc + jnp.dot(y.astype(BF16), w_ref[SB_WIDTH:, :], preferred_element_type=F32)
    o_ref[...] = _ffn(x_ref[...] + acc, g_ref, wg_ref, wu_ref, wd_ref, act_ref, th)


def _mix0_ffn(a_out, proj, conv_w, w_out, x, g, wg, wu, wd, *, tm=1024, th=256):
    b, s, d = x.shape
    halo = lambda col: (lambda bi, i: (bi, jnp.maximum(i * (tm // 8) - 1, 0), col))
    return pl.pallas_call(
        functools.partial(_mix0_ffn_kernel, th=th),
        out_shape=jax.ShapeDtypeStruct((b, s, d), F32),
        grid=(b, s // tm),
        in_specs=[
            pl.BlockSpec((None, SB_WIDTH, tm), lambda bi, i: (bi, 0, i)),
            pl.BlockSpec((None, tm, CONV_CH), lambda bi, i: (bi, i, 3)),
            pl.BlockSpec((None, tm, CONV_CH), lambda bi, i: (bi, i, 4)),
            pl.BlockSpec((None, tm, CONV_CH), lambda bi, i: (bi, i, 5)),
            pl.BlockSpec((None, 8, CONV_CH), halo(4)),
            pl.BlockSpec((None, 8, CONV_CH), halo(5)),
            pl.BlockSpec((3, CONV_CH), lambda bi, i: (0, 0)),
            pl.BlockSpec((SB_WIDTH + CONV_CH, d), lambda bi, i: (0, 0), pipeline_mode=pl.Buffered(1)),
            pl.BlockSpec((None, tm, d), lambda bi, i: (bi, i, 0)),
        ] + _ffn_specs(d, lambda bi, i: (0, 0)),
        out_specs=pl.BlockSpec((None, tm, d), lambda bi, i: (bi, i, 0)),
        scratch_shapes=[pltpu.VMEM((tm, FFN_HIDDEN), BF16)],
        compiler_params=pltpu.CompilerParams(
            dimension_semantics=("parallel", "arbitrary"), vmem_limit_bytes=VMEM_LIMIT_LARGE),
        name="mix0_ffn",
    )(a_out, proj, proj, proj, proj, proj, conv_w, w_out, x, g, wg, wu, wd)


def _proj_ffn_kernel(at_ref, w_ref, x_ref, g_ref, wg_ref, wu_ref, wd_ref, o_ref, act_ref, *, th):
    x1 = x_ref[...] + lax.dot_general(at_ref[...], w_ref[...], (((0,), (0,)), ((), ())),
                                      preferred_element_type=F32)
    o_ref[...] = _ffn(x1, g_ref, wg_ref, wu_ref, wd_ref, act_ref, th)


def _proj_ffn(a_t, w, x, g, wg, wu, wd, *, tm=1024, th=256):
    b, k, s = a_t.shape
    d = w.shape[1]
    return pl.pallas_call(
        functools.partial(_proj_ffn_kernel, th=th),
        out_shape=jax.ShapeDtypeStruct((b, s, d), F32),
        grid=(b, s // tm),
        in_specs=[
            pl.BlockSpec((None, k, tm), lambda bi, i: (bi, 0, i)),
            pl.BlockSpec((k, d), lambda bi, i: (0, 0)),
            pl.BlockSpec((None, tm, d), lambda bi, i: (bi, i, 0)),
        ] + _ffn_specs(d, lambda bi, i: (0, 0)),
        out_specs=pl.BlockSpec((None, tm, d), lambda bi, i: (bi, i, 0)),
        scratch_shapes=[pltpu.VMEM((tm, FFN_HIDDEN), BF16)],
        compiler_params=pltpu.CompilerParams(
            dimension_semantics=("parallel", "parallel"), vmem_limit_bytes=VMEM_LIMIT),
        name="proj_ffn",
    )(a_t, w, x, g, wg, wu, wd)


BIAS_ROWS = 3 * LANES


def _bias_tile_kernel(rb_ref, o_ref):
    c = lax.broadcasted_iota(I32, (BIAS_ROWS, LANES), 0)
    t = lax.broadcasted_iota(I32, (BIAS_ROWS, LANES), 1)
    dist = jnp.maximum(LANES + t - c, 0)
    exact = REL_BUCKETS // 2
    d_f = jnp.maximum(dist, 1).astype(F32)
    large = exact + (jnp.log(d_f / exact) / math.log(REL_MAX_DIST / exact)
                     * (REL_BUCKETS - exact)).astype(I32)
    large = jnp.minimum(large, REL_BUCKETS - 1)
    bucket = jnp.where(dist < exact, dist, large)

    def one_head(h, _):
        out = jnp.zeros((BIAS_ROWS, LANES), F32)
        for b in range(REL_BUCKETS):
            out = jnp.where(bucket == b, rb_ref[b, h], out)
        o_ref[h] = (out - rb_ref[REL_BUCKETS - 1, h]) * LOG2E
        return 0

    lax.fori_loop(0, ATT_HEADS, one_head, 0)


def _bias_tiles(rel_bias):
    return pl.pallas_call(
        _bias_tile_kernel,
        out_shape=jax.ShapeDtypeStruct((ATT_HEADS, BIAS_ROWS, LANES), F32),
        in_specs=[pl.BlockSpec(memory_space=pltpu.SMEM)],
        out_specs=pl.BlockSpec(memory_space=pltpu.VMEM),
        name="rel_bias_tiles",
    )(rel_bias)


def _dsa_kernel(q_ref, k_ref, v_ref, qi_ref, wq_ref, kik_ref, qg_ref, kg_ref, bt_ref, seg_ref,
                segt_ref, o_ref, kn_ref, vt_ref, kib_ref, key_ref, planes_ref, mfar_ref, s_ref, qs_ref, *, tq, chunk):
    i = pl.program_id(1)
    s_len = k_ref.shape[0]
    n_chunk = (i + chunk) // chunk
    n_far = (i - 1 + chunk - 1) // chunk
    w0 = jnp.maximum(i - 1, 0)

    @pl.when(i == 0)
    def _prepare_keys():
        kg = kg_ref[...]
        for g in range(KV_HEADS):
            kn = _rms(k_ref[:, g * HEAD_DIM:(g + 1) * HEAD_DIM], kg).astype(BF16)
            kn_ref[g] = jnp.concatenate([kn, kn], axis=1)
        for j in range(s_len // LANES):
            for pair in range(KV_HEADS // 2):
                v_t = v_ref[j * LANES:(j + 1) * LANES, pair * LANES:(pair + 1) * LANES].T
                vt_ref[2 * pair, j, 0:HEAD_DIM] = v_t[0:HEAD_DIM].astype(BF16)
                vt_ref[2 * pair + 1, j, 0:HEAD_DIM] = v_t[HEAD_DIM:].astype(BF16)
        vt_ref[:, :, HEAD_DIM:V_ROWS] = jnp.ones((KV_HEADS, s_len // LANES, V_ROWS - HEAD_DIM, LANES),
                                                 BF16)
        ki = kik_ref[:, 0:HEAD_DIM].astype(BF16)
        kib_ref[...] = jnp.concatenate([ki, ki], axis=1)
        planes_ref[...] = jnp.zeros(planes_ref.shape, I32)

    ck = chunk * LANES
    krow = lax.broadcasted_iota(I32, (ck, tq), 0)
    tpos = i * tq + lax.broadcasted_iota(I32, (ck, tq), 1)
    low_half = lax.broadcasted_iota(I32, (tq, LANES), 1) < HEAD_DIM

    def split_pair(x):
        return [jnp.where(low_half, x, 0.0).astype(BF16), jnp.where(low_half, 0.0, x).astype(BF16)]

    q = q_ref[...]
    sq = q * q
    sq_hi = sq.astype(BF16)
    sq_lo = (sq - sq_hi.astype(F32)).astype(BF16)
    seg = seg_ref[...]
    ssum = (jnp.dot(sq_hi, seg, preferred_element_type=F32)
            + jnp.dot(sq_lo, seg, preferred_element_type=F32))
    rinv = lax.rsqrt(ssum * (1.0 / HEAD_DIM) + EPS)
    pieces, rest = [], rinv
    for _ in range(2):
        piece = rest.astype(BF16)
        pieces.append(piece)
        rest = rest - piece.astype(F32)
    segt = segt_ref[...]
    rinv_b = sum(jnp.dot(piece, segt, preferred_element_type=F32) for piece in pieces)
    qn = q * rinv_b * (qg_ref[...] * (HEAD_DIM ** -0.5 * LOG2E))
    for g in range(KV_HEADS):
        rows = []
        for pair in range(2 * g, 2 * g + 2):
            rows += split_pair(qn[:, pair * LANES:(pair + 1) * LANES])
        qs_ref[g] = jnp.concatenate(rows, axis=0)

    w_t = wq_ref[...].T
    qi_pairs = [jnp.concatenate(split_pair(qi_ref[:, pair * LANES:(pair + 1) * LANES] * 0.125), axis=0)
                for pair in range(IDX_HEADS // 2)]
    w_heads = [w_t[HEAD_DIM + h:HEAD_DIM + h + 1, :] * (IDX_HEADS ** -0.5)
               for h in range(IDX_HEADS)]

    def bit_planes(c):
        for half in range(ck // GROUP_KEYS):
            words = [key_ref[c, pl.ds(half * GROUP_KEYS + 8 * r, 8), :] for r in range(32)]
            for shift, keep in ((16, 0x0000FFFF), (8, 0x00FF00FF), (4, 0x0F0F0F0F),
                                (2, 0x33333333), (1, 0x55555555)):
                amount = jnp.full((8, tq), shift, I32)
                for r in range(32):
                    if r & shift == 0:
                        a, b = words[r], words[r + shift]
                        t = (lax.shift_right_logical(a, amount) ^ b) & keep
                        words[r] = a ^ lax.shift_left(t, amount)
                        words[r + shift] = b ^ t
            words[31] = ~words[31]
            for b in range(32):
                planes_ref[c * (ck // GROUP_KEYS) + half, b] = words[b]

    def score_chunk(c):
        kt = kib_ref[pl.ds(pl.multiple_of(c * ck, ck), ck), :]
        dots = [lax.dot_general(kt, qi_pair, NT_DIMS, preferred_element_type=F32)
                for qi_pair in qi_pairs]
        sc = jnp.zeros((ck, tq), F32)
        for h in range(IDX_HEADS):
            d = dots[h // 2][:, (h % 2) * tq:(h % 2 + 1) * tq]
            sc = sc + w_heads[h] * jnp.maximum(d, 0.0)
        sc = jnp.where(sc == 0.0, 0.0, sc)
        sc = jnp.where(c * ck + krow <= tpos, sc, -jnp.inf)
        bits = pltpu.bitcast(sc, I32)
        key_ref[c] = bits ^ ((bits >> 31) & 0x7FFFFFFF)

    score_chunk(0)

    def scan(c, _):
        bit_planes(c - 1)
        score_chunk(c)
        return 0

    lax.fori_loop(1, n_chunk, scan, 0)

    def count_keys(preds):
        def body(c, cs):
            key = key_ref[c]
            return tuple(
                cnt + jnp.sum(pred(c, key).astype(I32).reshape(ck // 8, 8, tq), axis=0)
                for cnt, pred in zip(cs, preds))
        cs = lax.fori_loop(0, n_chunk, body,
                           tuple(jnp.zeros((8, tq), I32) for _ in preds))
        return [jnp.sum(cnt, axis=0, keepdims=True) for cnt in cs]

    n_groups = s_len // GROUP_KEYS

    def sliced_search():
        bit_planes(n_chunk - 1)
        live = n_chunk * (ck // GROUP_KEYS)

        def two_bits(step, state):
            eqs, n_gt, thr_u = state
            b0 = 30 - 2 * step
            s11, s10, s01, s00 = [], [], [], []
            for g in range(n_groups):
                with_hi = eqs[g] & planes_ref[g, b0 + 1]
                without_hi = eqs[g] ^ with_hi
                low = planes_ref[g, b0]
                s11.append(with_hi & low)
                s10.append(with_hi ^ s11[g])
                s01.append(without_hi & low)
                s00.append(without_hi ^ s01[g])

            def count(sets):
                total = lax.population_count(sets[0])
                for g in range(1, n_groups):
                    total = total + lax.population_count(sets[g])
                return jnp.sum(total, axis=0, keepdims=True)

            up11 = n_gt + count(s11)
            up10 = up11 + count(s10)
            up01 = up10 + count(s01)
            in11 = up11 >= TOPK
            in10 = jnp.logical_not(in11) & (up10 >= TOPK)
            in01 = jnp.logical_not(in11 | in10) & (up01 >= TOPK)
            n_gt = jnp.where(in11, n_gt, jnp.where(in10, up11, jnp.where(in01, up10, up01)))
            bits = jnp.where(in11, 3, jnp.where(in10, 2, jnp.where(in01, 1, 0)))
            thr_u = thr_u | jnp.left_shift(bits, b0)
            m11, m10, m01 = (jnp.broadcast_to(m, (8, tq)) for m in (in11, in10, in01))
            eqs = tuple(jnp.where(m11, s11[g], jnp.where(m10, s10[g], jnp.where(m01, s01[g], s00[g])))
                        for g in range(n_groups))
            return eqs, n_gt, thr_u

        eqs = tuple(jnp.full((8, tq), jnp.where(g < live, -1, 0), I32) for g in range(n_groups))
        zero = jnp.zeros((1, tq), I32)
        eqs, n_gt, thr_u = lax.fori_loop(0, 16, two_bits, (eqs, zero, zero))
        n_eq = lax.population_count(eqs[0])
        for g in range(1, n_groups):
            n_eq = n_eq + lax.population_count(eqs[g])
        return thr_u ^ INT_MIN, n_gt, jnp.sum(n_eq, axis=0, keepdims=True)

    thr, n_gt, n_eq = lax.cond(
        (i + 1) * tq <= TOPK,
        lambda: (jnp.full((1, tq), KEY_NEG_INF, I32), jnp.zeros((1, tq), I32), jnp.zeros((1, tq), I32)),
        sliced_search)
    thr_b = jnp.broadcast_to(thr, (ck, tq))
    need = TOPK - n_gt
    tied = (n_eq > need) & (thr != KEY_NEG_INF)
    any_tied = jnp.max(tied.astype(I32)) > 0

    def tie_bit(b, cut):
        cand = cut + lax.shift_left(jnp.int32(1), 10 - b)
        cand_b = jnp.broadcast_to(cand, (ck, tq))
        (cnt,) = count_keys([lambda c, key: (key == thr_b) & (c * ck + krow < cand_b)])
        return jnp.where(cnt < need, cand, cut)

    cut = lax.cond(any_tied,
                   lambda: lax.fori_loop(0, 11, tie_bit, jnp.zeros((1, tq), I32)),
                   lambda: jnp.full((1, tq), s_len, I32))

    def selection_mask(key, spos, tpos):
        thr_k = jnp.broadcast_to(thr, key.shape)
        cut_k = jnp.broadcast_to(cut, key.shape)
        sel = (key > thr_k) | ((key == thr_k) & (spos <= cut_k))
        return jnp.where(sel & (spos <= tpos), 0.0, NEG)

    def mask_chunk(c, _):
        spos = c * ck + krow
        madd = selection_mask(key_ref[c], spos, tpos)
        mfar_ref[c] = jnp.where(spos < (i - 1) * LANES, madd, NEG)
        return 0

    lax.fori_loop(0, n_far, mask_chunk, 0)

    def tile_mask(tile):
        c, u = tile // chunk, tile % chunk
        key = key_ref[c, pl.ds(pl.multiple_of(u * LANES, LANES), LANES), :]
        krow_t = lax.broadcasted_iota(I32, (LANES, tq), 0)
        tpos_t = i * tq + lax.broadcasted_iota(I32, (LANES, tq), 1)
        return selection_mask(key, tile * LANES + krow_t, tpos_t)

    qs = [qs_ref[g] for g in range(KV_HEADS)]

    groups = range(KV_HEADS)

    def attend(sts, runs, add_fn):
        n_keys = sum(count for _, count in runs) * LANES
        block_max = []
        for g in groups:
            kt = jnp.concatenate(
                [kn_ref[g, pl.ds(pl.multiple_of(first * LANES, LANES), count * LANES), :]
                 for first, count in runs], axis=0)
            add = jnp.concatenate([add_fn(g * GQA_REP + r) for r in range(GQA_REP)], axis=1)
            s = lax.dot_general(kt, qs[g], NT_DIMS, preferred_element_type=F32) + add
            s_ref[g, 0:n_keys, :] = s
            block_max.append(jnp.max(s, axis=0, keepdims=True))
        new = []
        for g in groups:
            m, acc = sts[g]
            m_new = jnp.maximum(m, block_max[g])
            p = jnp.exp2(s_ref[g, 0:n_keys, :] - m_new).astype(BF16)
            v_t = jnp.concatenate([vt_ref[g, first + u] for first, count in runs
                                   for u in range(count)], axis=1)
            new.append((m_new, jnp.exp2(m - m_new) * acc
                        + jnp.dot(v_t, p, preferred_element_type=F32)))
        return tuple(new)

    def far_chunk(c, sts):
        madd = mfar_ref[c]
        return attend(sts, [(c * chunk, chunk)], lambda head: madd)

    init = (jnp.full((1, GQA_REP * tq), NEG, F32), jnp.zeros((V_ROWS, GQA_REP * tq), F32))
    sts = lax.fori_loop(0, n_far - 1, far_chunk, (init,) * KV_HEADS)
    win_mask = jnp.concatenate([tile_mask(w0), tile_mask(w0 + 1)], axis=0)
    bias_row = pl.multiple_of(jnp.where(i == 0, LANES, 0), LANES)
    window_add = lambda head: bt_ref[head, pl.ds(bias_row, 2 * LANES), :] + win_mask

    def last_chunk_and_window(sts):
        last = n_far - 1
        madd = mfar_ref[last]
        return attend(sts, [(last * chunk, chunk), (w0, 2)],
                      lambda head: jnp.concatenate([madd, window_add(head)], axis=0))

    sts = lax.cond(n_far >= 1, last_chunk_and_window,
                   lambda sts: attend(sts, [(w0, 2)], window_add), sts)
    for g in range(KV_HEADS):
        acc = sts[g][1]
        out_t = acc[0:HEAD_DIM] / acc[HEAD_DIM:HEAD_DIM + 1]
        for pair in range(GQA_REP // 2):
            two = jnp.concatenate(
                [out_t[:, (2 * pair) * tq:(2 * pair + 1) * tq],
                 out_t[:, (2 * pair + 1) * tq:(2 * pair + 2) * tq]], axis=0)
            col = (g * GQA_REP + 2 * pair) * HEAD_DIM
            o_ref[col:col + LANES, :] = two.astype(o_ref.dtype)


def _dsa(proj, q_gain, k_gain, bias_tiles, *, tq=128, chunk=4):
    b, s, _ = proj.shape
    n_t = s // LANES
    att_width = ATT_HEADS * HEAD_DIM
    seg = (jnp.arange(att_width)[:, None] // HEAD_DIM == jnp.arange(LANES)[None, :]).astype(BF16)
    q_gain_all = jnp.tile(q_gain, (1, ATT_HEADS))
    return pl.pallas_call(
        functools.partial(_dsa_kernel, tq=tq, chunk=chunk),
        out_shape=jax.ShapeDtypeStruct((b, att_width, s), BF16),
        grid=(b, s // tq),
        in_specs=[
            pl.BlockSpec((None, tq, 1024), lambda bi, i: (bi, i, 0)),
            pl.BlockSpec((None, s, 256), lambda bi, i: (bi, 0, 4)),
            pl.BlockSpec((None, s, 256), lambda bi, i: (bi, 0, 5)),
            pl.BlockSpec((None, tq, 512), lambda bi, i: (bi, i, 3)),
            pl.BlockSpec((None, tq, LANES), lambda bi, i: (bi, i, 16)),
            pl.BlockSpec((None, s, LANES), lambda bi, i: (bi, 0, 16)),
            pl.BlockSpec((1, att_width), lambda bi, i: (0, 0)),
            pl.BlockSpec((1, HEAD_DIM), lambda bi, i: (0, 0)),
            pl.BlockSpec((ATT_HEADS, BIAS_ROWS, LANES), lambda bi, i: (0, 0, 0)),
            pl.BlockSpec((att_width, LANES), lambda bi, i: (0, 0)),
            pl.BlockSpec((LANES, att_width), lambda bi, i: (0, 0)),
        ],
        out_specs=pl.BlockSpec((None, att_width, tq), lambda bi, i: (bi, 0, i)),
        scratch_shapes=[
            pltpu.VMEM((KV_HEADS, s, LANES), BF16),
            pltpu.VMEM((KV_HEADS, n_t, V_ROWS, LANES), BF16),
            pltpu.VMEM((s, LANES), BF16),
            pltpu.VMEM((n_t // chunk, chunk * LANES, tq), I32),
            pltpu.VMEM((s // GROUP_KEYS, 32, 8, tq), I32),
            pltpu.VMEM((n_t // chunk, chunk * LANES, tq), F32),
            pltpu.VMEM((KV_HEADS, (chunk + 2) * LANES, GQA_REP * tq), F32),
            pltpu.VMEM((KV_HEADS, GQA_REP * tq, LANES), BF16),
        ],
        compiler_params=pltpu.CompilerParams(
            dimension_semantics=("parallel", "arbitrary"), vmem_limit_bytes=VMEM_LIMIT),
        name="dsa_attention",
    )(proj, proj, proj, proj, proj, proj, q_gain_all, k_gain, bias_tiles, seg, seg.T)


def kernel(x, norm_mix, norm_ffn, ev_w_in, ev_conv_w, ev_w_out, od_w_in, od_q_gain,
           od_k_gain, od_w_out, rel_bias, ffn_w_gate, ffn_w_up, ffn_w_down):
    b, s, d = x.shape
    n_tok = b * s
    bf = lambda w: w.astype(BF16)

    proj0 = _norm_matmul(x.reshape(n_tok, d), norm_mix[0:1], bf(ev_w_in[0]))
    proj0 = proj0.reshape(b, s, -1)
    a_out = _stickbreak(proj0)
    x2 = _mix0_ffn(a_out, proj0, ev_conv_w[0], bf(ev_w_out[0]), x, norm_ffn[0:1],
                   bf(ffn_w_gate[0]), bf(ffn_w_up[0]), bf(ffn_w_down[0])).reshape(n_tok, d)

    w_in1 = jnp.pad(od_w_in[0], ((0, 0), (0, ODD_IN_PAD - od_w_in.shape[2])))
    proj1 = _norm_matmul(x2, norm_mix[1:2], bf(w_in1)).reshape(b, s, ODD_IN_PAD)
    att = _dsa(proj1, od_q_gain[0:1], od_k_gain[0:1], _bias_tiles(rel_bias))
    return _proj_ffn(att, bf(od_w_out[0]), x2.reshape(b, s, d), norm_ffn[1:2],
                     bf(ffn_w_gate[1]), bf(ffn_w_up[1]), bf(ffn_w_down[1]))
```
